```python
import math
import jax, jax.numpy as jnp
from jax import lax
import numpy as np

D_MODEL = 2048
BATCH = 1
SEQ = 8192
DEPTH = 1

HEAD_DIM = 128
A_HEADS = 8
DIL_PATTERNS = ((128, 1), (512, 4), (2048, 16))
B_Q_HEADS = 8
B_KV_HEADS = 2
GRID_W = 64
ROPE_THETA = 10000.0
Q_BLOCK = 128
NORM_EPS = 1e-6

PEER_HEADS = 8
PEER_N_KEYS = 128
PEER_N_EXPERTS = PEER_N_KEYS * PEER_N_KEYS
PEER_QUERY_DIM = 256
PEER_TOPK = 16
TOKEN_BLOCK = 128

A_WIDTH = A_HEADS * HEAD_DIM
B_Q_WIDTH = B_Q_HEADS * HEAD_DIM
B_KV_WIDTH = B_KV_HEADS * HEAD_DIM
MIX_WIDTH = A_WIDTH + B_Q_WIDTH
IN_COLS = 3 * A_WIDTH + B_Q_WIDTH + 2 * B_KV_WIDTH

kernel_name = "hybrid_dilated_gqa_peer_encoder"


def rms_norm(x, g):
    xf = x.astype(jnp.float32)
    y = xf * lax.rsqrt(jnp.mean(xf * xf, axis=-1, keepdims=True) + NORM_EPS)
    return (y * g.astype(jnp.float32)).astype(x.dtype)


def dilated_offsets():
    rows = []
    for w, d in DIL_PATTERNS:
        n_side = (w // 2) // d
        rows.append(np.arange(-n_side, n_side + 1, dtype=np.int32) * d)
    return np.stack(rows, axis=0)


def alibi_slopes(n_heads):
    return 2.0 ** (-8.0 * (jnp.arange(n_heads, dtype=jnp.float32) + 1.0) / n_heads)


def dilated_window_attention(q, k, v):
    b, h, s, hd = q.shape
    offs_np = dilated_offsets()
    n_p, n_k = offs_np.shape
    offs = jnp.asarray(offs_np)
    bias = -alibi_slopes(h)[:, None, None] * jnp.abs(offs).astype(jnp.float32)[None]
    scale = HEAD_DIM ** -0.5
    nb = s // Q_BLOCK
    qb = q.reshape(b, h, nb, Q_BLOCK, hd).transpose(2, 0, 1, 3, 4)
    starts = jnp.arange(nb, dtype=jnp.int32) * Q_BLOCK

    def block(args):
        qblk, start = args
        t = start + jnp.arange(Q_BLOCK, dtype=jnp.int32)
        idx = t[:, None, None] + offs[None]
        valid = (idx >= 0) & (idx < s)
        idx_c = jnp.clip(idx, 0, s - 1).reshape(-1)
        kg = jnp.take(k, idx_c, axis=2).reshape(b, h, Q_BLOCK, n_p, n_k, hd)
        vg = jnp.take(v, idx_c, axis=2).reshape(b, h, Q_BLOCK, n_p, n_k, hd)
        sc = jnp.einsum('bhqd,bhqpkd->bhqpk', qblk, kg).astype(jnp.float32) * scale
        sc = sc + bias[None, :, None]
        sc = jnp.where(valid[None, None], sc, -jnp.inf)
        lse = jax.nn.logsumexp(sc, axis=-1)
        p = jnp.exp(sc - lse[..., None])
        o = jnp.einsum('bhqpk,bhqpkd->bhqpd', p.astype(v.dtype), vg)
        w = jax.nn.softmax(lse, axis=-1)
        return jnp.einsum('bhqp,bhqpd->bhqd', w.astype(v.dtype), o)

    out = lax.map(block, (qb, starts))
    return out.transpose(1, 2, 0, 3, 4).reshape(b, h, s, hd)


def axial_rope(s):
    rows = s // GRID_W
    row = jnp.repeat(jnp.arange(rows, dtype=jnp.float32), GRID_W)
    col = jnp.tile(jnp.arange(GRID_W, dtype=jnp.float32), rows)
    half = HEAD_DIM // 2
    inv = ROPE_THETA ** (-jnp.arange(0, half, 2, dtype=jnp.float32) / half)
    ang = jnp.concatenate([row[:, None] * inv, col[:, None] * inv], axis=-1)
    return jnp.cos(ang), jnp.sin(ang)


def apply_rope(x, cos, sin):
    xf = x.astype(jnp.float32).reshape(*x.shape[:-1], HEAD_DIM // 2, 2)
    x0, x1 = xf[..., 0], xf[..., 1]
    c = cos[None, :, None, :]
    sn = sin[None, :, None, :]
    out = jnp.stack([x0 * c - x1 * sn, x0 * sn + x1 * c], axis=-1)
    return out.reshape(x.shape).astype(x.dtype)


def gqa_block_attention(q, k, v):
    b, hq, s, hd = q.shape
    hkv = k.shape[1]
    g = hq // hkv
    scale = HEAD_DIM ** -0.5
    nb = s // Q_BLOCK
    qb = q.reshape(b, hkv, g, nb, Q_BLOCK, hd).transpose(3, 0, 1, 2, 4, 5)

    def block(qblk):
        sc = jnp.einsum('bkgqd,bksd->bkgqs', qblk, k).astype(jnp.float32) * scale
        p = jax.nn.softmax(sc, axis=-1)
        return jnp.einsum('bkgqs,bksd->bkgqd', p.astype(v.dtype), v)

    out = lax.map(block, qb)
    return out.transpose(1, 2, 3, 0, 4, 5).reshape(b, hq, s, hd)


def peer_ffn(x, w_query, sub_keys, expert_u, expert_v):
    b, s, d = x.shape
    n_tok = b * s
    xt = x.reshape(n_tok, d)
    q = (xt @ w_query).reshape(n_tok, PEER_HEADS, 2, PEER_QUERY_DIM // 2)
    scores = jnp.einsum('thcd,hcnd->thcn', q, sub_keys).astype(jnp.float32)
    s_top, i_top = lax.top_k(scores, PEER_TOPK)
    cand = s_top[:, :, 0, :, None] + s_top[:, :, 1, None, :]
    cand_idx = i_top[:, :, 0, :, None] * PEER_N_KEYS + i_top[:, :, 1, None, :]
    best, pos = lax.top_k(cand.reshape(n_tok, PEER_HEADS, PEER_TOPK * PEER_TOPK), PEER_TOPK)
    experts = jnp.take_along_axis(cand_idx.reshape(n_tok, PEER_HEADS, -1), pos, axis=-1)
    gates = jax.nn.softmax(best, axis=-1).astype(x.dtype)
    nb = n_tok // TOKEN_BLOCK

    def block(args):
        xb, eb, gb = args
        u = jnp.take(expert_u, eb, axis=0)
        vv = jnp.take(expert_v, eb, axis=0)
        act = jax.nn.gelu(jnp.einsum('td,thkd->thk', xb, u))
        return jnp.einsum('thk,thkd->td', act * gb, vv)

    out = lax.map(block, (xt.reshape(nb, TOKEN_BLOCK, d),
                          experts.reshape(nb, TOKEN_BLOCK, PEER_HEADS, PEER_TOPK),
                          gates.reshape(nb, TOKEN_BLOCK, PEER_HEADS, PEER_TOPK)))
    return out.reshape(b, s, d)


def setup_inputs(seed: int = 0) -> dict:
    key = jax.random.key(seed)
    ks = jax.random.split(key, 12)
    f32 = jnp.float32
    x = jax.random.normal(ks[0], (BATCH, SEQ, D_MODEL), f32)
    norm1_g = 1.0 + 0.02 * jax.random.normal(ks[1], (DEPTH, D_MODEL), f32)
    w_in = jax.random.normal(ks[2], (DEPTH, D_MODEL, IN_COLS), f32) * D_MODEL ** -0.5
    q_norm_g = 1.0 + 0.02 * jax.random.normal(ks[3], (DEPTH, HEAD_DIM), f32)
    k_norm_g = 1.0 + 0.02 * jax.random.normal(ks[4], (DEPTH, HEAD_DIM), f32)
    w_out = jax.random.normal(ks[5], (DEPTH, MIX_WIDTH, D_MODEL), f32) * MIX_WIDTH ** -0.5
    norm2_g = 1.0 + 0.02 * jax.random.normal(ks[6], (DEPTH, D_MODEL), f32)
    peer_w_query = jax.random.normal(ks[7], (DEPTH, D_MODEL, PEER_HEADS * PEER_QUERY_DIM), f32) * D_MODEL ** -0.5
    peer_sub_keys = jax.random.normal(ks[8], (DEPTH, PEER_HEADS, 2, PEER_N_KEYS, PEER_QUERY_DIM // 2), f32) * (PEER_QUERY_DIM // 2) ** -0.5
    peer_u = jax.random.normal(ks[9], (DEPTH, PEER_N_EXPERTS, D_MODEL), f32) * D_MODEL ** -0.5
    peer_v = jax.random.normal(ks[10], (DEPTH, PEER_N_EXPERTS, D_MODEL), f32) * PEER_HEADS ** -0.5
    final_norm_g = 1.0 + 0.02 * jax.random.normal(ks[11], (D_MODEL,), f32)
    return {"x": x, "norm1_g": norm1_g, "w_in": w_in, "q_norm_g": q_norm_g,
            "k_norm_g": k_norm_g, "w_out": w_out, "norm2_g": norm2_g,
            "peer_w_query": peer_w_query, "peer_sub_keys": peer_sub_keys,
            "peer_u": peer_u, "peer_v": peer_v, "final_norm_g": final_norm_g}


def reference(x, norm1_g, w_in, q_norm_g, k_norm_g, w_out, norm2_g,
              peer_w_query, peer_sub_keys, peer_u, peer_v, final_norm_g):
    b, s, _ = x.shape
    cos, sin = axial_rope(s)
    splits = [A_WIDTH, 2 * A_WIDTH, 3 * A_WIDTH, 3 * A_WIDTH + B_Q_WIDTH,
              3 * A_WIDTH + B_Q_WIDTH + B_KV_WIDTH]
    for l in range(DEPTH):
        h = rms_norm(x, norm1_g[l])
        proj = h @ w_in[l]
        qa, ka, va, qb, kb, vb = jnp.split(proj, splits, axis=-1)
        to_heads = lambda t, n: t.reshape(b, s, n, HEAD_DIM)
        qa_, ka_, va_ = [to_heads(t, A_HEADS).transpose(0, 2, 1, 3) for t in (qa, ka, va)]
        out_a = dilated_window_attention(qa_, ka_, va_)
        qb_ = apply_rope(rms_norm(to_heads(qb, B_Q_HEADS), q_norm_g[l]), cos, sin)
        kb_ = apply_rope(rms_norm(to_heads(kb, B_KV_HEADS), k_norm_g[l]), cos, sin)
        vb_ = to_heads(vb, B_KV_HEADS)
        out_b = gqa_block_attention(qb_.transpose(0, 2, 1, 3), kb_.transpose(0, 2, 1, 3),
                                    vb_.transpose(0, 2, 1, 3))
        mixed = jnp.concatenate([out_a.transpose(0, 2, 1, 3).reshape(b, s, A_WIDTH),
                                 out_b.transpose(0, 2, 1, 3).reshape(b, s, B_Q_WIDTH)], axis=-1)
        x = x + mixed @ w_out[l]
        x = x + peer_ffn(rms_norm(x, norm2_g[l]), peer_w_query[l], peer_sub_keys[l],
                         peer_u[l], peer_v[l])
    return rms_norm(x, final_norm_g)
```

```python
import functools

import numpy as np
import jax
import jax.numpy as jnp
from jax import lax
from jax.experimental import pallas as pl
from jax.experimental.pallas import tpu as pltpu

F32 = jnp.float32
BF16 = jnp.bfloat16

HEAD_DIM = 128
A_HEADS = 8
DIL_PATTERNS = ((128, 1), (512, 4), (2048, 16))
B_Q_HEADS = 8
B_KV_HEADS = 2
GRID_W = 64
ROPE_THETA = 10000.0
NORM_EPS = 1e-6
PEER_HEADS = 8
PEER_N_KEYS = 128
PEER_TOPK = 16

LANES = 128
NEG_BIG = -1e30
VMEM_LIMIT = 56 * 1024 * 1024


def _params(sem, vmem=VMEM_LIMIT):
    return pltpu.CompilerParams(dimension_semantics=sem, vmem_limit_bytes=vmem)


def _rms(x, g):
    ms = jnp.mean(x * x, axis=-1, keepdims=True)
    return x * lax.rsqrt(ms + NORM_EPS) * g


def _inproj_kernel(x_ref, g_ref, w_ref, o_ref, h_ref):
    @pl.when(pl.program_id(1) == 0)
    def _():
        h_ref[...] = _rms(x_ref[...], g_ref[...]).astype(BF16)

    o_ref[...] = jnp.dot(h_ref[...], w_ref[...], preferred_element_type=F32)


def in_projection(x, g, w_bf16, tm=1024, tn=512):
    s, d = x.shape
    n = w_bf16.shape[1]
    return pl.pallas_call(
        _inproj_kernel,
        grid=(s // tm, n // tn),
        in_specs=[pl.BlockSpec((tm, d), lambda i, j: (i, 0)),
                  pl.BlockSpec((1, d), lambda i, j: (0, 0)),
                  pl.BlockSpec((d, tn), lambda i, j: (0, j))],
        out_specs=pl.BlockSpec((tm, tn), lambda i, j: (i, j)),
        out_shape=jax.ShapeDtypeStruct((s, n), F32),
        scratch_shapes=[pltpu.VMEM((tm, d), BF16)],
        compiler_params=_params(("parallel", "arbitrary")),
        name="in_projection",
    )(x, g.reshape(1, d), w_bf16)


DIL_QBLK = 2048
DIL_SUB = 128
DIL_KWIN = 256


def _dilated_kernel(q_ref, k_ref, v_ref, o_ref, m_ref, l_ref, acc_ref, *, seq):
    h = pl.program_id(0)
    base = pl.program_id(1) * DIL_QBLK
    slope = jnp.exp2(-(h + 1).astype(F32))
    scale = HEAD_DIM ** -0.5

    m_ref[...] = jnp.full(m_ref.shape, NEG_BIG, F32)
    l_ref[...] = jnp.zeros(l_ref.shape, F32)
    acc_ref[...] = jnp.zeros(acc_ref.shape, F32)

    row = lax.broadcasted_iota(jnp.int32, (DIL_SUB, DIL_KWIN), 0)
    col = lax.broadcasted_iota(jnp.int32, (DIL_SUB, DIL_KWIN), 1)
    col_minus_row = col - row

    for w, d in DIL_PATTERNS:
        half = (w // 2) // d
        assert half * 2 + DIL_SUB == DIL_KWIN
        seg_len = seq // d

        def rows(start, n, d=d):
            return pl.ds(start, n, stride=d) if d > 1 else pl.ds(start, n)

        def body(c, carry, d=d, half=half, seg_len=seg_len, rows=rows):
            r = c % d
            cc = c // d
            p0 = base // d + cc * DIL_SUB
            kp = jnp.clip(p0 - half, 0, seg_len - DIL_KWIN)
            lq = r + d * (cc * DIL_SUB)
            ks = r + d * kp

            q = (q_ref[rows(lq, DIL_SUB), :] * scale).astype(BF16)
            k = k_ref[rows(ks, DIL_KWIN), :].astype(BF16)
            v = v_ref[rows(ks, DIL_KWIN), :].astype(BF16)
            s = lax.dot_general(q, k, (((1,), (1,)), ((), ())), preferred_element_type=F32)
            rel = jnp.abs(col_minus_row + (kp - p0))
            s = jnp.where(rel <= half, s - (slope * d) * rel.astype(F32), NEG_BIG)
            mb = jnp.max(s, axis=1, keepdims=True)
            p = jnp.exp(s - mb)
            lb = jnp.sum(p, axis=1, keepdims=True)
            ob = jnp.dot(p.astype(BF16), v, preferred_element_type=F32)

            idx = rows(lq, DIL_SUB)
            m_old = m_ref[idx, :]
            m_new = jnp.maximum(m_old, mb)
            a_old = jnp.exp(m_old - m_new)
            a_blk = jnp.exp(mb - m_new)
            m_ref[idx, :] = m_new
            l_ref[idx, :] = a_old * l_ref[idx, :] + a_blk * lb
            acc_ref[idx, :] = a_old * acc_ref[idx, :] + a_blk * ob
            return carry

        lax.fori_loop(0, DIL_QBLK // DIL_SUB, body, 0)

    o_ref[...] = (acc_ref[...] / l_ref[...]).astype(o_ref.dtype)


def dilated_attention(proj, seq):
    assert seq % DIL_QBLK == 0
    for _, d in DIL_PATTERNS:
        assert DIL_QBLK // d >= DIL_SUB and seq // d >= DIL_KWIN
    nh = A_HEADS
    return pl.pallas_call(
        functools.partial(_dilated_kernel, seq=seq),
        grid=(nh, seq // DIL_QBLK),
        in_specs=[pl.BlockSpec((DIL_QBLK, HEAD_DIM), lambda h, i: (i, h)),
                  pl.BlockSpec((seq, HEAD_DIM), lambda h, i: (0, nh + h)),
                  pl.BlockSpec((seq, HEAD_DIM), lambda h, i: (0, 2 * nh + h))],
        out_specs=pl.BlockSpec((DIL_QBLK, HEAD_DIM), lambda h, i: (i, h)),
        out_shape=jax.ShapeDtypeStruct((seq, nh * HEAD_DIM), BF16),
        scratch_shapes=[pltpu.VMEM((DIL_QBLK, LANES), F32),
                        pltpu.VMEM((DIL_QBLK, LANES), F32),
                        pltpu.VMEM((DIL_QBLK, HEAD_DIM), F32)],
        compiler_params=_params(("parallel", "arbitrary")),
        name="dilated_attention",
    )(proj, proj, proj)


def _rope_tables(seq):
    rows = seq // GRID_W
    row = jnp.repeat(jnp.arange(rows, dtype=F32), GRID_W)
    col = jnp.tile(jnp.arange(GRID_W, dtype=F32), rows)
    half = HEAD_DIM // 2
    inv = ROPE_THETA ** (-jnp.arange(0, half, 2, dtype=F32) / half)
    ang = jnp.concatenate([row[:, None] * inv, col[:, None] * inv], axis=-1)
    cos = jnp.repeat(jnp.cos(ang), 2, axis=-1)
    sign = jnp.tile(jnp.asarray([-1.0, 1.0], F32), half)
    sin = jnp.repeat(jnp.sin(ang), 2, axis=-1) * sign
    return cos, sin


def _gqa_prep_kernel(p_ref, cos_ref, sin_ref, qg_ref, kg_ref, qt_ref, k_ref, vt_ref):
    cos = cos_ref[...]
    sin = sin_ref[...]
    even = (lax.broadcasted_iota(jnp.int32, cos.shape, 1) % 2) == 0

    def norm_rope(xh, g):
        y = _rms(xh, g)
        partner = jnp.where(even, pltpu.roll(y, LANES - 1, 1), pltpu.roll(y, 1, 1))
        return y * cos + partner * sin

    scale = HEAD_DIM ** -0.5
    for hq in range(B_Q_HEADS):
        xh = p_ref[:, hq * HEAD_DIM:(hq + 1) * HEAD_DIM]
        y = norm_rope(xh, qg_ref[...]) * scale
        qt_ref[hq * HEAD_DIM:(hq + 1) * HEAD_DIM, :] = y.T.astype(BF16)
    k0 = B_Q_HEADS * HEAD_DIM
    v0 = k0 + B_KV_HEADS * HEAD_DIM
    for hk in range(B_KV_HEADS):
        xh = p_ref[:, k0 + hk * HEAD_DIM:k0 + (hk + 1) * HEAD_DIM]
        k_ref[:, hk * HEAD_DIM:(hk + 1) * HEAD_DIM] = norm_rope(xh, kg_ref[...]).astype(BF16)
        vh = p_ref[:, v0 + hk * HEAD_DIM:v0 + (hk + 1) * HEAD_DIM]
        vt_ref[hk * HEAD_DIM:(hk + 1) * HEAD_DIM, :] = vh.T.astype(BF16)


def gqa_prep(proj, q_g, k_g, seq, col_block, ts=512):
    wb = (B_Q_HEADS + 2 * B_KV_HEADS) * HEAD_DIM
    cos, sin = _rope_tables(seq)
    return pl.pallas_call(
        _gqa_prep_kernel,
        grid=(seq // ts,),
        in_specs=[pl.BlockSpec((ts, wb), lambda i: (i, col_block)),
                  pl.BlockSpec((ts, HEAD_DIM), lambda i: (i, 0)),
                  pl.BlockSpec((ts, HEAD_DIM), lambda i: (i, 0)),
                  pl.BlockSpec((1, HEAD_DIM), lambda i: (0, 0)),
                  pl.BlockSpec((1, HEAD_DIM), lambda i: (0, 0))],
        out_specs=[pl.BlockSpec((B_Q_HEADS * HEAD_DIM, ts), lambda i: (0, i)),
                   pl.BlockSpec((ts, B_KV_HEADS * HEAD_DIM), lambda i: (i, 0)),
                   pl.BlockSpec((B_KV_HEADS * HEAD_DIM, ts), lambda i: (0, i))],
        out_shape=[jax.ShapeDtypeStruct((B_Q_HEADS * HEAD_DIM, seq), BF16),
                   jax.ShapeDtypeStruct((seq, B_KV_HEADS * HEAD_DIM), BF16),
                   jax.ShapeDtypeStruct((B_KV_HEADS * HEAD_DIM, seq), BF16)],
        compiler_params=_params(("parallel",)),
        name="gqa_prep",
    )(proj, cos, sin, q_g.reshape(1, HEAD_DIM), k_g.reshape(1, HEAD_DIM))


def _gqa_kernel(qt_ref, k_ref, vt_ref, o_ref, m_ref, l_ref, acc_ref, *, seq, tk):
    m_ref[...] = jnp.full(m_ref.shape, NEG_BIG, F32)
    l_ref[...] = jnp.zeros(l_ref.shape, F32)
    acc_ref[...] = jnp.zeros(acc_ref.shape, F32)
    qt = qt_ref[...]

    def body(c, carry):
        start = pl.multiple_of(c * tk, tk)
        k = k_ref[pl.ds(start, tk), :]
        st = jnp.dot(k, qt, preferred_element_type=F32)
        m_old = m_ref[...]
        m_new = jnp.maximum(m_old, jnp.max(st, axis=0, keepdims=True))
        alpha = jnp.exp(m_old - m_new)
        p = jnp.exp(st - m_new)
        l_ref[...] = alpha * l_ref[...] + jnp.sum(p, axis=0, keepdims=True)
        vt = vt_ref[:, pl.ds(start, tk)]
        acc_ref[...] = alpha * acc_ref[...] + jnp.dot(vt, p.astype(BF16),
                                                      preferred_element_type=F32)
        m_ref[...] = m_new
        return carry

    lax.fori_loop(0, seq // tk, body, 0)
    o_ref[...] = (acc_ref[...] / l_ref[...]).T.astype(o_ref.dtype)


def gqa_attention(qt, k, vt, seq, tq=512, tk=512):
    group = B_Q_HEADS // B_KV_HEADS
    return pl.pallas_call(
        functools.partial(_gqa_kernel, seq=seq, tk=tk),
        grid=(B_Q_HEADS, seq // tq),
        in_specs=[pl.BlockSpec((HEAD_DIM, tq), lambda h, i: (h, i)),
                  pl.BlockSpec((seq, HEAD_DIM), lambda h, i: (0, h // group)),
                  pl.BlockSpec((HEAD_DIM, seq), lambda h, i: (h // group, 0))],
        out_specs=pl.BlockSpec((tq, HEAD_DIM), lambda h, i: (i, h)),
        out_shape=jax.ShapeDtypeStruct((seq, B_Q_HEADS * HEAD_DIM), BF16),
        scratch_shapes=[pltpu.VMEM((1, tq), F32),
                        pltpu.VMEM((1, tq), F32),
                        pltpu.VMEM((HEAD_DIM, tq), F32)],
        compiler_params=_params(("parallel", "arbitrary")),
        name="gqa_attention",
    )(qt, k, vt)


def _outproj_kernel(ma_ref, mb_ref, wa_ref, wb_ref, x_ref, g_ref, x1_ref, ht_ref):
    y = x_ref[...]
    y = y + jnp.dot(ma_ref[...], wa_ref[...], preferred_element_type=F32)
    y = y + jnp.dot(mb_ref[...], wb_ref[...], preferred_element_type=F32)
    x1_ref[...] = y
    ht_ref[...] = _rms(y, g_ref[...]).T.astype(BF16)


def out_projection(mixed_a, mixed_b, w_a, w_b, x, g, tm=512):
    s, d = x.shape
    wa, wb = mixed_a.shape[1], mixed_b.shape[1]
    return pl.pallas_call(
        _outproj_kernel,
        grid=(s // tm,),
        in_specs=[pl.BlockSpec((tm, wa), lambda i: (i, 0)),
                  pl.BlockSpec((tm, wb), lambda i: (i, 0)),
                  pl.BlockSpec((wa, d), lambda i: (0, 0)),
                  pl.BlockSpec((wb, d), lambda i: (0, 0)),
                  pl.BlockSpec((tm, d), lambda i: (i, 0)),
                  pl.BlockSpec((1, d), lambda i: (0, 0))],
        out_specs=[pl.BlockSpec((tm, d), lambda i: (i, 0)),
                   pl.BlockSpec((d, tm), lambda i: (0, i))],
        out_shape=[jax.ShapeDtypeStruct((s, d), F32),
                   jax.ShapeDtypeStruct((d, s), BF16)],
        compiler_params=_params(("parallel",)),
        name="out_projection",
    )(mixed_a, mixed_b, w_a, w_b, x, g.reshape(1, d))


N_RANK = PEER_TOPK + 1
RANK_ROWS = 24
CAND_COUNTS = tuple(N_RANK // (a + 1) for a in range(N_RANK))
CAND_ROWS = -(-sum(CAND_COUNTS) // 8) * 8


def _extract_sorted(work, n, emit):
    for j in range(n):
        mj = jnp.max(work, axis=0, keepdims=True)
        emit(j, mj)
        if j + 1 < n:
            work = jnp.where(work == mj, -jnp.inf, work)


def _router_kernel(ht_ref, wqt_ref, keys_ref, thr_ref, w0_ref, s1_ref, e1_ref,
                   q_scr, s_scr, top_scr, cand_scr):
    q_scr[...] = jnp.dot(wqt_ref[...], ht_ref[...], preferred_element_type=F32).astype(BF16)

    def half_body(hc, carry):
        qhc = q_scr[pl.ds(pl.multiple_of(hc * PEER_N_KEYS, PEER_N_KEYS), PEER_N_KEYS), :]
        s = jnp.dot(keys_ref[hc], qhc, preferred_element_type=F32)
        s_scr[hc] = s

        def emit(j, mj):
            top_scr[hc, pl.ds(j, 1), :] = mj

        _extract_sorted(s, N_RANK, emit)
        return carry

    lax.fori_loop(0, 2 * PEER_HEADS, half_body, 0)

    def head_body(h, carry):
        t0 = top_scr[2 * h]
        t1 = top_scr[2 * h + 1]
        cand_scr[...] = jnp.full(cand_scr.shape, -jnp.inf, F32)
        off = 0
        for a, nb in enumerate(CAND_COUNTS):
            cand_scr[off:off + nb, :] = t0[a:a + 1, :] + t1[0:nb, :]
            off += nb

        best = []
        _extract_sorted(cand_scr[...], N_RANK, lambda j, mj: best.append(mj))
        theta = 0.5 * (best[PEER_TOPK - 1] + best[PEER_TOPK])
        z = jnp.zeros_like(theta)
        for j in range(PEER_TOPK):
            z = z + jnp.exp(best[j] - best[0])

        s0 = s_scr[2 * h]
        s1 = s_scr[2 * h + 1]
        thr_ref[h] = theta - s0
        w0_ref[h] = jnp.exp(s0 - t0[0:1, :]) / z
        s1_ref[h] = s1
        e1_ref[h] = jnp.exp(s1 - t1[0:1, :])
        return carry

    lax.fori_loop(0, PEER_HEADS, head_body, 0)


def peer_router(ht, wqt, keys, tb=512):
    d, s = ht.shape
    nq = wqt.shape[0]
    tab = jax.ShapeDtypeStruct((PEER_HEADS, PEER_N_KEYS, s), F32)
    tab_spec = pl.BlockSpec((PEER_HEADS, PEER_N_KEYS, tb), lambda i: (0, 0, i))
    return pl.pallas_call(
        _router_kernel,
        grid=(s // tb,),
        in_specs=[pl.BlockSpec((d, tb), lambda i: (0, i)),
                  pl.BlockSpec((nq, d), lambda i: (0, 0)),
                  pl.BlockSpec(keys.shape, lambda i: (0, 0, 0))],
        out_specs=[tab_spec] * 4,
        out_shape=[tab] * 4,
        scratch_shapes=[pltpu.VMEM((nq, tb), BF16),
                        pltpu.VMEM((2 * PEER_HEADS, PEER_N_KEYS, tb), F32),
                        pltpu.VMEM((2 * PEER_HEADS, RANK_ROWS, tb), F32),
                        pltpu.VMEM((CAND_ROWS, tb), F32)],
        compiler_params=_params(("parallel",)),
        name="peer_router",
    )(ht, wqt, keys)


def _gelu_tanh(x):
    return 0.5 * x * (1.0 + jnp.tanh(0.7978845608028654 * (x + 0.044715 * (x * x * x))))


def _peer_kernel(ht_ref, u_ref, v_ref, thr_ref, w0_ref, s1_ref, e1_ref, x1_ref, g_ref,
                 o_ref, acc_ref, *, n_i0):
    e = pl.program_id(1)

    @pl.when(e == 0)
    def _():
        acc_ref[...] = jnp.zeros(acc_ref.shape, F32)

    at = jnp.dot(u_ref[...], ht_ref[...], preferred_element_type=F32)
    act = _gelu_tanh(at)
    parts = []
    for ii in range(n_i0):
        i0 = e * n_i0 + ii
        gate = jnp.zeros((PEER_N_KEYS, at.shape[1]), F32)
        for h in range(PEER_HEADS):
            thr = thr_ref[h, pl.ds(i0, 1), :]
            w0 = w0_ref[h, pl.ds(i0, 1), :]
            gate = gate + jnp.where(s1_ref[h] >= thr, e1_ref[h] * w0, 0.0)
        parts.append(gate)
    gate_t = parts[0] if n_i0 == 1 else jnp.concatenate(parts, axis=0)
    pt = (act * gate_t).astype(BF16)
    acc_ref[...] += lax.dot_general(pt, v_ref[...], (((0,), (0,)), ((), ())),
                                    preferred_element_type=F32)

    @pl.when(e == pl.num_programs(1) - 1)
    def _():
        o_ref[...] = _rms(x1_ref[...] + acc_ref[...], g_ref[...])


def peer_experts(ht, u, v, tables, x1, g, tb=512, te=256):
    d, s = ht.shape
    n_exp = u.shape[0]
    n_i0 = te // PEER_N_KEYS
    tab_spec = pl.BlockSpec((PEER_HEADS, PEER_N_KEYS, tb), lambda i, e: (0, 0, i))
    return pl.pallas_call(
        functools.partial(_peer_kernel, n_i0=n_i0),
        grid=(s // tb, n_exp // te),
        in_specs=[pl.BlockSpec((d, tb), lambda i, e: (0, i)),
                  pl.BlockSpec((te, d), lambda i, e: (e, 0)),
                  pl.BlockSpec((te, d), lambda i, e: (e, 0)),
                  tab_spec, tab_spec, tab_spec, tab_spec,
                  pl.BlockSpec((tb, d), lambda i, e: (i, 0)),
                  pl.BlockSpec((1, d), lambda i, e: (0, 0))],
        out_specs=pl.BlockSpec((tb, d), lambda i, e: (i, 0)),
        out_shape=jax.ShapeDtypeStruct((s, d), F32),
        scratch_shapes=[pltpu.VMEM((tb, d), F32)],
        compiler_params=_params(("parallel", "arbitrary")),
        name="peer_experts",
    )(ht, u, v, *tables, x1, g.reshape(1, d))


def kernel(x, norm1_g, w_in, q_norm_g, k_norm_g, w_out, norm2_g, peer_w_query,
           peer_sub_keys, peer_u, peer_v, final_norm_g):
    b, s, d = x.shape
    assert b == 1
    a_width = A_HEADS * HEAD_DIM
    depth = w_in.shape[0]
    xs = x.reshape(s, d)
    for l in range(depth):
        proj = in_projection(xs, norm1_g[l], w_in[l].astype(BF16))
        mixed_a = dilated_attention(proj, s)
        qt, kb, vt = gqa_prep(proj, q_norm_g[l], k_norm_g[l], s, col_block=2)
        mixed_b = gqa_attention(qt, kb, vt, s)
        w_o = w_out[l].astype(BF16)
        x1, ht = out_projection(mixed_a, mixed_b, w_o[:a_width], w_o[a_width:], xs, norm2_g[l])
        keys = peer_sub_keys[l].reshape(2 * PEER_HEADS, PEER_N_KEYS, -1).astype(BF16)
        tables = peer_router(ht, peer_w_query[l].T.astype(BF16), keys)
        assert l == depth - 1
        xs = peer_experts(ht, peer_u[l].astype(BF16), peer_v[l].astype(BF16), tables, x1,
                          final_norm_g)
    return xs.reshape(b, s, d)
```

```python
import functools

import numpy as np
import jax
import jax.numpy as jnp
from jax import lax
from jax.experimental import pallas as pl
from jax.experimental.pallas import tpu as pltpu

F32 = jnp.float32
BF16 = jnp.bfloat16

HEAD_DIM = 128
A_HEADS = 8
DIL_PATTERNS = ((128, 1), (512, 4), (2048, 16))
B_Q_HEADS = 8
B_KV_HEADS = 2
GRID_W = 64
ROPE_THETA = 10000.0
NORM_EPS = 1e-6
PEER_HEADS = 8
PEER_N_KEYS = 128
PEER_TOPK = 16

LANES = 128
NEG_BIG = -1e30
LOG2_E = 1.4426950408889634
VMEM_LIMIT = 56 * 1024 * 1024


def _params(sem, vmem=VMEM_LIMIT):
    return pltpu.CompilerParams(dimension_semantics=sem, vmem_limit_bytes=vmem)


def _rms(x, g):
    ms = jnp.mean(x * x, axis=-1, keepdims=True)
    return x * lax.rsqrt(ms + NORM_EPS) * g


def _inproj_kernel(x_ref, g_ref, w_ref, o_ref, h_ref):
    @pl.when(pl.program_id(1) == 0)
    def _():
        h_ref[...] = _rms(x_ref[...], g_ref[...]).astype(BF16)

    o_ref[...] = jnp.dot(h_ref[...], w_ref[...], preferred_element_type=F32)


def in_projection(x, g, w_bf16, tm=1024, tn=512):
    s, d = x.shape
    n = w_bf16.shape[1]
    return pl.pallas_call(
        _inproj_kernel,
        grid=(s // tm, n // tn),
        in_specs=[pl.BlockSpec((tm, d), lambda i, j: (i, 0)),
                  pl.BlockSpec((1, d), lambda i, j: (0, 0)),
                  pl.BlockSpec((d, tn), lambda i, j: (0, j))],
        out_specs=pl.BlockSpec((tm, tn), lambda i, j: (i, j)),
        out_shape=jax.ShapeDtypeStruct((s, n), F32),
        scratch_shapes=[pltpu.VMEM((tm, d), BF16)],
        compiler_params=_params(("parallel", "arbitrary")),
        name="in_projection",
    )(x, g.reshape(1, d), w_bf16)


DIL_QBLK = 2048
DIL_SUB = 128
DIL_KWIN = 256
DIL_UNROLL = 4


def _dilated_kernel(q_ref, k_ref, v_ref, o_ref, m_ref, l_ref, acc_ref, *, seq):
    h = pl.program_id(0)
    base = pl.program_id(1) * DIL_QBLK
    slope = jnp.exp2(-(h + 1).astype(F32))
    scale = HEAD_DIM ** -0.5

    m_ref[...] = jnp.full(m_ref.shape, NEG_BIG, F32)
    l_ref[...] = jnp.zeros(l_ref.shape, F32)
    acc_ref[...] = jnp.zeros(acc_ref.shape, F32)

    row = lax.broadcasted_iota(jnp.int32, (DIL_SUB, DIL_KWIN), 0)
    col = lax.broadcasted_iota(jnp.int32, (DIL_SUB, DIL_KWIN), 1)
    col_minus_row = col - row

    for w, d in DIL_PATTERNS:
        half = (w // 2) // d
        assert half * 2 + DIL_SUB == DIL_KWIN
        seg_len = seq // d

        def rows(start, n, d=d):
            return pl.ds(start, n, stride=d) if d > 1 else pl.ds(start, n)

        def body(c, carry, d=d, half=half, seg_len=seg_len, rows=rows):
            r = c % d
            cc = c // d
            p0 = base // d + cc * DIL_SUB
            kp = jnp.clip(p0 - half, 0, seg_len - DIL_KWIN)
            lq = r + d * (cc * DIL_SUB)
            ks = r + d * kp

            q = (q_ref[rows(lq, DIL_SUB), :] * scale).astype(BF16)
            k = k_ref[rows(ks, DIL_KWIN), :].astype(BF16)
            v = v_ref[rows(ks, DIL_KWIN), :].astype(BF16)
            s = lax.dot_general(q, k, (((1,), (1,)), ((), ())), preferred_element_type=F32)
            rel = jnp.abs(col_minus_row + (kp - p0))
            s = jnp.where(rel <= half, s - (slope * d) * rel.astype(F32), NEG_BIG)
            mb = jnp.max(s, axis=1, keepdims=True)
            p = jnp.exp(s - mb)
            lb = jnp.sum(p, axis=1, keepdims=True)
            ob = jnp.dot(p.astype(BF16), v, preferred_element_type=F32)

            idx = rows(lq, DIL_SUB)
            m_old = m_ref[idx, :]
            m_new = jnp.maximum(m_old, mb)
            a_old = jnp.exp(m_old - m_new)
            a_blk = jnp.exp(mb - m_new)
            m_ref[idx, :] = m_new
            l_ref[idx, :] = a_old * l_ref[idx, :] + a_blk * lb
            acc_ref[idx, :] = a_old * acc_ref[idx, :] + a_blk * ob
            return carry

        def group(cg, carry, body=body):
            for u in range(DIL_UNROLL):
                body(cg * DIL_UNROLL + u, carry)
            return carry

        lax.fori_loop(0, DIL_QBLK // DIL_SUB // DIL_UNROLL, group, 0)

    o_ref[...] = (acc_ref[...] / l_ref[...]).astype(o_ref.dtype)


def dilated_attention(proj, seq):
    assert seq % DIL_QBLK == 0
    for _, d in DIL_PATTERNS:
        assert DIL_QBLK // d >= DIL_SUB and seq // d >= DIL_KWIN
    nh = A_HEADS
    return pl.pallas_call(
        functools.partial(_dilated_kernel, seq=seq),
        grid=(nh, seq // DIL_QBLK),
        in_specs=[pl.BlockSpec((DIL_QBLK, HEAD_DIM), lambda h, i: (i, h)),
                  pl.BlockSpec((seq, HEAD_DIM), lambda h, i: (0, nh + h)),
                  pl.BlockSpec((seq, HEAD_DIM), lambda h, i: (0, 2 * nh + h))],
        out_specs=pl.BlockSpec((DIL_QBLK, HEAD_DIM), lambda h, i: (i, h)),
        out_shape=jax.ShapeDtypeStruct((seq, nh * HEAD_DIM), BF16),
        scratch_shapes=[pltpu.VMEM((DIL_QBLK, LANES), F32),
                        pltpu.VMEM((DIL_QBLK, LANES), F32),
                        pltpu.VMEM((DIL_QBLK, HEAD_DIM), F32)],
        compiler_params=_params(("parallel", "arbitrary")),
        name="dilated_attention",
    )(proj, proj, proj)


def _rope_tables(seq):
    rows = seq // GRID_W
    row = jnp.repeat(jnp.arange(rows, dtype=F32), GRID_W)
    col = jnp.tile(jnp.arange(GRID_W, dtype=F32), rows)
    half = HEAD_DIM // 2
    inv = ROPE_THETA ** (-jnp.arange(0, half, 2, dtype=F32) / half)
    ang = jnp.concatenate([row[:, None] * inv, col[:, None] * inv], axis=-1)
    cos = jnp.repeat(jnp.cos(ang), 2, axis=-1)
    sign = jnp.tile(jnp.asarray([-1.0, 1.0], F32), half)
    sin = jnp.repeat(jnp.sin(ang), 2, axis=-1) * sign
    return cos, sin


def _gqa_prep_kernel(p_ref, cos_ref, sin_ref, qg_ref, kg_ref, qt_ref, k_ref, vt_ref):
    cos = cos_ref[...]
    sin = sin_ref[...]
    even = (lax.broadcasted_iota(jnp.int32, cos.shape, 1) % 2) == 0

    def norm_rope(xh, g):
        y = _rms(xh, g)
        partner = jnp.where(even, pltpu.roll(y, LANES - 1, 1), pltpu.roll(y, 1, 1))
        return y * cos + partner * sin

    scale = HEAD_DIM ** -0.5 * LOG2_E
    for hq in range(B_Q_HEADS):
        xh = p_ref[:, hq * HEAD_DIM:(hq + 1) * HEAD_DIM]
        y = norm_rope(xh, qg_ref[...]) * scale
        qt_ref[hq * HEAD_DIM:(hq + 1) * HEAD_DIM, :] = y.T.astype(BF16)
    k0 = B_Q_HEADS * HEAD_DIM
    v0 = k0 + B_KV_HEADS * HEAD_DIM
    for hk in range(B_KV_HEADS):
        xh = p_ref[:, k0 + hk * HEAD_DIM:k0 + (hk + 1) * HEAD_DIM]
        k_ref[:, hk * HEAD_DIM:(hk + 1) * HEAD_DIM] = norm_rope(xh, kg_ref[...]).astype(BF16)
        vh = p_ref[:, v0 + hk * HEAD_DIM:v0 + (hk + 1) * HEAD_DIM]
        vt_ref[hk * HEAD_DIM:(hk + 1) * HEAD_DIM, :] = vh.T.astype(BF16)


def gqa_prep(proj, q_g, k_g, seq, col_block, ts=512):
    wb = (B_Q_HEADS + 2 * B_KV_HEADS) * HEAD_DIM
    cos, sin = _rope_tables(seq)
    return pl.pallas_call(
        _gqa_prep_kernel,
        grid=(seq // ts,),
        in_specs=[pl.BlockSpec((ts, wb), lambda i: (i, col_block)),
                  pl.BlockSpec((ts, HEAD_DIM), lambda i: (i, 0)),
                  pl.BlockSpec((ts, HEAD_DIM), lambda i: (i, 0)),
                  pl.BlockSpec((1, HEAD_DIM), lambda i: (0, 0)),
                  pl.BlockSpec((1, HEAD_DIM), lambda i: (0, 0))],
        out_specs=[pl.BlockSpec((B_Q_HEADS * HEAD_DIM, ts), lambda i: (0, i)),
                   pl.BlockSpec((ts, B_KV_HEADS * HEAD_DIM), lambda i: (i, 0)),
                   pl.BlockSpec((B_KV_HEADS * HEAD_DIM, ts), lambda i: (0, i))],
        out_shape=[jax.ShapeDtypeStruct((B_Q_HEADS * HEAD_DIM, seq), BF16),
                   jax.ShapeDtypeStruct((seq, B_KV_HEADS * HEAD_DIM), BF16),
                   jax.ShapeDtypeStruct((B_KV_HEADS * HEAD_DIM, seq), BF16)],
        compiler_params=_params(("parallel",)),
        name="gqa_prep",
    )(proj, cos, sin, q_g.reshape(1, HEAD_DIM), k_g.reshape(1, HEAD_DIM))


def _gqa_kernel(qt_ref, k_ref, vt_ref, o_ref, m_ref, l_ref, acc_ref, *, seq, tk, nh):
    m_ref[...] = jnp.full(m_ref.shape, NEG_BIG, F32)
    l_ref[...] = jnp.zeros(l_ref.shape, F32)
    acc_ref[...] = jnp.zeros(acc_ref.shape, F32)

    def body(c, carry):
        start = pl.multiple_of(c * tk, tk)
        k = k_ref[pl.ds(start, tk), :]
        vt = vt_ref[:, pl.ds(start, tk)]
        for hh in range(nh):
            qt = qt_ref[hh * HEAD_DIM:(hh + 1) * HEAD_DIM, :]
            st = jnp.dot(k, qt, preferred_element_type=F32)
            m_old = m_ref[hh]
            m_new = jnp.maximum(m_old, jnp.max(st, axis=0, keepdims=True))
            alpha = jnp.exp2(m_old - m_new)
            p = jnp.exp2(st - m_new)
            l_ref[hh] = alpha * l_ref[hh] + jnp.sum(p, axis=0, keepdims=True)
            acc_ref[hh] = alpha * acc_ref[hh] + jnp.dot(vt, p.astype(BF16),
                                                        preferred_element_type=F32)
            m_ref[hh] = m_new
        return carry

    lax.fori_loop(0, seq // tk, body, 0)
    for hh in range(nh):
        o_ref[:, hh * HEAD_DIM:(hh + 1) * HEAD_DIM] = (acc_ref[hh] / l_ref[hh]).T.astype(o_ref.dtype)


def gqa_attention(qt, k, vt, seq, tq=512, tk=512):
    nh = B_Q_HEADS // B_KV_HEADS
    return pl.pallas_call(
        functools.partial(_gqa_kernel, seq=seq, tk=tk, nh=nh),
        grid=(B_KV_HEADS, seq // tq),
        in_specs=[pl.BlockSpec((nh * HEAD_DIM, tq), lambda g, i: (g, i)),
                  pl.BlockSpec((seq, HEAD_DIM), lambda g, i: (0, g)),
                  pl.BlockSpec((HEAD_DIM, seq), lambda g, i: (g, 0))],
        out_specs=pl.BlockSpec((tq, nh * HEAD_DIM), lambda g, i: (i, g)),
        out_shape=jax.ShapeDtypeStruct((seq, B_Q_HEADS * HEAD_DIM), BF16),
        scratch_shapes=[pltpu.VMEM((nh, 1, tq), F32),
                        pltpu.VMEM((nh, 1, tq), F32),
                        pltpu.VMEM((nh, HEAD_DIM, tq), F32)],
        compiler_params=_params(("parallel", "arbitrary")),
        name="gqa_attention",
    )(qt, k, vt)


def _outproj_kernel(ma_ref, mb_ref, wa_ref, wb_ref, x_ref, g_ref, x1_ref, ht_ref):
    y = x_ref[...]
    y = y + jnp.dot(ma_ref[...], wa_ref[...], preferred_element_type=F32)
    y = y + jnp.dot(mb_ref[...], wb_ref[...], preferred_element_type=F32)
    x1_ref[...] = y
    ht_ref[...] = _rms(y, g_ref[...]).T.astype(BF16)


def out_projection(mixed_a, mixed_b, w_a, w_b, x, g, tm=512):
    s, d = x.shape
    wa, wb = mixed_a.shape[1], mixed_b.shape[1]
    return pl.pallas_call(
        _outproj_kernel,
        grid=(s // tm,),
        in_specs=[pl.BlockSpec((tm, wa), lambda i: (i, 0)),
                  pl.BlockSpec((tm, wb), lambda i: (i, 0)),
                  pl.BlockSpec((wa, d), lambda i: (0, 0)),
                  pl.BlockSpec((wb, d), lambda i: (0, 0)),
                  pl.BlockSpec((tm, d), lambda i: (i, 0)),
                  pl.BlockSpec((1, d), lambda i: (0, 0))],
        out_specs=[pl.BlockSpec((tm, d), lambda i: (i, 0)),
                   pl.BlockSpec((d, tm), lambda i: (0, i))],
        out_shape=[jax.ShapeDtypeStruct((s, d), F32),
                   jax.ShapeDtypeStruct((d, s), BF16)],
        compiler_params=_params(("parallel",)),
        name="out_projection",
    )(mixed_a, mixed_b, w_a, w_b, x, g.reshape(1, d))


N_RANK = PEER_TOPK + 1
RANK_ROWS = 24
CAND_COUNTS = tuple(N_RANK // (a + 1) for a in range(N_RANK))
CAND_ROWS = -(-sum(CAND_COUNTS) // 8) * 8


def _extract_sorted(work, n, emit):
    for j in range(n):
        mj = jnp.max(work, axis=0, keepdims=True)
        emit(j, mj)
        if j + 1 < n:
            work = jnp.where(work == mj, -jnp.inf, work)


def _router_kernel(ht_ref, wqt_ref, keys_ref, thr_ref, w0_ref, s1_ref, e1_ref,
                   q_scr, s_scr, top_scr, cand_scr, *, tb):
    q_scr[...] = jnp.dot(wqt_ref[...], ht_ref[...], preferred_element_type=F32).astype(BF16)
    lane_groups = [slice(lg * LANES, (lg + 1) * LANES) for lg in range(tb // LANES)]

    def half_body(hc, carry):
        qhc = q_scr[pl.ds(pl.multiple_of(hc * PEER_N_KEYS, PEER_N_KEYS), PEER_N_KEYS), :]
        s = jnp.dot(keys_ref[hc], qhc, preferred_element_type=F32)
        s_scr[hc] = s
        for ls in lane_groups:
            def emit(j, mj, ls=ls):
                top_scr[hc, pl.ds(j, 1), ls] = mj

            _extract_sorted(s[:, ls], N_RANK, emit)
        return carry

    lax.fori_loop(0, 2 * PEER_HEADS, half_body, 0)

    def head_body(h, carry):
        t0 = top_scr[2 * h]
        t1 = top_scr[2 * h + 1]
        cand_scr[...] = jnp.full(cand_scr.shape, -jnp.inf, F32)
        off = 0
        for a, nb in enumerate(CAND_COUNTS):
            cand_scr[off:off + nb, :] = t0[a:a + 1, :] + t1[0:nb, :]
            off += nb

        best = []
        _extract_sorted(cand_scr[...], N_RANK, lambda j, mj: best.append(mj))
        theta = 0.5 * (best[PEER_TOPK - 1] + best[PEER_TOPK])
        z = jnp.zeros_like(theta)
        for j in range(PEER_TOPK):
            z = z + jnp.exp(best[j] - best[0])

        s0 = s_scr[2 * h]
        s1 = s_scr[2 * h + 1]
        thr_ref[h] = theta - s0
        w0_ref[h] = jnp.exp(s0 - t0[0:1, :]) / z
        e1 = jnp.exp(s1 - t1[0:1, :])
        for lg, ls in enumerate(lane_groups):
            s1_ref[h, lg] = s1[:, ls]
            e1_ref[h, lg] = e1[:, ls]
        return carry

    lax.fori_loop(0, PEER_HEADS, head_body, 0)


def peer_router(ht, wqt, keys, tb=512):
    d, s = ht.shape
    nq = wqt.shape[0]
    tab = jax.ShapeDtypeStruct((PEER_HEADS, PEER_N_KEYS, s), F32)
    tab_spec = pl.BlockSpec((PEER_HEADS, PEER_N_KEYS, tb), lambda i: (0, 0, i))
    tile = jax.ShapeDtypeStruct((PEER_HEADS, s // LANES, PEER_N_KEYS, LANES), F32)
    tile_spec = pl.BlockSpec((PEER_HEADS, tb // LANES, PEER_N_KEYS, LANES), lambda i: (0, i, 0, 0))
    return pl.pallas_call(
        functools.partial(_router_kernel, tb=tb),
        grid=(s // tb,),
        in_specs=[pl.BlockSpec((d, tb), lambda i: (0, i)),
                  pl.BlockSpec((nq, d), lambda i: (0, 0)),
                  pl.BlockSpec(keys.shape, lambda i: (0, 0, 0))],
        out_specs=[tab_spec, tab_spec, tile_spec, tile_spec],
        out_shape=[tab, tab, tile, tile],
        scratch_shapes=[pltpu.VMEM((nq, tb), BF16),
                        pltpu.VMEM((2 * PEER_HEADS, PEER_N_KEYS, tb), F32),
                        pltpu.VMEM((2 * PEER_HEADS, RANK_ROWS, tb), F32),
                        pltpu.VMEM((CAND_ROWS, tb), F32)],
        compiler_params=_params(("parallel",)),
        name="peer_router",
    )(ht, wqt, keys)


def _gelu_tanh(x):
    return 0.5 * x * (1.0 + jnp.tanh(0.7978845608028654 * (x + 0.044715 * (x * x * x))))


PEER_SUB = 256
GATE_ROWS = 32


def _peer_stage(ht_ref, u_ref, v_ref, thr_ref, w0_ref, s1_ref, e1_ref, o_ref,
                at_w, at_r, pt_w, pt_r, *, te, tb):
    d = o_ref.shape[1]
    n_sub = te // PEER_SUB
    col_w = d // n_sub

    def hidden_piece(j):
        rows = slice(j * PEER_SUB, (j + 1) * PEER_SUB)
        at = jnp.dot(u_ref[rows, :], ht_ref[...], preferred_element_type=F32)
        for lg in range(tb // LANES):
            at_w[lg, rows, :] = at[:, lg * LANES:(lg + 1) * LANES]

    def value_piece(c):
        cols = slice(c * col_w, (c + 1) * col_w)
        o_ref[:, cols] += lax.dot_general(pt_r[...], v_ref[:, cols], (((0,), (0,)), ((), ())),
                                          preferred_element_type=F32)

    def gate_piece(i0):
        rows = slice(i0 * PEER_N_KEYS, (i0 + 1) * PEER_N_KEYS)
        for lg in range(tb // LANES):
            ls = slice(lg * LANES, (lg + 1) * LANES)
            for q in range(PEER_N_KEYS // GATE_ROWS):
                qs = slice(q * GATE_ROWS, (q + 1) * GATE_ROWS)
                er = slice(i0 * PEER_N_KEYS + q * GATE_ROWS, i0 * PEER_N_KEYS + (q + 1) * GATE_ROWS)
                gate = None
                for h in range(PEER_HEADS):
                    thr = thr_ref[h, i0:i0 + 1, ls]
                    w0 = w0_ref[h, i0:i0 + 1, ls]
                    term = jnp.where(s1_ref[h, lg, qs, :] >= thr, e1_ref[h, lg, qs, :] * w0, 0.0)
                    gate = term if gate is None else gate + term
                pt_w[er, ls] = (_gelu_tanh(at_r[lg, er, :]) * gate).astype(BF16)

    mxu_pieces = [functools.partial(hidden_piece, j) for j in range(n_sub)]
    mxu_pieces += [functools.partial(value_piece, c) for c in range(n_sub)]
    n_i0 = te // PEER_N_KEYS
    assert n_i0 == len(mxu_pieces)
    for i0, piece in enumerate(mxu_pieces):
        gate_piece(i0)
        piece()


def _peer_kernel(ht_ref, u_ref, v_ref, thr_ref, w0_ref, s1_ref, e1_ref, x1_ref, g_ref,
                 o_ref, at0_ref, at1_ref, pt0_ref, pt1_ref, *, te, tb, n_e):
    g = pl.program_id(0)
    e_out = (g - 2) % n_e
    out_live = g >= 2
    at_refs = (at0_ref, at1_ref)
    pt_refs = (pt0_ref, pt1_ref)

    @pl.when(g == 0)
    def _():
        for r in at_refs + pt_refs:
            r[...] = jnp.zeros(r.shape, r.dtype)

    @pl.when((g == 0) | (out_live & (e_out == 0)))
    def _():
        o_ref[...] = x1_ref[...]

    for parity in range(2):
        @pl.when(g % 2 == parity)
        def _(parity=parity):
            _peer_stage(ht_ref, u_ref, v_ref, thr_ref, w0_ref, s1_ref, e1_ref, o_ref,
                        at_refs[parity], at_refs[1 - parity],
                        pt_refs[1 - parity], pt_refs[parity], te=te, tb=tb)

    @pl.when(out_live & (e_out == n_e - 1))
    def _():
        o_ref[...] = _rms(o_ref[...], g_ref[...])


def peer_experts(ht, u, v, tables, x1, g, tb=512, te=1024):
    d, s = ht.shape
    n_exp = u.shape[0]
    n_i0 = te // PEER_N_KEYS
    n_e = n_exp // te
    n_items = (s // tb) * n_e
    thr, w0, s1, e1 = tables

    def item(gi, lag):
        n = jnp.clip(gi - lag, 0, n_items - 1)
        return n // n_e, n % n_e

    row_spec = pl.BlockSpec((PEER_HEADS, n_i0, tb), lambda gi: (0, item(gi, 1)[1], item(gi, 1)[0]))
    tab_spec = pl.BlockSpec((PEER_HEADS, tb // LANES, PEER_N_KEYS, LANES),
                            lambda gi: (0, item(gi, 1)[0], 0, 0))
    return pl.pallas_call(
        functools.partial(_peer_kernel, te=te, tb=tb, n_e=n_e),
        grid=(n_items + 2,),
        in_specs=[pl.BlockSpec((d, tb), lambda gi: (0, item(gi, 0)[0])),
                  pl.BlockSpec((te, d), lambda gi: (item(gi, 0)[1], 0)),
                  pl.BlockSpec((te, d), lambda gi: (item(gi, 2)[1], 0)),
                  row_spec, row_spec, tab_spec, tab_spec,
                  pl.BlockSpec((tb, d), lambda gi: (item(gi, 2)[0], 0),
                               pipeline_mode=pl.Buffered(1)),
                  pl.BlockSpec((1, d), lambda gi: (0, 0))],
        out_specs=pl.BlockSpec((tb, d), lambda gi: (item(gi, 2)[0], 0)),
        out_shape=jax.ShapeDtypeStruct((s, d), F32),
        scratch_shapes=[pltpu.VMEM((tb // LANES, te, LANES), F32),
                        pltpu.VMEM((tb // LANES, te, LANES), F32),
                        pltpu.VMEM((te, tb), BF16),
                        pltpu.VMEM((te, tb), BF16)],
        compiler_params=_params(("arbitrary",)),
        name="peer_experts",
    )(ht, u, v, thr, w0, s1, e1, x1, g.reshape(1, d))


def kernel(x, norm1_g, w_in, q_norm_g, k_norm_g, w_out, norm2_g, peer_w_query,
           peer_sub_keys, peer_u, peer_v, final_norm_g):
    b, s, d = x.shape
    assert b == 1
    a_width = A_HEADS * HEAD_DIM
    depth = w_in.shape[0]
    xs = x.reshape(s, d)
    for l in range(depth):
        proj = in_projection(xs, norm1_g[l], w_in[l].astype(BF16))
        mixed_a = dilated_attention(proj, s)
        qt, kb, vt = gqa_prep(proj, q_norm_g[l], k_norm_g[l], s, col_block=2)
        mixed_b = gqa_attention(qt, kb, vt, s)
        w_o = w_out[l].astype(BF16)
        x1, ht = out_projection(mixed_a, mixed_b, w_o[:a_width], w_o[a_width:], xs, norm2_g[l])
        keys = peer_sub_keys[l].reshape(2 * PEER_HEADS, PEER_N_KEYS, -1).astype(BF16)
        tables = peer_router(ht, peer_w_query[l].T.astype(BF16), keys)
        assert l == depth - 1
        xs = peer_experts(ht, peer_u[l].astype(BF16), peer_v[l].astype(BF16), tables, x1,
                          final_norm_g)
    return xs.reshape(b, s, d)
```

```python
import functools

import numpy as np
import jax
import jax.numpy as jnp
from jax import lax
from jax.experimental import pallas as pl
from jax.experimental.pallas import tpu as pltpu

F32 = jnp.float32
BF16 = jnp.bfloat16

HEAD_DIM = 128
A_HEADS = 8
DIL_PATTERNS = ((128, 1), (512, 4), (2048, 16))
B_Q_HEADS = 8
B_KV_HEADS = 2
GRID_W = 64
ROPE_THETA = 10000.0
NORM_EPS = 1e-6
PEER_HEADS = 8
PEER_N_KEYS = 128
PEER_TOPK = 16

LANES = 128
NEG_BIG = -1e30
LOG2_E = 1.4426950408889634
VMEM_LIMIT = 56 * 1024 * 1024


def _params(sem, vmem=VMEM_LIMIT):
    return pltpu.CompilerParams(dimension_semantics=sem, vmem_limit_bytes=vmem)


def _rms(x, g):
    ms = jnp.mean(x * x, axis=-1, keepdims=True)
    return x * lax.rsqrt(ms + NORM_EPS) * g


def _inproj_kernel(x_ref, g_ref, w_ref, o_ref, h_ref):
    @pl.when(pl.program_id(1) == 0)
    def _():
        h_ref[...] = _rms(x_ref[...], g_ref[...]).astype(BF16)

    o_ref[...] = jnp.dot(h_ref[...], w_ref[...], preferred_element_type=F32)


def in_projection(x, g, w_bf16, tm=1024, tn=512):
    s, d = x.shape
    n = w_bf16.shape[1]
    return pl.pallas_call(
        _inproj_kernel,
        grid=(s // tm, n // tn),
        in_specs=[pl.BlockSpec((tm, d), lambda i, j: (i, 0)),
                  pl.BlockSpec((1, d), lambda i, j: (0, 0)),
                  pl.BlockSpec((d, tn), lambda i, j: (0, j))],
        out_specs=pl.BlockSpec((tm, tn), lambda i, j: (i, j)),
        out_shape=jax.ShapeDtypeStruct((s, n), F32),
        scratch_shapes=[pltpu.VMEM((tm, d), BF16)],
        compiler_params=_params(("parallel", "arbitrary")),
        name="in_projection",
    )(x, g.reshape(1, d), w_bf16)


DIL_QBLK = 2048
DIL_SUB = 128
DIL_KWIN = 256
DIL_UNROLL = 4


def _dilated_kernel(q_ref, k_ref, v_ref, o_ref, m_ref, l_ref, acc_ref, *, seq):
    h = pl.program_id(0)
    base = pl.program_id(1) * DIL_QBLK
    slope = jnp.exp2(-(h + 1).astype(F32))
    scale = HEAD_DIM ** -0.5

    m_ref[...] = jnp.full(m_ref.shape, NEG_BIG, F32)
    l_ref[...] = jnp.zeros(l_ref.shape, F32)
    acc_ref[...] = jnp.zeros(acc_ref.shape, F32)

    row = lax.broadcasted_iota(jnp.int32, (DIL_SUB, DIL_KWIN), 0)
    col = lax.broadcasted_iota(jnp.int32, (DIL_SUB, DIL_KWIN), 1)
    col_minus_row = col - row

    for w, d in DIL_PATTERNS:
        half = (w // 2) // d
        assert half * 2 + DIL_SUB == DIL_KWIN
        seg_len = seq // d

        def rows(start, n, d=d):
            return pl.ds(start, n, stride=d) if d > 1 else pl.ds(start, n)

        def body(c, carry, d=d, half=half, seg_len=seg_len, rows=rows):
            r = c % d
            cc = c // d
            p0 = base // d + cc * DIL_SUB
            kp = jnp.clip(p0 - half, 0, seg_len - DIL_KWIN)
            lq = r + d * (cc * DIL_SUB)
            ks = r + d * kp

            q = (q_ref[rows(lq, DIL_SUB), :] * scale).astype(BF16)
            k = k_ref[rows(ks, DIL_KWIN), :].astype(BF16)
            v = v_ref[rows(ks, DIL_KWIN), :].astype(BF16)
            s = lax.dot_general(q, k, (((1,), (1,)), ((), ())), preferred_element_type=F32)
            rel = jnp.abs(col_minus_row + (kp - p0))
            s = jnp.where(rel <= half, s - (slope * d) * rel.astype(F32), NEG_BIG)
            mb = jnp.max(s, axis=1, keepdims=True)
            p = jnp.exp(s - mb)
            lb = jnp.sum(p, axis=1, keepdims=True)
            ob = jnp.dot(p.astype(BF16), v, preferred_element_type=F32)

            idx = rows(lq, DIL_SUB)
            m_old = m_ref[idx, :]
            m_new = jnp.maximum(m_old, mb)
            a_old = jnp.exp(m_old - m_new)
            a_blk = jnp.exp(mb - m_new)
            m_ref[idx, :] = m_new
            l_ref[idx, :] = a_old * l_ref[idx, :] + a_blk * lb
            acc_ref[idx, :] = a_old * acc_ref[idx, :] + a_blk * ob
            return carry

        def group(cg, carry, body=body):
            for u in range(DIL_UNROLL):
                body(cg * DIL_UNROLL + u, carry)
            return carry

        lax.fori_loop(0, DIL_QBLK // DIL_SUB // DIL_UNROLL, group, 0)

    o_ref[...] = (acc_ref[...] / l_ref[...]).astype(o_ref.dtype)


def dilated_attention(proj, seq):
    assert seq % DIL_QBLK == 0
    for _, d in DIL_PATTERNS:
        assert DIL_QBLK // d >= DIL_SUB and seq // d >= DIL_KWIN
    nh = A_HEADS
    return pl.pallas_call(
        functools.partial(_dilated_kernel, seq=seq),
        grid=(nh, seq // DIL_QBLK),
        in_specs=[pl.BlockSpec((DIL_QBLK, HEAD_DIM), lambda h, i: (i, h)),
                  pl.BlockSpec((seq, HEAD_DIM), lambda h, i: (0, nh + h)),
                  pl.BlockSpec((seq, HEAD_DIM), lambda h, i: (0, 2 * nh + h))],
        out_specs=pl.BlockSpec((DIL_QBLK, HEAD_DIM), lambda h, i: (i, h)),
        out_shape=jax.ShapeDtypeStruct((seq, nh * HEAD_DIM), BF16),
        scratch_shapes=[pltpu.VMEM((DIL_QBLK, LANES), F32),
                        pltpu.VMEM((DIL_QBLK, LANES), F32),
                        pltpu.VMEM((DIL_QBLK, HEAD_DIM), F32)],
        compiler_params=_params(("parallel", "arbitrary")),
        name="dilated_attention",
    )(proj, proj, proj)


def _rope_tables(seq):
    rows = seq // GRID_W
    row = jnp.repeat(jnp.arange(rows, dtype=F32), GRID_W)
    col = jnp.tile(jnp.arange(GRID_W, dtype=F32), rows)
    half = HEAD_DIM // 2
    inv = ROPE_THETA ** (-jnp.arange(0, half, 2, dtype=F32) / half)
    ang = jnp.concatenate([row[:, None] * inv, col[:, None] * inv], axis=-1)
    cos = jnp.repeat(jnp.cos(ang), 2, axis=-1)
    sign = jnp.tile(jnp.asarray([-1.0, 1.0], F32), half)
    sin = jnp.repeat(jnp.sin(ang), 2, axis=-1) * sign
    return cos, sin


def _gqa_prep_kernel(p_ref, cos_ref, sin_ref, qg_ref, kg_ref, qt_ref, k_ref, vt_ref):
    cos = cos_ref[...]
    sin = sin_ref[...]
    even = (lax.broadcasted_iota(jnp.int32, cos.shape, 1) % 2) == 0

    def norm_rope(xh, g):
        y = _rms(xh, g)
        partner = jnp.where(even, pltpu.roll(y, LANES - 1, 1), pltpu.roll(y, 1, 1))
        return y * cos + partner * sin

    scale = HEAD_DIM ** -0.5 * LOG2_E
    for hq in range(B_Q_HEADS):
        xh = p_ref[:, hq * HEAD_DIM:(hq + 1) * HEAD_DIM]
        y = norm_rope(xh, qg_ref[...]) * scale
        qt_ref[hq * HEAD_DIM:(hq + 1) * HEAD_DIM, :] = y.T.astype(BF16)
    k0 = B_Q_HEADS * HEAD_DIM
    v0 = k0 + B_KV_HEADS * HEAD_DIM
    for hk in range(B_KV_HEADS):
        xh = p_ref[:, k0 + hk * HEAD_DIM:k0 + (hk + 1) * HEAD_DIM]
        k_ref[:, hk * HEAD_DIM:(hk + 1) * HEAD_DIM] = norm_rope(xh, kg_ref[...]).astype(BF16)
        vh = p_ref[:, v0 + hk * HEAD_DIM:v0 + (hk + 1) * HEAD_DIM]
        vt_ref[hk * HEAD_DIM:(hk + 1) * HEAD_DIM, :] = vh.T.astype(BF16)


def gqa_prep(proj, q_g, k_g, seq, col_block, ts=512):
    wb = (B_Q_HEADS + 2 * B_KV_HEADS) * HEAD_DIM
    cos, sin = _rope_tables(seq)
    return pl.pallas_call(
        _gqa_prep_kernel,
        grid=(seq // ts,),
        in_specs=[pl.BlockSpec((ts, wb), lambda i: (i, col_block)),
                  pl.BlockSpec((ts, HEAD_DIM), lambda i: (i, 0)),
                  pl.BlockSpec((ts, HEAD_DIM), lambda i: (i, 0)),
                  pl.BlockSpec((1, HEAD_DIM), lambda i: (0, 0)),
                  pl.BlockSpec((1, HEAD_DIM), lambda i: (0, 0))],
        out_specs=[pl.BlockSpec((B_Q_HEADS * HEAD_DIM, ts), lambda i: (0, i)),
                   pl.BlockSpec((ts, B_KV_HEADS * HEAD_DIM), lambda i: (i, 0)),
                   pl.BlockSpec((B_KV_HEADS * HEAD_DIM, ts), lambda i: (0, i))],
        out_shape=[jax.ShapeDtypeStruct((B_Q_HEADS * HEAD_DIM, seq), BF16),
                   jax.ShapeDtypeStruct((seq, B_KV_HEADS * HEAD_DIM), BF16),
                   jax.ShapeDtypeStruct((B_KV_HEADS * HEAD_DIM, seq), BF16)],
        compiler_params=_params(("parallel",)),
        name="gqa_prep",
    )(proj, cos, sin, q_g.reshape(1, HEAD_DIM), k_g.reshape(1, HEAD_DIM))


def _gqa_kernel(qt_ref, k_ref, vt_ref, o_ref, m_ref, l_ref, acc_ref, *, seq, tk, nh):
    m_ref[...] = jnp.full(m_ref.shape, NEG_BIG, F32)
    l_ref[...] = jnp.zeros(l_ref.shape, F32)
    acc_ref[...] = jnp.zeros(acc_ref.shape, F32)

    def body(c, carry):
        start = pl.multiple_of(c * tk, tk)
        k = k_ref[pl.ds(start, tk), :]
        vt = vt_ref[:, pl.ds(start, tk)]
        for hh in range(nh):
            qt = qt_ref[hh * HEAD_DIM:(hh + 1) * HEAD_DIM, :]
            st = jnp.dot(k, qt, preferred_element_type=F32)
            m_old = m_ref[hh]
            m_new = jnp.maximum(m_old, jnp.max(st, axis=0, keepdims=True))
            alpha = jnp.exp2(m_old - m_new)
            p = jnp.exp2(st - m_new)
            l_ref[hh] = alpha * l_ref[hh] + jnp.sum(p, axis=0, keepdims=True)
            acc_ref[hh] = alpha * acc_ref[hh] + jnp.dot(vt, p.astype(BF16),
                                                        preferred_element_type=F32)
            m_ref[hh] = m_new
        return carry

    lax.fori_loop(0, seq // tk, body, 0)
    for hh in range(nh):
        o_ref[:, hh * HEAD_DIM:(hh + 1) * HEAD_DIM] = (acc_ref[hh] / l_ref[hh]).T.astype(o_ref.dtype)


def gqa_attention(qt, k, vt, seq, tq=512, tk=512):
    nh = B_Q_HEADS // B_KV_HEADS
    return pl.pallas_call(
        functools.partial(_gqa_kernel, seq=seq, tk=tk, nh=nh),
        grid=(B_KV_HEADS, seq // tq),
        in_specs=[pl.BlockSpec((nh * HEAD_DIM, tq), lambda g, i: (g, i)),
                  pl.BlockSpec((seq, HEAD_DIM), lambda g, i: (0, g)),
                  pl.BlockSpec((HEAD_DIM, seq), lambda g, i: (g, 0))],
        out_specs=pl.BlockSpec((tq, nh * HEAD_DIM), lambda g, i: (i, g)),
        out_shape=jax.ShapeDtypeStruct((seq, B_Q_HEADS * HEAD_DIM), BF16),
        scratch_shapes=[pltpu.VMEM((nh, 1, tq), F32),
                        pltpu.VMEM((nh, 1, tq), F32),
                        pltpu.VMEM((nh, HEAD_DIM, tq), F32)],
        compiler_params=_params(("parallel", "arbitrary")),
        name="gqa_attention",
    )(qt, k, vt)


def _outproj_kernel(ma_ref, mb_ref, wa_ref, wb_ref, x_ref, g_ref, x1_ref, ht_ref):
    y = x_ref[...]
    y = y + jnp.dot(ma_ref[...], wa_ref[...], preferred_element_type=F32)
    y = y + jnp.dot(mb_ref[...], wb_ref[...], preferred_element_type=F32)
    x1_ref[...] = y
    ht_ref[...] = _rms(y, g_ref[...]).T.astype(BF16)


def out_projection(mixed_a, mixed_b, w_a, w_b, x, g, tm=512):
    s, d = x.shape
    wa, wb = mixed_a.shape[1], mixed_b.shape[1]
    return pl.pallas_call(
        _outproj_kernel,
        grid=(s // tm,),
        in_specs=[pl.BlockSpec((tm, wa), lambda i: (i, 0)),
                  pl.BlockSpec((tm, wb), lambda i: (i, 0)),
                  pl.BlockSpec((wa, d), lambda i: (0, 0)),
                  pl.BlockSpec((wb, d), lambda i: (0, 0)),
                  pl.BlockSpec((tm, d), lambda i: (i, 0)),
                  pl.BlockSpec((1, d), lambda i: (0, 0))],
        out_specs=[pl.BlockSpec((tm, d), lambda i: (i, 0)),
                   pl.BlockSpec((d, tm), lambda i: (0, i))],
        out_shape=[jax.ShapeDtypeStruct((s, d), F32),
                   jax.ShapeDtypeStruct((d, s), BF16)],
        compiler_params=_params(("parallel",)),
        name="out_projection",
    )(mixed_a, mixed_b, w_a, w_b, x, g.reshape(1, d))


N_RANK = PEER_TOPK + 1
RANK_ROWS = 24
CAND_COUNTS = tuple(N_RANK // (a + 1) for a in range(N_RANK))
CAND_ROWS = -(-sum(CAND_COUNTS) // 8) * 8


def _extract_sorted(work, n, emit):
    for j in range(n):
        mj = jnp.max(work, axis=0, keepdims=True)
        emit(j, mj)
        if j + 1 < n:
            work = jnp.where(work == mj, -jnp.inf, work)


def _router_kernel(ht_ref, wqt_ref, keys_ref, thr_ref, w0_ref, s1_ref, e1_ref,
                   q_scr, s_scr, top_scr, cand_scr, *, tb):
    q_scr[...] = jnp.dot(wqt_ref[...], ht_ref[...], preferred_element_type=F32).astype(BF16)
    lane_groups = [slice(lg * LANES, (lg + 1) * LANES) for lg in range(tb // LANES)]

    def half_body(hc, carry):
        qhc = q_scr[pl.ds(pl.multiple_of(hc * PEER_N_KEYS, PEER_N_KEYS), PEER_N_KEYS), :]
        s = jnp.dot(keys_ref[hc], qhc, preferred_element_type=F32)
        s_scr[hc] = s
        for ls in lane_groups:
            def emit(j, mj, ls=ls):
                top_scr[hc, pl.ds(j, 1), ls] = mj

            _extract_sorted(s[:, ls], N_RANK, emit)
        return carry

    lax.fori_loop(0, 2 * PEER_HEADS, half_body, 0)

    def head_body(h, carry):
        t0 = top_scr[2 * h]
        t1 = top_scr[2 * h + 1]
        cand_scr[...] = jnp.full(cand_scr.shape, -jnp.inf, F32)
        off = 0
        for a, nb in enumerate(CAND_COUNTS):
            cand_scr[off:off + nb, :] = t0[a:a + 1, :] + t1[0:nb, :]
            off += nb

        best = []
        _extract_sorted(cand_scr[...], N_RANK, lambda j, mj: best.append(mj))
        theta = 0.5 * (best[PEER_TOPK - 1] + best[PEER_TOPK])
        z = jnp.zeros_like(theta)
        for j in range(PEER_TOPK):
            z = z + jnp.exp(best[j] - best[0])

        s0 = s_scr[2 * h]
        s1 = s_scr[2 * h + 1]
        thr_ref[h] = theta - s0
        w0_ref[h] = jnp.exp(s0 - t0[0:1, :]) / z
        e1 = jnp.exp(s1 - t1[0:1, :])
        for lg, ls in enumerate(lane_groups):
            s1_ref[h, lg] = s1[:, ls]
            e1_ref[h, lg] = e1[:, ls]
        return carry

    lax.fori_loop(0, PEER_HEADS, head_body, 0)


def peer_router(ht, wqt, keys, tb=512):
    d, s = ht.shape
    nq = wqt.shape[0]
    tab = jax.ShapeDtypeStruct((PEER_HEADS, PEER_N_KEYS, s), F32)
    tab_spec = pl.BlockSpec((PEER_HEADS, PEER_N_KEYS, tb), lambda i: (0, 0, i))
    tile = jax.ShapeDtypeStruct((PEER_HEADS, s // LANES, PEER_N_KEYS, LANES), F32)
    tile_spec = pl.BlockSpec((PEER_HEADS, tb // LANES, PEER_N_KEYS, LANES), lambda i: (0, i, 0, 0))
    return pl.pallas_call(
        functools.partial(_router_kernel, tb=tb),
        grid=(s // tb,),
        in_specs=[pl.BlockSpec((d, tb), lambda i: (0, i)),
                  pl.BlockSpec((nq, d), lambda i: (0, 0)),
                  pl.BlockSpec(keys.shape, lambda i: (0, 0, 0))],
        out_specs=[tab_spec, tab_spec, tile_spec, tile_spec],
        out_shape=[tab, tab, tile, tile],
        scratch_shapes=[pltpu.VMEM((nq, tb), BF16),
                        pltpu.VMEM((2 * PEER_HEADS, PEER_N_KEYS, tb), F32),
                        pltpu.VMEM((2 * PEER_HEADS, RANK_ROWS, tb), F32),
                        pltpu.VMEM((CAND_ROWS, tb), F32)],
        compiler_params=_params(("parallel",)),
        name="peer_router",
    )(ht, wqt, keys)


def _gelu_tanh(x):
    return 0.5 * x * (1.0 + jnp.tanh(0.7978845608028654 * (x + 0.044715 * (x * x * x))))


PEER_SUB = 256
GATE_ROWS = 32


TOKEN_HALF = 256


def _data_dependent_zero(tile):
    bits = pltpu.bitcast(tile, jnp.uint32)
    zero_words = (bits >> 16) >> 16
    return zero_words.astype(jnp.int32).astype(F32).astype(BF16)


def _peer_stage(ht_ref, u_ref, v_ref, thr_ref, w0_ref, s1_ref, e1_ref, o_ref,
                at_w, at_r, pt_w, pt_r, us_ref, vs_ref, *, te, tb):
    d = u_ref.shape[1]
    n_sub = te // PEER_SUB
    n_i0 = te // PEER_N_KEYS
    k_slices = d // LANES
    tiles = [(lg, i0, q)
             for lg in range(tb // LANES)
             for i0 in range(n_i0)
             for q in range(PEER_N_KEYS // GATE_ROWS)]
    chunks = [(u_ref, us_ref, j, c) for j in range(n_sub) for c in range(k_slices)]
    chunks += [(v_ref, vs_ref, j, c) for c in range(k_slices) for j in range(n_sub)]
    assert len(chunks) == len(tiles)

    for (lg, i0, q), (src, dst, j, c) in zip(tiles, chunks):
        ls = slice(lg * LANES, (lg + 1) * LANES)
        qs = slice(q * GATE_ROWS, (q + 1) * GATE_ROWS)
        er = slice(i0 * PEER_N_KEYS + q * GATE_ROWS, i0 * PEER_N_KEYS + (q + 1) * GATE_ROWS)
        gate = None
        for h in range(PEER_HEADS):
            thr = thr_ref[h, i0:i0 + 1, ls]
            w0 = w0_ref[h, i0:i0 + 1, ls]
            term = jnp.where(s1_ref[h, lg, qs, :] >= thr, e1_ref[h, lg, qs, :] * w0, 0.0)
            gate = term if gate is None else gate + term
        p_tile = (_gelu_tanh(at_r[lg, er, :]) * gate).astype(BF16)
        pt_w[er, ls] = p_tile

        rows = slice(j * PEER_SUB, (j + 1) * PEER_SUB)
        cs = slice(c * LANES, (c + 1) * LANES)
        zero = _data_dependent_zero(p_tile)
        dst[rows, cs] = src[rows, cs] + pltpu.repeat(zero, PEER_SUB // zero.shape[0], axis=0)

    for j in range(n_sub):
        rows = slice(j * PEER_SUB, (j + 1) * PEER_SUB)
        at = jnp.dot(us_ref[rows, :], ht_ref[...], preferred_element_type=F32)
        for lg in range(tb // LANES):
            at_w[lg, rows, :] = at[:, lg * LANES:(lg + 1) * LANES]

    o_ref[...] += lax.dot_general(pt_r[...], vs_ref[...], (((0,), (0,)), ((), ())),
                                  preferred_element_type=F32)


def _peer_kernel(ht_ref, u_ref, v_ref, thr_ref, w0_ref, s1_ref, e1_ref, x1_ref, g_ref,
                 o_ref, at0_ref, at1_ref, pt0_ref, pt1_ref, us_ref, vs_ref, *, te, tb, n_e):
    g = pl.program_id(0)
    e_out = (g - 2) % n_e
    out_live = g >= 2
    at_refs = (at0_ref, at1_ref)
    pt_refs = (pt0_ref, pt1_ref)

    @pl.when(g == 0)
    def _():
        for r in at_refs + pt_refs:
            r[...] = jnp.zeros(r.shape, r.dtype)

    @pl.when((g == 0) | (out_live & (e_out == 0)))
    def _():
        o_ref[...] = x1_ref[...]

    for parity in range(2):
        @pl.when(g % 2 == parity)
        def _(parity=parity):
            _peer_stage(ht_ref, u_ref, v_ref, thr_ref, w0_ref, s1_ref, e1_ref, o_ref,
                        at_refs[parity], at_refs[1 - parity],
                        pt_refs[1 - parity], pt_refs[parity], us_ref, vs_ref, te=te, tb=tb)

    @pl.when(out_live & (e_out == n_e - 1))
    def _():
        o_ref[...] = _rms(o_ref[...], g_ref[...])


def peer_experts(ht, u, v, tables, x1, g, tb=512, te=1024):
    d, s = ht.shape
    n_exp = u.shape[0]
    n_i0 = te // PEER_N_KEYS
    n_e = n_exp // te
    n_items = (s // tb) * n_e
    thr, w0, s1, e1 = tables

    def item(gi, lag):
        n = jnp.clip(gi - lag, 0, n_items - 1)
        return n // n_e, n % n_e

    row_spec = pl.BlockSpec((PEER_HEADS, n_i0, tb), lambda gi: (0, item(gi, 1)[1], item(gi, 1)[0]))
    tab_spec = pl.BlockSpec((PEER_HEADS, tb // LANES, PEER_N_KEYS, LANES),
                            lambda gi: (0, item(gi, 1)[0], 0, 0), pipeline_mode=pl.Buffered(1))
    return pl.pallas_call(
        functools.partial(_peer_kernel, te=te, tb=tb, n_e=n_e),
        grid=(n_items + 2,),
        in_specs=[pl.BlockSpec((d, tb), lambda gi: (0, item(gi, 0)[0])),
                  pl.BlockSpec((te, d), lambda gi: (item(gi, 0)[1], 0)),
                  pl.BlockSpec((te, d), lambda gi: (item(gi, 2)[1], 0)),
                  row_spec, row_spec, tab_spec, tab_spec,
                  pl.BlockSpec((tb, d), lambda gi: (item(gi, 2)[0], 0),
                               pipeline_mode=pl.Buffered(1)),
                  pl.BlockSpec((1, d), lambda gi: (0, 0))],
        out_specs=pl.BlockSpec((tb, d), lambda gi: (item(gi, 2)[0], 0)),
        out_shape=jax.ShapeDtypeStruct((s, d), F32),
        scratch_shapes=[pltpu.VMEM((tb // LANES, te, LANES), F32),
                        pltpu.VMEM((tb // LANES, te, LANES), F32),
                        pltpu.VMEM((te, tb), BF16),
                        pltpu.VMEM((te, tb), BF16),
                        pltpu.VMEM((te, d), BF16),
                        pltpu.VMEM((te, d), BF16)],
        compiler_params=_params(("arbitrary",)),
        name="peer_experts",
    )(ht, u, v, thr, w0, s1, e1, x1, g.reshape(1, d))


def kernel(x, norm1_g, w_in, q_norm_g, k_norm_g, w_out, norm2_g, peer_w_query,
           peer_sub_keys, peer_u, peer_v, final_norm_g):
    b, s, d = x.shape
    assert b == 1
    a_width = A_HEADS * HEAD_DIM
    depth = w_in.shape[0]
    xs = x.reshape(s, d)
    for l in range(depth):
        proj = in_projection(xs, norm1_g[l], w_in[l].astype(BF16))
        mixed_a = dilated_attention(proj, s)
        qt, kb, vt = gqa_prep(proj, q_norm_g[l], k_norm_g[l], s, col_block=2)
        mixed_b = gqa_attention(qt, kb, vt, s)
        w_o = w_out[l].astype(BF16)
        x1, ht = out_projection(mixed_a, mixed_b, w_o[:a_width], w_o[a_width:], xs, norm2_g[l])
        keys = peer_sub_keys[l].reshape(2 * PEER_HEADS, PEER_N_KEYS, -1).astype(BF16)
        tables = peer_router(ht, peer_w_query[l].T.astype(BF16), keys)
        assert l == depth - 1
        xs = peer_experts(ht, peer_u[l].astype(BF16), peer_v[l].astype(BF16), tables, x1,
                          final_norm_g)
    return xs.reshape(b, s, d)
```

```python
import functools

import numpy as np
import jax
import jax.numpy as jnp
from jax import lax
from jax.experimental import pallas as pl
from jax.experimental.pallas import tpu as pltpu

F32 = jnp.float32
BF16 = jnp.bfloat16

HEAD_DIM = 128
A_HEADS = 8
DIL_PATTERNS = ((128, 1), (512, 4), (2048, 16))
B_Q_HEADS = 8
B_KV_HEADS = 2
GRID_W = 64
ROPE_THETA = 10000.0
NORM_EPS = 1e-6
PEER_HEADS = 8
PEER_N_KEYS = 128
PEER_TOPK = 16

LANES = 128
NEG_BIG = -1e30
LOG2_E = 1.4426950408889634
VMEM_LIMIT = 56 * 1024 * 1024


def _params(sem, vmem=VMEM_LIMIT):
    return pltpu.CompilerParams(dimension_semantics=sem, vmem_limit_bytes=vmem)


def _rms(x, g):
    ms = jnp.mean(x * x, axis=-1, keepdims=True)
    return x * lax.rsqrt(ms + NORM_EPS) * g


def _inproj_kernel(x_ref, g_ref, w_ref, o_ref, h_ref):
    @pl.when(pl.program_id(1) == 0)
    def _():
        h_ref[...] = _rms(x_ref[...], g_ref[...]).astype(BF16)

    o_ref[...] = jnp.dot(h_ref[...], w_ref[...], preferred_element_type=F32)


def in_projection(x, g, w_bf16, tm=1024, tn=512):
    s, d = x.shape
    n = w_bf16.shape[1]
    return pl.pallas_call(
        _inproj_kernel,
        grid=(s // tm, n // tn),
        in_specs=[pl.BlockSpec((tm, d), lambda i, j: (i, 0)),
                  pl.BlockSpec((1, d), lambda i, j: (0, 0)),
                  pl.BlockSpec((d, tn), lambda i, j: (0, j))],
        out_specs=pl.BlockSpec((tm, tn), lambda i, j: (i, j)),
        out_shape=jax.ShapeDtypeStruct((s, n), F32),
        scratch_shapes=[pltpu.VMEM((tm, d), BF16)],
        compiler_params=_params(("parallel", "arbitrary")),
        name="in_projection",
    )(x, g.reshape(1, d), w_bf16)


DIL_QBLK = 2048
DIL_SUB = 128
DIL_KWIN = 256
DIL_UNROLL = 4


def _dilated_kernel(q_ref, k_ref, v_ref, o_ref, m_ref, l_ref, acc_ref, *, seq):
    h = pl.program_id(0)
    base = pl.program_id(1) * DIL_QBLK
    slope = jnp.exp2(-(h + 1).astype(F32))
    scale = HEAD_DIM ** -0.5

    m_ref[...] = jnp.full(m_ref.shape, NEG_BIG, F32)
    l_ref[...] = jnp.zeros(l_ref.shape, F32)
    acc_ref[...] = jnp.zeros(acc_ref.shape, F32)

    row = lax.broadcasted_iota(jnp.int32, (DIL_SUB, DIL_KWIN), 0)
    col = lax.broadcasted_iota(jnp.int32, (DIL_SUB, DIL_KWIN), 1)
    col_minus_row = col - row

    for w, d in DIL_PATTERNS:
        half = (w // 2) // d
        assert half * 2 + DIL_SUB == DIL_KWIN
        seg_len = seq // d

        def rows(start, n, d=d):
            return pl.ds(start, n, stride=d) if d > 1 else pl.ds(start, n)

        def body(c, carry, d=d, half=half, seg_len=seg_len, rows=rows):
            r = c % d
            cc = c // d
            p0 = base // d + cc * DIL_SUB
            kp = jnp.clip(p0 - half, 0, seg_len - DIL_KWIN)
            lq = r + d * (cc * DIL_SUB)
            ks = r + d * kp

            q = (q_ref[rows(lq, DIL_SUB), :] * scale).astype(BF16)
            k = k_ref[rows(ks, DIL_KWIN), :].astype(BF16)
            v = v_ref[rows(ks, DIL_KWIN), :].astype(BF16)
            s = lax.dot_general(q, k, (((1,), (1,)), ((), ())), preferred_element_type=F32)
            rel = jnp.abs(col_minus_row + (kp - p0))
            s = jnp.where(rel <= half, s - (slope * d) * rel.astype(F32), NEG_BIG)
            mb = jnp.max(s, axis=1, keepdims=True)
            p = jnp.exp(s - mb)
            lb = jnp.sum(p, axis=1, keepdims=True)
            ob = jnp.dot(p.astype(BF16), v, preferred_element_type=F32)

            idx = rows(lq, DIL_SUB)
            m_old = m_ref[idx, :]
            m_new = jnp.maximum(m_old, mb)
            a_old = jnp.exp(m_old - m_new)
            a_blk = jnp.exp(mb - m_new)
            m_ref[idx, :] = m_new
            l_ref[idx, :] = a_old * l_ref[idx, :] + a_blk * lb
            acc_ref[idx, :] = a_old * acc_ref[idx, :] + a_blk * ob
            return carry

        def group(cg, carry, body=body):
            for u in range(DIL_UNROLL):
                body(cg * DIL_UNROLL + u, carry)
            return carry

        lax.fori_loop(0, DIL_QBLK // DIL_SUB // DIL_UNROLL, group, 0)

    o_ref[...] = (acc_ref[...] / l_ref[...]).astype(o_ref.dtype)


def dilated_attention(proj, seq):
    assert seq % DIL_QBLK == 0
    for _, d in DIL_PATTERNS:
        assert DIL_QBLK // d >= DIL_SUB and seq // d >= DIL_KWIN
    nh = A_HEADS
    return pl.pallas_call(
        functools.partial(_dilated_kernel, seq=seq),
        grid=(nh, seq // DIL_QBLK),
        in_specs=[pl.BlockSpec((DIL_QBLK, HEAD_DIM), lambda h, i: (i, h)),
                  pl.BlockSpec((seq, HEAD_DIM), lambda h, i: (0, nh + h)),
                  pl.BlockSpec((seq, HEAD_DIM), lambda h, i: (0, 2 * nh + h))],
        out_specs=pl.BlockSpec((DIL_QBLK, HEAD_DIM), lambda h, i: (i, h)),
        out_shape=jax.ShapeDtypeStruct((seq, nh * HEAD_DIM), BF16),
        scratch_shapes=[pltpu.VMEM((DIL_QBLK, LANES), F32),
                        pltpu.VMEM((DIL_QBLK, LANES), F32),
                        pltpu.VMEM((DIL_QBLK, HEAD_DIM), F32)],
        compiler_params=_params(("parallel", "arbitrary")),
        name="dilated_attention",
    )(proj, proj, proj)


def _rope_tables(seq):
    rows = seq // GRID_W
    row = jnp.repeat(jnp.arange(rows, dtype=F32), GRID_W)
    col = jnp.tile(jnp.arange(GRID_W, dtype=F32), rows)
    half = HEAD_DIM // 2
    inv = ROPE_THETA ** (-jnp.arange(0, half, 2, dtype=F32) / half)
    ang = jnp.concatenate([row[:, None] * inv, col[:, None] * inv], axis=-1)
    cos = jnp.repeat(jnp.cos(ang), 2, axis=-1)
    sign = jnp.tile(jnp.asarray([-1.0, 1.0], F32), half)
    sin = jnp.repeat(jnp.sin(ang), 2, axis=-1) * sign
    return cos, sin


def _gqa_prep_kernel(p_ref, cos_ref, sin_ref, qg_ref, kg_ref, qt_ref, k_ref, vt_ref):
    cos = cos_ref[...]
    sin = sin_ref[...]
    even = (lax.broadcasted_iota(jnp.int32, cos.shape, 1) % 2) == 0

    def norm_rope(xh, g):
        y = _rms(xh, g)
        partner = jnp.where(even, pltpu.roll(y, LANES - 1, 1), pltpu.roll(y, 1, 1))
        return y * cos + partner * sin

    scale = HEAD_DIM ** -0.5 * LOG2_E
    for hq in range(B_Q_HEADS):
        xh = p_ref[:, hq * HEAD_DIM:(hq + 1) * HEAD_DIM]
        y = norm_rope(xh, qg_ref[...]) * scale
        qt_ref[hq * HEAD_DIM:(hq + 1) * HEAD_DIM, :] = y.T.astype(BF16)
    k0 = B_Q_HEADS * HEAD_DIM
    v0 = k0 + B_KV_HEADS * HEAD_DIM
    for hk in range(B_KV_HEADS):
        xh = p_ref[:, k0 + hk * HEAD_DIM:k0 + (hk + 1) * HEAD_DIM]
        k_ref[:, hk * HEAD_DIM:(hk + 1) * HEAD_DIM] = norm_rope(xh, kg_ref[...]).astype(BF16)
        vh = p_ref[:, v0 + hk * HEAD_DIM:v0 + (hk + 1) * HEAD_DIM]
        vt_ref[hk * HEAD_DIM:(hk + 1) * HEAD_DIM, :] = vh.T.astype(BF16)


def gqa_prep(proj, q_g, k_g, seq, col_block, ts=512):
    wb = (B_Q_HEADS + 2 * B_KV_HEADS) * HEAD_DIM
    cos, sin = _rope_tables(seq)
    return pl.pallas_call(
        _gqa_prep_kernel,
        grid=(seq // ts,),
        in_specs=[pl.BlockSpec((ts, wb), lambda i: (i, col_block)),
                  pl.BlockSpec((ts, HEAD_DIM), lambda i: (i, 0)),
                  pl.BlockSpec((ts, HEAD_DIM), lambda i: (i, 0)),
                  pl.BlockSpec((1, HEAD_DIM), lambda i: (0, 0)),
                  pl.BlockSpec((1, HEAD_DIM), lambda i: (0, 0))],
        out_specs=[pl.BlockSpec((B_Q_HEADS * HEAD_DIM, ts), lambda i: (0, i)),
                   pl.BlockSpec((ts, B_KV_HEADS * HEAD_DIM), lambda i: (i, 0)),
                   pl.BlockSpec((B_KV_HEADS * HEAD_DIM, ts), lambda i: (0, i))],
        out_shape=[jax.ShapeDtypeStruct((B_Q_HEADS * HEAD_DIM, seq), BF16),
                   jax.ShapeDtypeStruct((seq, B_KV_HEADS * HEAD_DIM), BF16),
                   jax.ShapeDtypeStruct((B_KV_HEADS * HEAD_DIM, seq), BF16)],
        compiler_params=_params(("parallel",)),
        name="gqa_prep",
    )(proj, cos, sin, q_g.reshape(1, HEAD_DIM), k_g.reshape(1, HEAD_DIM))


def _gqa_kernel(qt_ref, k_ref, vt_ref, o_ref, m_ref, l_ref, acc_ref, *, seq, tk, nh):
    m_ref[...] = jnp.full(m_ref.shape, NEG_BIG, F32)
    l_ref[...] = jnp.zeros(l_ref.shape, F32)
    acc_ref[...] = jnp.zeros(acc_ref.shape, F32)

    def body(c, carry):
        start = pl.multiple_of(c * tk, tk)
        k = k_ref[pl.ds(start, tk), :]
        vt = vt_ref[:, pl.ds(start, tk)]
        for hh in range(nh):
            qt = qt_ref[hh * HEAD_DIM:(hh + 1) * HEAD_DIM, :]
            st = jnp.dot(k, qt, preferred_element_type=F32)
            m_old = m_ref[hh]
            m_new = jnp.maximum(m_old, jnp.max(st, axis=0, keepdims=True))
            alpha = jnp.exp2(m_old - m_new)
            p = jnp.exp2(st - m_new)
            l_ref[hh] = alpha * l_ref[hh] + jnp.sum(p, axis=0, keepdims=True)
            acc_ref[hh] = alpha * acc_ref[hh] + jnp.dot(vt, p.astype(BF16),
                                                        preferred_element_type=F32)
            m_ref[hh] = m_new
        return carry

    lax.fori_loop(0, seq // tk, body, 0)
    for hh in range(nh):
        o_ref[:, hh * HEAD_DIM:(hh + 1) * HEAD_DIM] = (acc_ref[hh] / l_ref[hh]).T.astype(o_ref.dtype)


def gqa_attention(qt, k, vt, seq, tq=512, tk=512):
    nh = B_Q_HEADS // B_KV_HEADS
    return pl.pallas_call(
        functools.partial(_gqa_kernel, seq=seq, tk=tk, nh=nh),
        grid=(B_KV_HEADS, seq // tq),
        in_specs=[pl.BlockSpec((nh * HEAD_DIM, tq), lambda g, i: (g, i)),
                  pl.BlockSpec((seq, HEAD_DIM), lambda g, i: (0, g)),
                  pl.BlockSpec((HEAD_DIM, seq), lambda g, i: (g, 0))],
        out_specs=pl.BlockSpec((tq, nh * HEAD_DIM), lambda g, i: (i, g)),
        out_shape=jax.ShapeDtypeStruct((seq, B_Q_HEADS * HEAD_DIM), BF16),
        scratch_shapes=[pltpu.VMEM((nh, 1, tq), F32),
                        pltpu.VMEM((nh, 1, tq), F32),
                        pltpu.VMEM((nh, HEAD_DIM, tq), F32)],
        compiler_params=_params(("parallel", "arbitrary")),
        name="gqa_attention",
    )(qt, k, vt)


def _outproj_kernel(ma_ref, mb_ref, wa_ref, wb_ref, x_ref, g_ref, x1_ref, ht_ref):
    y = x_ref[...]
    y = y + jnp.dot(ma_ref[...], wa_ref[...], preferred_element_type=F32)
    y = y + jnp.dot(mb_ref[...], wb_ref[...], preferred_element_type=F32)
    x1_ref[...] = y
    ht_ref[...] = _rms(y, g_ref[...]).T.astype(BF16)


def out_projection(mixed_a, mixed_b, w_a, w_b, x, g, tm=512):
    s, d = x.shape
    wa, wb = mixed_a.shape[1], mixed_b.shape[1]
    return pl.pallas_call(
        _outproj_kernel,
        grid=(s // tm,),
        in_specs=[pl.BlockSpec((tm, wa), lambda i: (i, 0)),
                  pl.BlockSpec((tm, wb), lambda i: (i, 0)),
                  pl.BlockSpec((wa, d), lambda i: (0, 0)),
                  pl.BlockSpec((wb, d), lambda i: (0, 0)),
                  pl.BlockSpec((tm, d), lambda i: (i, 0)),
                  pl.BlockSpec((1, d), lambda i: (0, 0))],
        out_specs=[pl.BlockSpec((tm, d), lambda i: (i, 0)),
                   pl.BlockSpec((d, tm), lambda i: (0, i))],
        out_shape=[jax.ShapeDtypeStruct((s, d), F32),
                   jax.ShapeDtypeStruct((d, s), BF16)],
        compiler_params=_params(("parallel",)),
        name="out_projection",
    )(mixed_a, mixed_b, w_a, w_b, x, g.reshape(1, d))


N_RANK = PEER_TOPK + 1
RANK_ROWS = 24
CAND_COUNTS = tuple(N_RANK // (a + 1) for a in range(N_RANK))
CAND_ROWS = -(-sum(CAND_COUNTS) // 8) * 8


def _extract_sorted(work, n, emit):
    for j in range(n):
        mj = jnp.max(work, axis=0, keepdims=True)
        emit(j, mj)
        if j + 1 < n:
            work = jnp.where(work == mj, -jnp.inf, work)


def _router_kernel(ht_ref, wqt_ref, keys_ref, thr_ref, w0_ref, s1_ref, e1_ref,
                   q_scr, s_scr, top_scr, cand_scr, *, tb):
    q_scr[...] = jnp.dot(wqt_ref[...], ht_ref[...], preferred_element_type=F32).astype(BF16)
    lane_groups = [slice(lg * LANES, (lg + 1) * LANES) for lg in range(tb // LANES)]

    def half_body(hc, carry):
        qhc = q_scr[pl.ds(pl.multiple_of(hc * PEER_N_KEYS, PEER_N_KEYS), PEER_N_KEYS), :]
        s = jnp.dot(keys_ref[hc], qhc, preferred_element_type=F32)
        s_scr[hc] = s
        for ls in lane_groups:
            def emit(j, mj, ls=ls):
                top_scr[hc, pl.ds(j, 1), ls] = mj

            _extract_sorted(s[:, ls], N_RANK, emit)
        return carry

    lax.fori_loop(0, 2 * PEER_HEADS, half_body, 0)

    def head_body(h, carry):
        t0 = top_scr[2 * h]
        t1 = top_scr[2 * h + 1]
        cand_scr[...] = jnp.full(cand_scr.shape, -jnp.inf, F32)
        off = 0
        for a, nb in enumerate(CAND_COUNTS):
            cand_scr[off:off + nb, :] = t0[a:a + 1, :] + t1[0:nb, :]
            off += nb

        best = []
        _extract_sorted(cand_scr[...], N_RANK, lambda j, mj: best.append(mj))
        theta = 0.5 * (best[PEER_TOPK - 1] + best[PEER_TOPK])
        z = jnp.zeros_like(theta)
        for j in range(PEER_TOPK):
            z = z + jnp.exp(best[j] - best[0])

        s0 = s_scr[2 * h]
        s1 = s_scr[2 * h + 1]
        thr_ref[h] = theta - s0
        w0_ref[h] = jnp.exp(s0 - t0[0:1, :]) / z
        e1 = jnp.exp(s1 - t1[0:1, :])
        for lg, ls in enumerate(lane_groups):
            s1_ref[h, lg] = s1[:, ls]
            e1_ref[h, lg] = e1[:, ls]
        return carry

    lax.fori_loop(0, PEER_HEADS, head_body, 0)


def peer_router(ht, wqt, keys, tb=512):
    d, s = ht.shape
    nq = wqt.shape[0]
    tab = jax.ShapeDtypeStruct((PEER_HEADS, PEER_N_KEYS, s), F32)
    tab_spec = pl.BlockSpec((PEER_HEADS, PEER_N_KEYS, tb), lambda i: (0, 0, i))
    tile = jax.ShapeDtypeStruct((PEER_HEADS, s // LANES, PEER_N_KEYS, LANES), F32)
    tile_spec = pl.BlockSpec((PEER_HEADS, tb // LANES, PEER_N_KEYS, LANES), lambda i: (0, i, 0, 0))
    return pl.pallas_call(
        functools.partial(_router_kernel, tb=tb),
        grid=(s // tb,),
        in_specs=[pl.BlockSpec((d, tb), lambda i: (0, i)),
                  pl.BlockSpec((nq, d), lambda i: (0, 0)),
                  pl.BlockSpec(keys.shape, lambda i: (0, 0, 0))],
        out_specs=[tab_spec, tab_spec, tile_spec, tile_spec],
        out_shape=[tab, tab, tile, tile],
        scratch_shapes=[pltpu.VMEM((nq, tb), BF16),
                        pltpu.VMEM((2 * PEER_HEADS, PEER_N_KEYS, tb), F32),
                        pltpu.VMEM((2 * PEER_HEADS, RANK_ROWS, tb), F32),
                        pltpu.VMEM((CAND_ROWS, tb), F32)],
        compiler_params=_params(("parallel",)),
        name="peer_router",
    )(ht, wqt, keys)


def _gelu_tanh(x):
    return 0.5 * x * (1.0 + jnp.tanh(0.7978845608028654 * (x + 0.044715 * (x * x * x))))


PEER_SUB = 256
GATE_ROWS = 32


TOKEN_HALF = 256


def _data_dependent_zero(tiles):
    bits = None
    for t in tiles:
        b = pltpu.bitcast(t, jnp.uint32)
        bits = b if bits is None else bits | b
    zero_words = (bits >> 16) >> 16
    return zero_words.astype(jnp.int32).astype(F32).astype(BF16)


def _peer_stage(ht_ref, u_ref, v_ref, thr_ref, w0_ref, s1_ref, e1_ref, o_ref,
                at_w, at_r, pt_w, pt_r, us_ref, vs_ref, *, te, tb, i0_base):
    d = u_ref.shape[1]
    n_sub = te // PEER_SUB
    n_i0 = te // PEER_N_KEYS
    k_slices = d // LANES
    tiles = [(lg, i0, q)
             for lg in range(tb // LANES)
             for i0 in range(n_i0)
             for q in range(PEER_N_KEYS // GATE_ROWS)]
    chunks = [(u_ref, us_ref, j, c) for j in range(n_sub) for c in range(k_slices)]
    chunks += [(v_ref, vs_ref, j, c) for c in range(k_slices) for j in range(n_sub)]
    assert len(tiles) % len(chunks) == 0
    tiles_per_chunk = len(tiles) // len(chunks)

    recent = []
    for n, (lg, i0, q) in enumerate(tiles):
        ls = slice(lg * LANES, (lg + 1) * LANES)
        qs = slice(q * GATE_ROWS, (q + 1) * GATE_ROWS)
        er = slice(i0 * PEER_N_KEYS + q * GATE_ROWS, i0 * PEER_N_KEYS + (q + 1) * GATE_ROWS)
        row = i0_base + i0
        gate = None
        for h in range(PEER_HEADS):
            thr = thr_ref[h, row:row + 1, ls]
            w0 = w0_ref[h, row:row + 1, ls]
            term = jnp.where(s1_ref[h, lg, qs, :] >= thr, e1_ref[h, lg, qs, :] * w0, 0.0)
            gate = term if gate is None else gate + term
        p_tile = (_gelu_tanh(at_r[lg, er, :]) * gate).astype(BF16)
        pt_w[er, ls] = p_tile

        recent.append(p_tile)
        if len(recent) == tiles_per_chunk:
            src, dst, j, c = chunks[n // tiles_per_chunk]
            rows = slice(j * PEER_SUB, (j + 1) * PEER_SUB)
            cs = slice(c * LANES, (c + 1) * LANES)
            zero = _data_dependent_zero(recent)
            dst[rows, cs] = src[rows, cs] + pltpu.repeat(zero, PEER_SUB // zero.shape[0], axis=0)
            recent = []

    for j in range(n_sub):
        rows = slice(j * PEER_SUB, (j + 1) * PEER_SUB)
        at = jnp.dot(us_ref[rows, :], ht_ref[...], preferred_element_type=F32)
        for lg in range(tb // LANES):
            at_w[lg, rows, :] = at[:, lg * LANES:(lg + 1) * LANES]

    o_ref[...] += lax.dot_general(pt_r[...], vs_ref[...], (((0,), (0,)), ((), ())),
                                  preferred_element_type=F32)


def _peer_kernel(ht_ref, u_ref, v_ref, thr_ref, w0_ref, s1_ref, e1_ref, x1_ref, g_ref,
                 o_ref, at0_ref, at1_ref, pt0_ref, pt1_ref, us_ref, vs_ref, *, te, tb, n_e,
                 row_group):
    g = pl.program_id(0)
    e_out = (g - 2) % n_e
    out_live = g >= 2
    at_refs = (at0_ref, at1_ref)
    pt_refs = (pt0_ref, pt1_ref)

    @pl.when(g == 0)
    def _():
        for r in at_refs + pt_refs:
            r[...] = jnp.zeros(r.shape, r.dtype)

    @pl.when((g == 0) | (out_live & (e_out == 0)))
    def _():
        o_ref[...] = x1_ref[...]

    for parity in range(2):
        @pl.when(g % 2 == parity)
        def _(parity=parity):
            i0_base = ((1 - parity) % row_group) * (te // PEER_N_KEYS)
            _peer_stage(ht_ref, u_ref, v_ref, thr_ref, w0_ref, s1_ref, e1_ref, o_ref,
                        at_refs[parity], at_refs[1 - parity],
                        pt_refs[1 - parity], pt_refs[parity], us_ref, vs_ref, te=te, tb=tb,
                        i0_base=i0_base)

    @pl.when(out_live & (e_out == n_e - 1))
    def _():
        o_ref[...] = _rms(o_ref[...], g_ref[...])


SUBLANES = 8


def peer_experts(ht, u, v, tables, x1, g, tb=1024, te=512):
    d, s = ht.shape
    n_exp = u.shape[0]
    n_i0 = te // PEER_N_KEYS
    n_e = n_exp // te
    n_items = (s // tb) * n_e
    thr, w0, s1, e1 = tables
    row_group = max(1, SUBLANES // n_i0)
    assert row_group in (1, 2) and n_e % 2 == 0

    def item(gi, lag):
        n = jnp.clip(gi - lag, 0, n_items - 1)
        return n // n_e, n % n_e

    row_spec = pl.BlockSpec((PEER_HEADS, n_i0 * row_group, tb),
                            lambda gi: (0, item(gi, 1)[1] // row_group, item(gi, 1)[0]))
    once = pl.Buffered(1)
    tab_spec = pl.BlockSpec((PEER_HEADS, tb // LANES, PEER_N_KEYS, LANES),
                            lambda gi: (0, item(gi, 1)[0], 0, 0), pipeline_mode=once)
    return pl.pallas_call(
        functools.partial(_peer_kernel, te=te, tb=tb, n_e=n_e, row_group=row_group),
        grid=(n_items + 2,),
        in_specs=[pl.BlockSpec((d, tb), lambda gi: (0, item(gi, 0)[0])),
                  pl.BlockSpec((te, d), lambda gi: (item(gi, 0)[1], 0)),
                  pl.BlockSpec((te, d), lambda gi: (item(gi, 2)[1], 0)),
                  row_spec, row_spec, tab_spec, tab_spec,
                  pl.BlockSpec((tb, d), lambda gi: (item(gi, 2)[0], 0), pipeline_mode=once),
                  pl.BlockSpec((1, d), lambda gi: (0, 0))],
        out_specs=pl.BlockSpec((tb, d), lambda gi: (item(gi, 2)[0], 0), pipeline_mode=once),
        out_shape=jax.ShapeDtypeStruct((s, d), F32),
        scratch_shapes=[pltpu.VMEM((tb // LANES, te, LANES), F32),
                        pltpu.VMEM((tb // LANES, te, LANES), F32),
                        pltpu.VMEM((te, tb), BF16),
                        pltpu.VMEM((te, tb), BF16),
                        pltpu.VMEM((te, d), BF16),
                        pltpu.VMEM((te, d), BF16)],
        compiler_params=_params(("arbitrary",)),
        name="peer_experts",
    )(ht, u, v, thr, w0, s1, e1, x1, g.reshape(1, d))


def kernel(x, norm1_g, w_in, q_norm_g, k_norm_g, w_out, norm2_g, peer_w_query,
           peer_sub_keys, peer_u, peer_v, final_norm_g):
    b, s, d = x.shape
    assert b == 1
    a_width = A_HEADS * HEAD_DIM
    depth = w_in.shape[0]
    xs = x.reshape(s, d)
    for l in range(depth):
        proj = in_projection(xs, norm1_g[l], w_in[l].astype(BF16))
        mixed_a = dilated_attention(proj, s)
        qt, kb, vt = gqa_prep(proj, q_norm_g[l], k_norm_g[l], s, col_block=2)
        mixed_b = gqa_attention(qt, kb, vt, s)
        w_o = w_out[l].astype(BF16)
        x1, ht = out_projection(mixed_a, mixed_b, w_o[:a_width], w_o[a_width:], xs, norm2_g[l])
        keys = peer_sub_keys[l].reshape(2 * PEER_HEADS, PEER_N_KEYS, -1).astype(BF16)
        tables = peer_router(ht, peer_w_query[l].T.astype(BF16), keys)
        assert l == depth - 1
        xs = peer_experts(ht, peer_u[l].astype(BF16), peer_v[l].astype(BF16), tables, x1,
                          final_norm_g)
    return xs.reshape(b, s, d)
```

```python
import functools

import numpy as np
import jax
import jax.numpy as jnp
from jax import lax
from jax.experimental import pallas as pl
from jax.experimental.pallas import tpu as pltpu

F32 = jnp.float32
BF16 = jnp.bfloat16

HEAD_DIM = 128
A_HEADS = 8
DIL_PATTERNS = ((128, 1), (512, 4), (2048, 16))
B_Q_HEADS = 8
B_KV_HEADS = 2
GRID_W = 64
ROPE_THETA = 10000.0
NORM_EPS = 1e-6
PEER_HEADS = 8
PEER_N_KEYS = 128
PEER_TOPK = 16

LANES = 128
SUBLANES = 8
NEG_BIG = -1e30
LOG2_E = 1.4426950408889634
VMEM_LIMIT = 56 * 1024 * 1024


def _params(sem, vmem=VMEM_LIMIT):
    return pltpu.CompilerParams(dimension_semantics=sem, vmem_limit_bytes=vmem)


def _rms(x, g):
    ms = jnp.mean(x * x, axis=-1, keepdims=True)
    return x * lax.rsqrt(ms + NORM_EPS) * g


def _inproj_kernel(x_ref, g_ref, w_ref, o_ref, h_ref):
    @pl.when(pl.program_id(1) == 0)
    def _():
        h_ref[...] = _rms(x_ref[...], g_ref[...]).astype(BF16)

    o_ref[...] = jnp.dot(h_ref[...], w_ref[...], preferred_element_type=F32)


def in_projection(x, g, w_bf16, tm=1024, tn=512):
    s, d = x.shape
    n = w_bf16.shape[1]
    return pl.pallas_call(
        _inproj_kernel,
        grid=(s // tm, n // tn),
        in_specs=[pl.BlockSpec((tm, d), lambda i, j: (i, 0)),
                  pl.BlockSpec((1, d), lambda i, j: (0, 0)),
                  pl.BlockSpec((d, tn), lambda i, j: (0, j))],
        out_specs=pl.BlockSpec((tm, tn), lambda i, j: (i, j)),
        out_shape=jax.ShapeDtypeStruct((s, n), F32),
        scratch_shapes=[pltpu.VMEM((tm, d), BF16)],
        compiler_params=_params(("parallel", "arbitrary")),
        name="in_projection",
    )(x, g.reshape(1, d), w_bf16)


DIL_QBLK = 2048
DIL_SUB = 128
DIL_KWIN = 256
DIL_UNROLL = 4


def _dilated_kernel(q_ref, k_ref, v_ref, o_ref, m_ref, l_ref, acc_ref, *, seq):
    h = pl.program_id(0)
    base = pl.program_id(1) * DIL_QBLK
    slope = jnp.exp2(-(h + 1).astype(F32))
    scale = HEAD_DIM ** -0.5

    m_ref[...] = jnp.full(m_ref.shape, NEG_BIG, F32)
    l_ref[...] = jnp.zeros(l_ref.shape, F32)
    acc_ref[...] = jnp.zeros(acc_ref.shape, F32)

    row = lax.broadcasted_iota(jnp.int32, (DIL_SUB, DIL_KWIN), 0)
    col = lax.broadcasted_iota(jnp.int32, (DIL_SUB, DIL_KWIN), 1)
    col_minus_row = col - row

    for w, d in DIL_PATTERNS:
        half = (w // 2) // d
        assert half * 2 + DIL_SUB == DIL_KWIN
        seg_len = seq // d

        def rows(start, n, d=d):
            return pl.ds(start, n, stride=d) if d > 1 else pl.ds(start, n)

        def body(c, carry, d=d, half=half, seg_len=seg_len, rows=rows):
            r = c % d
            cc = c // d
            p0 = base // d + cc * DIL_SUB
            kp = jnp.clip(p0 - half, 0, seg_len - DIL_KWIN)
            lq = r + d * (cc * DIL_SUB)
            ks = r + d * kp

            q = (q_ref[rows(lq, DIL_SUB), :] * scale).astype(BF16)
            k = k_ref[rows(ks, DIL_KWIN), :].astype(BF16)
            v = v_ref[rows(ks, DIL_KWIN), :].astype(BF16)
            s = lax.dot_general(q, k, (((1,), (1,)), ((), ())), preferred_element_type=F32)
            rel = jnp.abs(col_minus_row + (kp - p0))
            s = jnp.where(rel <= half, s - (slope * d) * rel.astype(F32), NEG_BIG)
            mb = jnp.max(s, axis=1, keepdims=True)
            p = jnp.exp(s - mb)
            lb = jnp.sum(p, axis=1, keepdims=True)
            ob = jnp.dot(p.astype(BF16), v, preferred_element_type=F32)

            idx = rows(lq, DIL_SUB)
            m_old = m_ref[idx, :]
            m_new = jnp.maximum(m_old, mb)
            a_old = jnp.exp(m_old - m_new)
            a_blk = jnp.exp(mb - m_new)
            m_ref[idx, :] = m_new
            l_ref[idx, :] = a_old * l_ref[idx, :] + a_blk * lb
            acc_ref[idx, :] = a_old * acc_ref[idx, :] + a_blk * ob
            return carry

        def group(cg, carry, body=body):
            for u in range(DIL_UNROLL):
                body(cg * DIL_UNROLL + u, carry)
            return carry

        lax.fori_loop(0, DIL_QBLK // DIL_SUB // DIL_UNROLL, group, 0)

    o_ref[...] = (acc_ref[...] / l_ref[...]).astype(o_ref.dtype)


def dilated_attention(proj, seq):
    assert seq % DIL_QBLK == 0
    for _, d in DIL_PATTERNS:
        assert DIL_QBLK // d >= DIL_SUB and seq // d >= DIL_KWIN
    nh = A_HEADS
    return pl.pallas_call(
        functools.partial(_dilated_kernel, seq=seq),
        grid=(nh, seq // DIL_QBLK),
        in_specs=[pl.BlockSpec((DIL_QBLK, HEAD_DIM), lambda h, i: (i, h)),
                  pl.BlockSpec((seq, HEAD_DIM), lambda h, i: (0, nh + h)),
                  pl.BlockSpec((seq, HEAD_DIM), lambda h, i: (0, 2 * nh + h))],
        out_specs=pl.BlockSpec((DIL_QBLK, HEAD_DIM), lambda h, i: (i, h)),
        out_shape=jax.ShapeDtypeStruct((seq, nh * HEAD_DIM), BF16),
        scratch_shapes=[pltpu.VMEM((DIL_QBLK, LANES), F32),
                        pltpu.VMEM((DIL_QBLK, LANES), F32),
                        pltpu.VMEM((DIL_QBLK, HEAD_DIM), F32)],
        compiler_params=_params(("parallel", "arbitrary")),
        name="dilated_attention",
    )(proj, proj, proj)


def _rope_tables(seq):
    rows = seq // GRID_W
    row = jnp.repeat(jnp.arange(rows, dtype=F32), GRID_W)
    col = jnp.tile(jnp.arange(GRID_W, dtype=F32), rows)
    half = HEAD_DIM // 2
    inv = ROPE_THETA ** (-jnp.arange(0, half, 2, dtype=F32) / half)
    ang = jnp.concatenate([row[:, None] * inv, col[:, None] * inv], axis=-1)
    cos = jnp.repeat(jnp.cos(ang), 2, axis=-1)
    sign = jnp.tile(jnp.asarray([-1.0, 1.0], F32), half)
    sin = jnp.repeat(jnp.sin(ang), 2, axis=-1) * sign
    return cos, sin


def _gqa_prep_kernel(p_ref, cos_ref, sin_ref, qg_ref, kg_ref, qt_ref, k_ref, vt_ref):
    cos = cos_ref[...]
    sin = sin_ref[...]
    even = (lax.broadcasted_iota(jnp.int32, cos.shape, 1) % 2) == 0

    def norm_rope(xh, g):
        y = _rms(xh, g)
        partner = jnp.where(even, pltpu.roll(y, LANES - 1, 1), pltpu.roll(y, 1, 1))
        return y * cos + partner * sin

    scale = HEAD_DIM ** -0.5 * LOG2_E
    for hq in range(B_Q_HEADS):
        xh = p_ref[:, hq * HEAD_DIM:(hq + 1) * HEAD_DIM]
        y = norm_rope(xh, qg_ref[...]) * scale
        qt_ref[hq * HEAD_DIM:(hq + 1) * HEAD_DIM, :] = y.T.astype(BF16)
    k0 = B_Q_HEADS * HEAD_DIM
    v0 = k0 + B_KV_HEADS * HEAD_DIM
    for hk in range(B_KV_HEADS):
        xh = p_ref[:, k0 + hk * HEAD_DIM:k0 + (hk + 1) * HEAD_DIM]
        k_ref[:, hk * HEAD_DIM:(hk + 1) * HEAD_DIM] = norm_rope(xh, kg_ref[...]).astype(BF16)
        vh = p_ref[:, v0 + hk * HEAD_DIM:v0 + (hk + 1) * HEAD_DIM]
        vt_ref[hk * HEAD_DIM:(hk + 1) * HEAD_DIM, :] = vh.T.astype(BF16)


def gqa_prep(proj, q_g, k_g, seq, col_block, ts=512):
    wb = (B_Q_HEADS + 2 * B_KV_HEADS) * HEAD_DIM
    cos, sin = _rope_tables(seq)
    return pl.pallas_call(
        _gqa_prep_kernel,
        grid=(seq // ts,),
        in_specs=[pl.BlockSpec((ts, wb), lambda i: (i, col_block)),
                  pl.BlockSpec((ts, HEAD_DIM), lambda i: (i, 0)),
                  pl.BlockSpec((ts, HEAD_DIM), lambda i: (i, 0)),
                  pl.BlockSpec((1, HEAD_DIM), lambda i: (0, 0)),
                  pl.BlockSpec((1, HEAD_DIM), lambda i: (0, 0))],
        out_specs=[pl.BlockSpec((B_Q_HEADS * HEAD_DIM, ts), lambda i: (0, i)),
                   pl.BlockSpec((ts, B_KV_HEADS * HEAD_DIM), lambda i: (i, 0)),
                   pl.BlockSpec((B_KV_HEADS * HEAD_DIM, ts), lambda i: (0, i))],
        out_shape=[jax.ShapeDtypeStruct((B_Q_HEADS * HEAD_DIM, seq), BF16),
                   jax.ShapeDtypeStruct((seq, B_KV_HEADS * HEAD_DIM), BF16),
                   jax.ShapeDtypeStruct((B_KV_HEADS * HEAD_DIM, seq), BF16)],
        compiler_params=_params(("parallel",)),
        name="gqa_prep",
    )(proj, cos, sin, q_g.reshape(1, HEAD_DIM), k_g.reshape(1, HEAD_DIM))


def _gqa_kernel(qt_ref, k_ref, vt_ref, o_ref, m_ref, l_ref, acc_ref, *, seq, tk, nh):
    m_ref[...] = jnp.full(m_ref.shape, NEG_BIG, F32)
    l_ref[...] = jnp.zeros(l_ref.shape, F32)
    acc_ref[...] = jnp.zeros(acc_ref.shape, F32)

    def body(c, carry):
        start = pl.multiple_of(c * tk, tk)
        k = k_ref[pl.ds(start, tk), :]
        vt = vt_ref[:, pl.ds(start, tk)]
        for hh in range(nh):
            qt = qt_ref[hh * HEAD_DIM:(hh + 1) * HEAD_DIM, :]
            st = jnp.dot(k, qt, preferred_element_type=F32)
            m_old = m_ref[hh]
            m_new = jnp.maximum(m_old, jnp.max(st, axis=0, keepdims=True))
            alpha = jnp.exp2(m_old - m_new)
            p = jnp.exp2(st - m_new)
            l_ref[hh] = alpha * l_ref[hh] + jnp.sum(p, axis=0, keepdims=True)
            acc_ref[hh] = alpha * acc_ref[hh] + jnp.dot(vt, p.astype(BF16),
                                                        preferred_element_type=F32)
            m_ref[hh] = m_new
        return carry

    lax.fori_loop(0, seq // tk, body, 0)
    for hh in range(nh):
        o_ref[:, hh * HEAD_DIM:(hh + 1) * HEAD_DIM] = (acc_ref[hh] / l_ref[hh]).T.astype(o_ref.dtype)


def gqa_attention(qt, k, vt, seq, tq=512, tk=512):
    nh = B_Q_HEADS // B_KV_HEADS
    return pl.pallas_call(
        functools.partial(_gqa_kernel, seq=seq, tk=tk, nh=nh),
        grid=(B_KV_HEADS, seq // tq),
        in_specs=[pl.BlockSpec((nh * HEAD_DIM, tq), lambda g, i: (g, i)),
                  pl.BlockSpec((seq, HEAD_DIM), lambda g, i: (0, g)),
                  pl.BlockSpec((HEAD_DIM, seq), lambda g, i: (g, 0))],
        out_specs=pl.BlockSpec((tq, nh * HEAD_DIM), lambda g, i: (i, g)),
        out_shape=jax.ShapeDtypeStruct((seq, B_Q_HEADS * HEAD_DIM), BF16),
        scratch_shapes=[pltpu.VMEM((nh, 1, tq), F32),
                        pltpu.VMEM((nh, 1, tq), F32),
                        pltpu.VMEM((nh, HEAD_DIM, tq), F32)],
        compiler_params=_params(("parallel", "arbitrary")),
        name="gqa_attention",
    )(qt, k, vt)


def _outproj_kernel(ma_ref, mb_ref, wa_ref, wb_ref, x_ref, g_ref, x1_ref, ht_ref):
    y = x_ref[...]
    y = y + jnp.dot(ma_ref[...], wa_ref[...], preferred_element_type=F32)
    y = y + jnp.dot(mb_ref[...], wb_ref[...], preferred_element_type=F32)
    x1_ref[...] = y
    ht_ref[...] = _rms(y, g_ref[...]).T.astype(BF16)


def out_projection(mixed_a, mixed_b, w_a, w_b, x, g, tm=512):
    s, d = x.shape
    wa, wb = mixed_a.shape[1], mixed_b.shape[1]
    return pl.pallas_call(
        _outproj_kernel,
        grid=(s // tm,),
        in_specs=[pl.BlockSpec((tm, wa), lambda i: (i, 0)),
                  pl.BlockSpec((tm, wb), lambda i: (i, 0)),
                  pl.BlockSpec((wa, d), lambda i: (0, 0)),
                  pl.BlockSpec((wb, d), lambda i: (0, 0)),
                  pl.BlockSpec((tm, d), lambda i: (i, 0)),
                  pl.BlockSpec((1, d), lambda i: (0, 0))],
        out_specs=[pl.BlockSpec((tm, d), lambda i: (i, 0)),
                   pl.BlockSpec((d, tm), lambda i: (0, i))],
        out_shape=[jax.ShapeDtypeStruct((s, d), F32),
                   jax.ShapeDtypeStruct((d, s), BF16)],
        compiler_params=_params(("parallel",)),
        name="out_projection",
    )(mixed_a, mixed_b, w_a, w_b, x, g.reshape(1, d))


N_RANK = PEER_TOPK + 1
RANK_ROWS = 24
CAND_COUNTS = tuple(N_RANK // (a + 1) for a in range(N_RANK))
CAND_ROWS = -(-sum(CAND_COUNTS) // 8) * 8


def _extract_sorted(work, n, emit):
    for j in range(n):
        mj = jnp.max(work, axis=0, keepdims=True)
        emit(j, mj)
        if j + 1 < n:
            work = jnp.where(work == mj, -jnp.inf, work)


def _odd_even_merge_sort_pairs(n):
    pairs = []

    def merge(lo, hi, r):
        step = r * 2
        if step < hi - lo:
            merge(lo, hi, step)
            merge(lo + r, hi, step)
            pairs.extend((i, i + r) for i in range(lo + r, hi - r, step))
        else:
            pairs.append((lo, lo + r))

    def sort(lo, hi):
        if hi - lo >= 1:
            mid = lo + (hi - lo) // 2
            sort(lo, mid)
            sort(mid + 1, hi)
            merge(lo, hi, 1)

    sort(0, n - 1)
    return pairs


def _extract_sorted_tile(tile, n, emit):
    depth = tile.shape[0] // SUBLANES
    lists = [tile[SUBLANES * j:SUBLANES * (j + 1)] for j in range(depth)]
    for i, j in _odd_even_merge_sort_pairs(depth):
        lists[i], lists[j] = jnp.maximum(lists[i], lists[j]), jnp.minimum(lists[i], lists[j])
    for r in range(n):
        mj = jnp.max(lists[0], axis=0, keepdims=True)
        emit(r, mj)
        took = lists[0] == mj
        for j in range(min(depth, n - (r + 1))):
            below = lists[j + 1] if j + 1 < depth else -jnp.inf
            lists[j] = jnp.where(took, below, lists[j])


def _router_kernel(ht_ref, wqt_ref, keys_ref, thr_ref, w0_ref, s1_ref, e1_ref,
                   q_scr, s_scr, top_scr, cand_scr, *, tb):
    q_scr[...] = jnp.dot(wqt_ref[...], ht_ref[...], preferred_element_type=F32).astype(BF16)
    lane_groups = [slice(lg * LANES, (lg + 1) * LANES) for lg in range(tb // LANES)]

    def half_body(hc, carry):
        qhc = q_scr[pl.ds(pl.multiple_of(hc * PEER_N_KEYS, PEER_N_KEYS), PEER_N_KEYS), :]
        s = jnp.dot(keys_ref[hc], qhc, preferred_element_type=F32)
        s_scr[hc] = s
        for ls in lane_groups:
            def emit(j, mj, ls=ls):
                top_scr[hc, pl.ds(j, 1), ls] = mj

            _extract_sorted_tile(s[:, ls], N_RANK, emit)
        return carry

    lax.fori_loop(0, 2 * PEER_HEADS, half_body, 0)

    def head_body(h, carry):
        t0 = top_scr[2 * h]
        t1 = top_scr[2 * h + 1]
        cand_scr[...] = jnp.full(cand_scr.shape, -jnp.inf, F32)
        off = 0
        for a, nb in enumerate(CAND_COUNTS):
            cand_scr[off:off + nb, :] = t0[a:a + 1, :] + t1[0:nb, :]
            off += nb

        best = []
        _extract_sorted(cand_scr[...], N_RANK, lambda j, mj: best.append(mj))
        theta = 0.5 * (best[PEER_TOPK - 1] + best[PEER_TOPK])
        z = jnp.zeros_like(theta)
        for j in range(PEER_TOPK):
            z = z + jnp.exp(best[j] - best[0])

        s0 = s_scr[2 * h]
        s1 = s_scr[2 * h + 1]
        thr_ref[h] = theta - s0
        w0_ref[h] = jnp.exp(s0 - t0[0:1, :]) / z
        e1 = jnp.exp(s1 - t1[0:1, :])
        for lg, ls in enumerate(lane_groups):
            s1_ref[h, lg] = s1[:, ls]
            e1_ref[h, lg] = e1[:, ls]
        return carry

    lax.fori_loop(0, PEER_HEADS, head_body, 0)


def peer_router(ht, wqt, keys, tb=512):
    d, s = ht.shape
    nq = wqt.shape[0]
    tab = jax.ShapeDtypeStruct((PEER_HEADS, PEER_N_KEYS, s), F32)
    tab_spec = pl.BlockSpec((PEER_HEADS, PEER_N_KEYS, tb), lambda i: (0, 0, i))
    tile = jax.ShapeDtypeStruct((PEER_HEADS, s // LANES, PEER_N_KEYS, LANES), F32)
    tile_spec = pl.BlockSpec((PEER_HEADS, tb // LANES, PEER_N_KEYS, LANES), lambda i: (0, i, 0, 0))
    return pl.pallas_call(
        functools.partial(_router_kernel, tb=tb),
        grid=(s // tb,),
        in_specs=[pl.BlockSpec((d, tb), lambda i: (0, i)),
                  pl.BlockSpec((nq, d), lambda i: (0, 0)),
                  pl.BlockSpec(keys.shape, lambda i: (0, 0, 0))],
        out_specs=[tab_spec, tab_spec, tile_spec, tile_spec],
        out_shape=[tab, tab, tile, tile],
        scratch_shapes=[pltpu.VMEM((nq, tb), BF16),
                        pltpu.VMEM((2 * PEER_HEADS, PEER_N_KEYS, tb), F32),
                        pltpu.VMEM((2 * PEER_HEADS, RANK_ROWS, tb), F32),
                        pltpu.VMEM((CAND_ROWS, tb), F32)],
        compiler_params=_params(("parallel",)),
        name="peer_router",
    )(ht, wqt, keys)


def _gelu_tanh(x):
    return 0.5 * x * (1.0 + jnp.tanh(0.7978845608028654 * (x + 0.044715 * (x * x * x))))


PEER_SUB = 256
GATE_ROWS = 32


def _peer_stage(ht_ref, u_ref, v_ref, thr_ref, w0_ref, s1_ref, e1_ref, o_ref,
                at_w, at_r, pt_w, pt_r, *, te, tb):
    d = o_ref.shape[1]
    n_sub = te // PEER_SUB
    col_w = d // n_sub

    def hidden_piece(j):
        rows = slice(j * PEER_SUB, (j + 1) * PEER_SUB)
        at = jnp.dot(u_ref[rows, :], ht_ref[...], preferred_element_type=F32)
        for lg in range(tb // LANES):
            at_w[lg, rows, :] = at[:, lg * LANES:(lg + 1) * LANES]

    def value_piece(c):
        cols = slice(c * col_w, (c + 1) * col_w)
        o_ref[:, cols] += lax.dot_general(pt_r[...], v_ref[:, cols], (((0,), (0,)), ((), ())),
                                          preferred_element_type=F32)

    def gate_piece(i0):
        for lg in range(tb // LANES):
            ls = slice(lg * LANES, (lg + 1) * LANES)
            for q in range(PEER_N_KEYS // GATE_ROWS):
                qs = slice(q * GATE_ROWS, (q + 1) * GATE_ROWS)
                er = slice(i0 * PEER_N_KEYS + q * GATE_ROWS, i0 * PEER_N_KEYS + (q + 1) * GATE_ROWS)
                gate = None
                for h in range(PEER_HEADS):
                    thr = thr_ref[h, i0:i0 + 1, ls]
                    w0 = w0_ref[h, i0:i0 + 1, ls]
                    term = jnp.where(s1_ref[h, lg, qs, :] >= thr, e1_ref[h, lg, qs, :] * w0, 0.0)
                    gate = term if gate is None else gate + term
                pt_w[er, ls] = (_gelu_tanh(at_r[lg, er, :]) * gate).astype(BF16)

    mxu_pieces = [functools.partial(hidden_piece, j) for j in range(n_sub)]
    mxu_pieces += [functools.partial(value_piece, c) for c in range(n_sub)]
    n_i0 = te // PEER_N_KEYS
    assert n_i0 == len(mxu_pieces)
    for i0, piece in enumerate(mxu_pieces):
        gate_piece(i0)
        piece()


def _peer_kernel(ht_ref, u_ref, v_ref, thr_ref, w0_ref, s1_ref, e1_ref, x1_ref, g_ref,
                 o_ref, at0_ref, at1_ref, pt0_ref, pt1_ref, *, te, tb, n_e):
    g = pl.program_id(0)
    e_out = (g - 2) % n_e
    out_live = g >= 2
    at_refs = (at0_ref, at1_ref)
    pt_refs = (pt0_ref, pt1_ref)

    @pl.when(g == 0)
    def _():
        for r in at_refs + pt_refs:
            r[...] = jnp.zeros(r.shape, r.dtype)

    @pl.when((g == 0) | (out_live & (e_out == 0)))
    def _():
        o_ref[...] = x1_ref[...]

    for parity in range(2):
        @pl.when(g % 2 == parity)
        def _(parity=parity):
            _peer_stage(ht_ref, u_ref, v_ref, thr_ref, w0_ref, s1_ref, e1_ref, o_ref,
                        at_refs[parity], at_refs[1 - parity],
                        pt_refs[1 - parity], pt_refs[parity], te=te, tb=tb)

    @pl.when(out_live & (e_out == n_e - 1))
    def _():
        o_ref[...] = _rms(o_ref[...], g_ref[...])


def peer_experts(ht, u, v, tables, x1, g, tb=512, te=1024):
    d, s = ht.shape
    n_exp = u.shape[0]
    n_i0 = te // PEER_N_KEYS
    n_e = n_exp // te
    n_items = (s // tb) * n_e
    thr, w0, s1, e1 = tables

    def item(gi, lag):
        n = jnp.clip(gi - lag, 0, n_items - 1)
        return n // n_e, n % n_e

    row_spec = pl.BlockSpec((PEER_HEADS, n_i0, tb), lambda gi: (0, item(gi, 1)[1], item(gi, 1)[0]))
    tab_spec = pl.BlockSpec((PEER_HEADS, tb // LANES, PEER_N_KEYS, LANES),
                            lambda gi: (0, item(gi, 1)[0], 0, 0))
    return pl.pallas_call(
        functools.partial(_peer_kernel, te=te, tb=tb, n_e=n_e),
        grid=(n_items + 2,),
        in_specs=[pl.BlockSpec((d, tb), lambda gi: (0, item(gi, 0)[0])),
                  pl.BlockSpec((te, d), lambda gi: (item(gi, 0)[1], 0)),
                  pl.BlockSpec((te, d), lambda gi: (item(gi, 2)[1], 0)),
                  row_spec, row_spec, tab_spec, tab_spec,
                  pl.BlockSpec((tb, d), lambda gi: (item(gi, 2)[0], 0),
                               pipeline_mode=pl.Buffered(1)),
                  pl.BlockSpec((1, d), lambda gi: (0, 0))],
        out_specs=pl.BlockSpec((tb, d), lambda gi: (item(gi, 2)[0], 0)),
        out_shape=jax.ShapeDtypeStruct((s, d), F32),
        scratch_shapes=[pltpu.VMEM((tb // LANES, te, LANES), F32),
                        pltpu.VMEM((tb // LANES, te, LANES), F32),
                        pltpu.VMEM((te, tb), BF16),
                        pltpu.VMEM((te, tb), BF16)],
        compiler_params=_params(("arbitrary",)),
        name="peer_experts",
    )(ht, u, v, thr, w0, s1, e1, x1, g.reshape(1, d))


def kernel(x, norm1_g, w_in, q_norm_g, k_norm_g, w_out, norm2_g, peer_w_query,
           peer_sub_keys, peer_u, peer_v, final_norm_g):
    b, s, d = x.shape
    assert b == 1
    a_width = A_HEADS * HEAD_DIM
    depth = w_in.shape[0]
    xs = x.reshape(s, d)
    for l in range(depth):
        proj = in_projection(xs, norm1_g[l], w_in[l].astype(BF16))
        mixed_a = dilated_attention(proj, s)
        qt, kb, vt = gqa_prep(proj, q_norm_g[l], k_norm_g[l], s, col_block=2)
        mixed_b = gqa_attention(qt, kb, vt, s)
        w_o = w_out[l].astype(BF16)
        x1, ht = out_projection(mixed_a, mixed_b, w_o[:a_width], w_o[a_width:], xs, norm2_g[l])
        keys = peer_sub_keys[l].reshape(2 * PEER_HEADS, PEER_N_KEYS, -1).astype(BF16)
        tables = peer_router(ht, peer_w_query[l].T.astype(BF16), keys)
        assert l == depth - 1
        xs = peer_experts(ht, peer_u[l].astype(BF16), peer_v[l].astype(BF16), tables, x1,
                          final_norm_g)
    return xs.reshape(b, s, d)
```

```python
import functools

import numpy as np
import jax
import jax.numpy as jnp
from jax import lax
from jax.experimental import pallas as pl
from jax.experimental.pallas import tpu as pltpu

F32 = jnp.float32
BF16 = jnp.bfloat16

HEAD_DIM = 128
A_HEADS = 8
DIL_PATTERNS = ((128, 1), (512, 4), (2048, 16))
B_Q_HEADS = 8
B_KV_HEADS = 2
GRID_W = 64
ROPE_THETA = 10000.0
NORM_EPS = 1e-6
PEER_HEADS = 8
PEER_N_KEYS = 128
PEER_TOPK = 16

LANES = 128
SUBLANES = 8
NEG_BIG = -1e30
LOG2_E = 1.4426950408889634
VMEM_LIMIT = 56 * 1024 * 1024


def _params(sem, vmem=VMEM_LIMIT):
    return pltpu.CompilerParams(dimension_semantics=sem, vmem_limit_bytes=vmem)


def _rms(x, g):
    ms = jnp.mean(x * x, axis=-1, keepdims=True)
    return x * lax.rsqrt(ms + NORM_EPS) * g


def _inproj_kernel(x_ref, g_ref, w_ref, o_ref, h_ref):
    @pl.when(pl.program_id(1) == 0)
    def _():
        h_ref[...] = _rms(x_ref[...], g_ref[...]).astype(BF16)

    o_ref[...] = jnp.dot(h_ref[...], w_ref[...], preferred_element_type=F32)


def in_projection(x, g, w_bf16, tm=1024, tn=512):
    s, d = x.shape
    n = w_bf16.shape[1]
    return pl.pallas_call(
        _inproj_kernel,
        grid=(s // tm, n // tn),
        in_specs=[pl.BlockSpec((tm, d), lambda i, j: (i, 0)),
                  pl.BlockSpec((1, d), lambda i, j: (0, 0)),
                  pl.BlockSpec((d, tn), lambda i, j: (0, j))],
        out_specs=pl.BlockSpec((tm, tn), lambda i, j: (i, j)),
        out_shape=jax.ShapeDtypeStruct((s, n), F32),
        scratch_shapes=[pltpu.VMEM((tm, d), BF16)],
        compiler_params=_params(("parallel", "arbitrary")),
        name="in_projection",
    )(x, g.reshape(1, d), w_bf16)


DIL_QBLK = 2048
DIL_SUB = 128
DIL_KWIN = 256
DIL_UNROLL = 4
DIL_EDGE_CASES = 3


def _dilated_kernel(q_ref, k_ref, v_ref, o_ref, m_ref, l_ref, acc_ref, bias_ref, *, seq):
    h = pl.program_id(0)
    base = pl.program_id(1) * DIL_QBLK
    slope = jnp.exp2(-(h + 1).astype(F32))
    scale = HEAD_DIM ** -0.5

    m_ref[...] = jnp.full(m_ref.shape, NEG_BIG, F32)
    l_ref[...] = jnp.zeros(l_ref.shape, F32)
    acc_ref[...] = jnp.zeros(acc_ref.shape, F32)

    row = lax.broadcasted_iota(jnp.int32, (DIL_SUB, DIL_KWIN), 0)
    col = lax.broadcasted_iota(jnp.int32, (DIL_SUB, DIL_KWIN), 1)
    col_minus_row = col - row

    for b_idx, (w, d) in enumerate(DIL_PATTERNS):
        half = (w // 2) // d
        assert half * 2 + DIL_SUB == DIL_KWIN
        seg_len = seq // d

        for v_idx in range(DIL_EDGE_CASES):
            rel = jnp.abs(col_minus_row - v_idx * half)
            bias_ref[b_idx * DIL_EDGE_CASES + v_idx] = jnp.where(
                rel <= half, -(slope * d) * rel.astype(F32), NEG_BIG)

        def rows(start, n, d=d):
            return pl.ds(start, n, stride=d) if d > 1 else pl.ds(start, n)

        def body(c, carry, d=d, half=half, seg_len=seg_len, rows=rows, b_idx=b_idx):
            r = c % d
            cc = c // d
            p0 = base // d + cc * DIL_SUB
            kp = jnp.clip(p0 - half, 0, seg_len - DIL_KWIN)
            lq = r + d * (cc * DIL_SUB)
            ks = r + d * kp

            q = (q_ref[rows(lq, DIL_SUB), :] * scale).astype(BF16)
            k = k_ref[rows(ks, DIL_KWIN), :].astype(BF16)
            v = v_ref[rows(ks, DIL_KWIN), :].astype(BF16)
            s = lax.dot_general(q, k, (((1,), (1,)), ((), ())), preferred_element_type=F32)
            s = s + bias_ref[b_idx * DIL_EDGE_CASES + (p0 - kp) // half]
            mb = jnp.max(s, axis=1, keepdims=True)
            p = jnp.exp(s - mb)
            lb = jnp.sum(p, axis=1, keepdims=True)
            ob = jnp.dot(p.astype(BF16), v, preferred_element_type=F32)

            idx = rows(lq, DIL_SUB)
            m_old = m_ref[idx, :]
            m_new = jnp.maximum(m_old, mb)
            a_old = jnp.exp(m_old - m_new)
            a_blk = jnp.exp(mb - m_new)
            m_ref[idx, :] = m_new
            l_ref[idx, :] = a_old * l_ref[idx, :] + a_blk * lb
            acc_ref[idx, :] = a_old * acc_ref[idx, :] + a_blk * ob
            return carry

        def group(cg, carry, body=body):
            for u in range(DIL_UNROLL):
                body(cg * DIL_UNROLL + u, carry)
            return carry

        lax.fori_loop(0, DIL_QBLK // DIL_SUB // DIL_UNROLL, group, 0)

    o_ref[...] = (acc_ref[...] / l_ref[...]).astype(o_ref.dtype)


def dilated_attention(proj, seq):
    assert seq % DIL_QBLK == 0
    for _, d in DIL_PATTERNS:
        assert DIL_QBLK // d >= DIL_SUB and seq // d >= DIL_KWIN
    nh = A_HEADS
    return pl.pallas_call(
        functools.partial(_dilated_kernel, seq=seq),
        grid=(nh, seq // DIL_QBLK),
        in_specs=[pl.BlockSpec((DIL_QBLK, HEAD_DIM), lambda h, i: (i, h)),
                  pl.BlockSpec((seq, HEAD_DIM), lambda h, i: (0, nh + h)),
                  pl.BlockSpec((seq, HEAD_DIM), lambda h, i: (0, 2 * nh + h))],
        out_specs=pl.BlockSpec((DIL_QBLK, HEAD_DIM), lambda h, i: (i, h)),
        out_shape=jax.ShapeDtypeStruct((seq, nh * HEAD_DIM), BF16),
        scratch_shapes=[pltpu.VMEM((DIL_QBLK, LANES), F32),
                        pltpu.VMEM((DIL_QBLK, LANES), F32),
                        pltpu.VMEM((DIL_QBLK, HEAD_DIM), F32),
                        pltpu.VMEM((len(DIL_PATTERNS) * DIL_EDGE_CASES, DIL_SUB, DIL_KWIN), F32)],
        compiler_params=_params(("parallel", "arbitrary")),
        name="dilated_attention",
    )(proj, proj, proj)


def _rope_tables(seq):
    rows = seq // GRID_W
    row = jnp.repeat(jnp.arange(rows, dtype=F32), GRID_W)
    col = jnp.tile(jnp.arange(GRID_W, dtype=F32), rows)
    half = HEAD_DIM // 2
    inv = ROPE_THETA ** (-jnp.arange(0, half, 2, dtype=F32) / half)
    ang = jnp.concatenate([row[:, None] * inv, col[:, None] * inv], axis=-1)
    cos = jnp.repeat(jnp.cos(ang), 2, axis=-1)
    sign = jnp.tile(jnp.asarray([-1.0, 1.0], F32), half)
    sin = jnp.repeat(jnp.sin(ang), 2, axis=-1) * sign
    return cos, sin


def _gqa_prep_kernel(p_ref, cos_ref, sin_ref, qg_ref, kg_ref, qt_ref, k_ref, vt_ref):
    cos = cos_ref[...]
    sin = sin_ref[...]
    even = (lax.broadcasted_iota(jnp.int32, cos.shape, 1) % 2) == 0

    def norm_rope(xh, g):
        y = _rms(xh, g)
        partner = jnp.where(even, pltpu.roll(y, LANES - 1, 1), pltpu.roll(y, 1, 1))
        return y * cos + partner * sin

    scale = HEAD_DIM ** -0.5 * LOG2_E
    for hq in range(B_Q_HEADS):
        xh = p_ref[:, hq * HEAD_DIM:(hq + 1) * HEAD_DIM]
        y = norm_rope(xh, qg_ref[...]) * scale
        qt_ref[hq * HEAD_DIM:(hq + 1) * HEAD_DIM, :] = y.T.astype(BF16)
    k0 = B_Q_HEADS * HEAD_DIM
    v0 = k0 + B_KV_HEADS * HEAD_DIM
    for hk in range(B_KV_HEADS):
        xh = p_ref[:, k0 + hk * HEAD_DIM:k0 + (hk + 1) * HEAD_DIM]
        k_ref[:, hk * HEAD_DIM:(hk + 1) * HEAD_DIM] = norm_rope(xh, kg_ref[...]).astype(BF16)
        vh = p_ref[:, v0 + hk * HEAD_DIM:v0 + (hk + 1) * HEAD_DIM]
        vt_ref[hk * HEAD_DIM:(hk + 1) * HEAD_DIM, :] = vh.T.astype(BF16)


def gqa_prep(proj, q_g, k_g, seq, col_block, ts=512):
    wb = (B_Q_HEADS + 2 * B_KV_HEADS) * HEAD_DIM
    cos, sin = _rope_tables(seq)
    return pl.pallas_call(
        _gqa_prep_kernel,
        grid=(seq // ts,),
        in_specs=[pl.BlockSpec((ts, wb), lambda i: (i, col_block)),
                  pl.BlockSpec((ts, HEAD_DIM), lambda i: (i, 0)),
                  pl.BlockSpec((ts, HEAD_DIM), lambda i: (i, 0)),
                  pl.BlockSpec((1, HEAD_DIM), lambda i: (0, 0)),
                  pl.BlockSpec((1, HEAD_DIM), lambda i: (0, 0))],
        out_specs=[pl.BlockSpec((B_Q_HEADS * HEAD_DIM, ts), lambda i: (0, i)),
                   pl.BlockSpec((ts, B_KV_HEADS * HEAD_DIM), lambda i: (i, 0)),
                   pl.BlockSpec((B_KV_HEADS * HEAD_DIM, ts), lambda i: (0, i))],
        out_shape=[jax.ShapeDtypeStruct((B_Q_HEADS * HEAD_DIM, seq), BF16),
                   jax.ShapeDtypeStruct((seq, B_KV_HEADS * HEAD_DIM), BF16),
                   jax.ShapeDtypeStruct((B_KV_HEADS * HEAD_DIM, seq), BF16)],
        compiler_params=_params(("parallel",)),
        name="gqa_prep",
    )(proj, cos, sin, q_g.reshape(1, HEAD_DIM), k_g.reshape(1, HEAD_DIM))


STAGE_ROWS = 32


def _data_dependent_zero(tiles):
    bits = None
    for t in tiles:
        b = pltpu.bitcast(t, jnp.uint32)
        bits = b if bits is None else bits | b
    zero_words = (bits >> 16) >> 16
    return zero_words.astype(jnp.int32).astype(F32).astype(BF16)


def _gqa_kernel(qt_ref, k_ref, vt_ref, o_ref, m_ref, l_ref, acc_ref, st_ref, ks_ref,
                *, seq, tk, nh):
    n_chunks = seq // tk
    tq = qt_ref.shape[1]
    m_ref[...] = jnp.full(m_ref.shape, NEG_BIG, F32)
    l_ref[...] = jnp.zeros(l_ref.shape, F32)
    acc_ref[...] = jnp.zeros(acc_ref.shape, F32)

    def q_head(hh):
        return qt_ref[hh * HEAD_DIM:(hh + 1) * HEAD_DIM, :]

    def chunk_start(c):
        return pl.multiple_of(jnp.minimum(c, n_chunks - 1) * tk, tk)

    st_ref[...] = jnp.dot(k_ref[pl.ds(0, tk), :], q_head(0), preferred_element_type=F32)

    def body(c, carry):
        here = chunk_start(c)
        vt = vt_ref[:, pl.ds(here, tk)]
        st = st_ref[...]
        for hh in range(nh):
            m_old = m_ref[hh]
            m_new = jnp.maximum(m_old, jnp.max(st, axis=0, keepdims=True))
            alpha = jnp.exp2(m_old - m_new)
            p = jnp.exp2(st - m_new)
            pb = p.astype(BF16)

            nxt = here if hh + 1 < nh else chunk_start(c + 1)
            slot = ks_ref.at[hh % 2]
            for r in range(0, tk, STAGE_ROWS):
                strip = [pb[r:r + STAGE_ROWS, j:j + LANES] for j in range(0, tq, LANES)]
                zero = _data_dependent_zero(strip)
                piece = k_ref[pl.ds(pl.multiple_of(nxt + r, STAGE_ROWS), STAGE_ROWS), :]
                slot[r:r + STAGE_ROWS, :] = piece + pltpu.repeat(zero, STAGE_ROWS // zero.shape[0], axis=0)
            st = jnp.dot(slot[...], q_head((hh + 1) % nh), preferred_element_type=F32)

            l_ref[hh] = alpha * l_ref[hh] + jnp.sum(p, axis=0, keepdims=True)
            acc_ref[hh] = alpha * acc_ref[hh] + jnp.dot(vt, pb, preferred_element_type=F32)
            m_ref[hh] = m_new
        st_ref[...] = st
        return carry

    lax.fori_loop(0, n_chunks, body, 0)
    for hh in range(nh):
        o_ref[:, hh * HEAD_DIM:(hh + 1) * HEAD_DIM] = (acc_ref[hh] / l_ref[hh]).T.astype(o_ref.dtype)


def gqa_attention(qt, k, vt, seq, tq=512, tk=512):
    nh = B_Q_HEADS // B_KV_HEADS
    return pl.pallas_call(
        functools.partial(_gqa_kernel, seq=seq, tk=tk, nh=nh),
        grid=(B_KV_HEADS, seq // tq),
        in_specs=[pl.BlockSpec((nh * HEAD_DIM, tq), lambda g, i: (g, i)),
                  pl.BlockSpec((seq, HEAD_DIM), lambda g, i: (0, g)),
                  pl.BlockSpec((HEAD_DIM, seq), lambda g, i: (g, 0))],
        out_specs=pl.BlockSpec((tq, nh * HEAD_DIM), lambda g, i: (i, g)),
        out_shape=jax.ShapeDtypeStruct((seq, B_Q_HEADS * HEAD_DIM), BF16),
        scratch_shapes=[pltpu.VMEM((nh, 1, tq), F32),
                        pltpu.VMEM((nh, 1, tq), F32),
                        pltpu.VMEM((nh, HEAD_DIM, tq), F32),
                        pltpu.VMEM((tk, tq), F32),
                        pltpu.VMEM((2, tk, HEAD_DIM), BF16)],
        compiler_params=_params(("parallel", "arbitrary")),
        name="gqa_attention",
    )(qt, k, vt)


def _outproj_kernel(ma_ref, mb_ref, wa_ref, wb_ref, x_ref, g_ref, x1_ref, ht_ref):
    y = x_ref[...]
    y = y + jnp.dot(ma_ref[...], wa_ref[...], preferred_element_type=F32)
    y = y + jnp.dot(mb_ref[...], wb_ref[...], preferred_element_type=F32)
    x1_ref[...] = y
    ht_ref[...] = _rms(y, g_ref[...]).T.astype(BF16)


def out_projection(mixed_a, mixed_b, w_a, w_b, x, g, tm=512):
    s, d = x.shape
    wa, wb = mixed_a.shape[1], mixed_b.shape[1]
    return pl.pallas_call(
        _outproj_kernel,
        grid=(s // tm,),
        in_specs=[pl.BlockSpec((tm, wa), lambda i: (i, 0)),
                  pl.BlockSpec((tm, wb), lambda i: (i, 0)),
                  pl.BlockSpec((wa, d), lambda i: (0, 0)),
                  pl.BlockSpec((wb, d), lambda i: (0, 0)),
                  pl.BlockSpec((tm, d), lambda i: (i, 0)),
                  pl.BlockSpec((1, d), lambda i: (0, 0))],
        out_specs=[pl.BlockSpec((tm, d), lambda i: (i, 0)),
                   pl.BlockSpec((d, tm), lambda i: (0, i))],
        out_shape=[jax.ShapeDtypeStruct((s, d), F32),
                   jax.ShapeDtypeStruct((d, s), BF16)],
        compiler_params=_params(("parallel",)),
        name="out_projection",
    )(mixed_a, mixed_b, w_a, w_b, x, g.reshape(1, d))


N_RANK = PEER_TOPK + 1
RANK_ROWS = 24
CAND_COUNTS = tuple(N_RANK // (a + 1) for a in range(N_RANK))
CAND_ROWS = -(-sum(CAND_COUNTS) // 8) * 8


def _extract_sorted(work, n, emit):
    for j in range(n):
        mj = jnp.max(work, axis=0, keepdims=True)
        emit(j, mj)
        if j + 1 < n:
            work = jnp.where(work == mj, -jnp.inf, work)


def _odd_even_merge_sort_pairs(n):
    pairs = []

    def merge(lo, hi, r):
        step = r * 2
        if step < hi - lo:
            merge(lo, hi, step)
            merge(lo + r, hi, step)
            pairs.extend((i, i + r) for i in range(lo + r, hi - r, step))
        else:
            pairs.append((lo, lo + r))

    def sort(lo, hi):
        if hi - lo >= 1:
            mid = lo + (hi - lo) // 2
            sort(lo, mid)
            sort(mid + 1, hi)
            merge(lo, hi, 1)

    sort(0, n - 1)
    return pairs


def _extract_sorted_tile(tile, n, emit):
    depth = tile.shape[0] // SUBLANES
    lists = [tile[SUBLANES * j:SUBLANES * (j + 1)] for j in range(depth)]
    for i, j in _odd_even_merge_sort_pairs(depth):
        lists[i], lists[j] = jnp.maximum(lists[i], lists[j]), jnp.minimum(lists[i], lists[j])
    for r in range(n):
        mj = jnp.max(lists[0], axis=0, keepdims=True)
        emit(r, mj)
        took = lists[0] == mj
        for j in range(min(depth, n - (r + 1))):
            below = lists[j + 1] if j + 1 < depth else -jnp.inf
            lists[j] = jnp.where(took, below, lists[j])


def _router_kernel(ht_ref, wqt_ref, keys_ref, thr_ref, w0_ref, s1_ref, e1_ref,
                   q_scr, s_scr, top_scr, cand_scr, *, tb):
    q_scr[...] = jnp.dot(wqt_ref[...], ht_ref[...], preferred_element_type=F32).astype(BF16)
    lane_groups = [slice(lg * LANES, (lg + 1) * LANES) for lg in range(tb // LANES)]

    def half_body(hc, carry):
        qhc = q_scr[pl.ds(pl.multiple_of(hc * PEER_N_KEYS, PEER_N_KEYS), PEER_N_KEYS), :]
        s = jnp.dot(keys_ref[hc], qhc, preferred_element_type=F32)
        s_scr[hc] = s
        for ls in lane_groups:
            def emit(j, mj, ls=ls):
                top_scr[hc, pl.ds(j, 1), ls] = mj

            _extract_sorted_tile(s[:, ls], N_RANK, emit)
        return carry

    lax.fori_loop(0, 2 * PEER_HEADS, half_body, 0)

    def head_body(h, carry):
        t0 = top_scr[2 * h]
        t1 = top_scr[2 * h + 1]
        cand_scr[...] = jnp.full(cand_scr.shape, -jnp.inf, F32)
        off = 0
        for a, nb in enumerate(CAND_COUNTS):
            cand_scr[off:off + nb, :] = t0[a:a + 1, :] + t1[0:nb, :]
            off += nb

        best = []
        _extract_sorted(cand_scr[...], N_RANK, lambda j, mj: best.append(mj))
        theta = 0.5 * (best[PEER_TOPK - 1] + best[PEER_TOPK])
        z = jnp.zeros_like(theta)
        for j in range(PEER_TOPK):
            z = z + jnp.exp(best[j] - best[0])

        s0 = s_scr[2 * h]
        s1 = s_scr[2 * h + 1]
        thr_ref[h] = theta - s0
        w0_ref[h] = jnp.exp(s0 - t0[0:1, :]) / z
        e1 = jnp.exp(s1 - t1[0:1, :])
        for lg, ls in enumerate(lane_groups):
            s1_ref[h, lg] = s1[:, ls]
            e1_ref[h, lg] = e1[:, ls]
        return carry

    lax.fori_loop(0, PEER_HEADS, head_body, 0)


def peer_router(ht, wqt, keys, tb=512):
    d, s = ht.shape
    nq = wqt.shape[0]
    tab = jax.ShapeDtypeStruct((PEER_HEADS, PEER_N_KEYS, s), F32)
    tab_spec = pl.BlockSpec((PEER_HEADS, PEER_N_KEYS, tb), lambda i: (0, 0, i))
    tile = jax.ShapeDtypeStruct((PEER_HEADS, s // LANES, PEER_N_KEYS, LANES), F32)
    tile_spec = pl.BlockSpec((PEER_HEADS, tb // LANES, PEER_N_KEYS, LANES), lambda i: (0, i, 0, 0))
    return pl.pallas_call(
        functools.partial(_router_kernel, tb=tb),
        grid=(s // tb,),
        in_specs=[pl.BlockSpec((d, tb), lambda i: (0, i)),
                  pl.BlockSpec((nq, d), lambda i: (0, 0)),
                  pl.BlockSpec(keys.shape, lambda i: (0, 0, 0))],
        out_specs=[tab_spec, tab_spec, tile_spec, tile_spec],
        out_shape=[tab, tab, tile, tile],
        scratch_shapes=[pltpu.VMEM((nq, tb), BF16),
                        pltpu.VMEM((2 * PEER_HEADS, PEER_N_KEYS, tb), F32),
                        pltpu.VMEM((2 * PEER_HEADS, RANK_ROWS, tb), F32),
                        pltpu.VMEM((CAND_ROWS, tb), F32)],
        compiler_params=_params(("parallel",)),
        name="peer_router",
    )(ht, wqt, keys)


def _gelu_tanh(x):
    return 0.5 * x * (1.0 + jnp.tanh(0.7978845608028654 * (x + 0.044715 * (x * x * x))))


PEER_SUB = 256
GATE_ROWS = 32


def _peer_stage(ht_ref, u_ref, v_ref, thr_ref, w0_ref, s1_ref, e1_ref, o_ref,
                at_w, at_r, pt_w, pt_r, *, te, tb):
    d = o_ref.shape[1]
    n_sub = te // PEER_SUB
    col_w = d // n_sub

    def hidden_piece(j):
        rows = slice(j * PEER_SUB, (j + 1) * PEER_SUB)
        at = jnp.dot(u_ref[rows, :], ht_ref[...], preferred_element_type=F32)
        for lg in range(tb // LANES):
            at_w[lg, rows, :] = at[:, lg * LANES:(lg + 1) * LANES]

    def value_piece(c):
        cols = slice(c * col_w, (c + 1) * col_w)
        o_ref[:, cols] += lax.dot_general(pt_r[...], v_ref[:, cols], (((0,), (0,)), ((), ())),
                                          preferred_element_type=F32)

    def gate_piece(i0):
        for lg in range(tb // LANES):
            ls = slice(lg * LANES, (lg + 1) * LANES)
            for q in range(PEER_N_KEYS // GATE_ROWS):
                qs = slice(q * GATE_ROWS, (q + 1) * GATE_ROWS)
                er = slice(i0 * PEER_N_KEYS + q * GATE_ROWS, i0 * PEER_N_KEYS + (q + 1) * GATE_ROWS)
                gate = None
                for h in range(PEER_HEADS):
                    thr = thr_ref[h, i0:i0 + 1, ls]
                    w0 = w0_ref[h, i0:i0 + 1, ls]
                    term = jnp.where(s1_ref[h, lg, qs, :] >= thr, e1_ref[h, lg, qs, :] * w0, 0.0)
                    gate = term if gate is None else gate + term
                pt_w[er, ls] = (_gelu_tanh(at_r[lg, er, :]) * gate).astype(BF16)

    mxu_pieces = [functools.partial(hidden_piece, j) for j in range(n_sub)]
    mxu_pieces += [functools.partial(value_piece, c) for c in range(n_sub)]
    n_i0 = te // PEER_N_KEYS
    assert n_i0 == len(mxu_pieces)
    for i0, piece in enumerate(mxu_pieces):
        gate_piece(i0)
        piece()


def _peer_kernel(ht_ref, u_ref, v_ref, thr_ref, w0_ref, s1_ref, e1_ref, x1_ref, g_ref,
                 o_ref, at0_ref, at1_ref, pt0_ref, pt1_ref, *, te, tb, n_e):
    g = pl.program_id(0)
    e_out = (g - 2) % n_e
    out_live = g >= 2
    at_refs = (at0_ref, at1_ref)
    pt_refs = (pt0_ref, pt1_ref)

    @pl.when(g == 0)
    def _():
        for r in at_refs + pt_refs:
            r[...] = jnp.zeros(r.shape, r.dtype)

    @pl.when((g == 0) | (out_live & (e_out == 0)))
    def _():
        o_ref[...] = x1_ref[...]

    for parity in range(2):
        @pl.when(g % 2 == parity)
        def _(parity=parity):
            _peer_stage(ht_ref, u_ref, v_ref, thr_ref, w0_ref, s1_ref, e1_ref, o_ref,
                        at_refs[parity], at_refs[1 - parity],
                        pt_refs[1 - parity], pt_refs[parity], te=te, tb=tb)

    @pl.when(out_live & (e_out == n_e - 1))
    def _():
        o_ref[...] = _rms(o_ref[...], g_ref[...])


def peer_experts(ht, u, v, tables, x1, g, tb=512, te=1024):
    d, s = ht.shape
    n_exp = u.shape[0]
    n_i0 = te // PEER_N_KEYS
    n_e = n_exp // te
    n_items = (s // tb) * n_e
    thr, w0, s1, e1 = tables

    def item(gi, lag):
        n = jnp.clip(gi - lag, 0, n_items - 1)
        return n // n_e, n % n_e

    row_spec = pl.BlockSpec((PEER_HEADS, n_i0, tb), lambda gi: (0, item(gi, 1)[1], item(gi, 1)[0]))
    tab_spec = pl.BlockSpec((PEER_HEADS, tb // LANES, PEER_N_KEYS, LANES),
                            lambda gi: (0, item(gi, 1)[0], 0, 0))
    return pl.pallas_call(
        functools.partial(_peer_kernel, te=te, tb=tb, n_e=n_e),
        grid=(n_items + 2,),
        in_specs=[pl.BlockSpec((d, tb), lambda gi: (0, item(gi, 0)[0])),
                  pl.BlockSpec((te, d), lambda gi: (item(gi, 0)[1], 0)),
                  pl.BlockSpec((te, d), lambda gi: (item(gi, 2)[1], 0)),
                  row_spec, row_spec, tab_spec, tab_spec,
                  pl.BlockSpec((tb, d), lambda gi: (item(gi, 2)[0], 0),
                               pipeline_mode=pl.Buffered(1)),
                  pl.BlockSpec((1, d), lambda gi: (0, 0))],
        out_specs=pl.BlockSpec((tb, d), lambda gi: (item(gi, 2)[0], 0)),
        out_shape=jax.ShapeDtypeStruct((s, d), F32),
        scratch_shapes=[pltpu.VMEM((tb // LANES, te, LANES), F32),
                        pltpu.VMEM((tb // LANES, te, LANES), F32),
                        pltpu.VMEM((te, tb), BF16),
                        pltpu.VMEM((te, tb), BF16)],
        compiler_params=_params(("arbitrary",)),
        name="peer_experts",
    )(ht, u, v, thr, w0, s1, e1, x1, g.reshape(1, d))


def kernel(x, norm1_g, w_in, q_norm_g, k_norm_g, w_out, norm2_g, peer_w_query,
           peer_sub_keys, peer_u, peer_v, final_norm_g):
    b, s, d = x.shape
    assert b == 1
    a_width = A_HEADS * HEAD_DIM
    depth = w_in.shape[0]
    xs = x.reshape(s, d)
    for l in range(depth):
        proj = in_projection(xs, norm1_g[l], w_in[l].astype(BF16))
        mixed_a = dilated_attention(proj, s)
        qt, kb, vt = gqa_prep(proj, q_norm_g[l], k_norm_g[l], s, col_block=2)
        mixed_b = gqa_attention(qt, kb, vt, s)
        w_o = w_out[l].astype(BF16)
        x1, ht = out_projection(mixed_a, mixed_b, w_o[:a_width], w_o[a_width:], xs, norm2_g[l])
        keys = peer_sub_keys[l].reshape(2 * PEER_HEADS, PEER_N_KEYS, -1).astype(BF16)
        tables = peer_router(ht, peer_w_query[l].T.astype(BF16), keys)
        assert l == depth - 1
        xs = peer_experts(ht, peer_u[l].astype(BF16), peer_v[l].astype(BF16), tables, x1,
                          final_norm_g)
    return xs.reshape(b, s, d)
```

```python
import functools

import numpy as np
import jax
import jax.numpy as jnp
from jax import lax
from jax.experimental import pallas as pl
from jax.experimental.pallas import tpu as pltpu

F32 = jnp.float32
BF16 = jnp.bfloat16

HEAD_DIM = 128
A_HEADS = 8
DIL_PATTERNS = ((128, 1), (512, 4), (2048, 16))
B_Q_HEADS = 8
B_KV_HEADS = 2
GRID_W = 64
ROPE_THETA = 10000.0
NORM_EPS = 1e-6
PEER_HEADS = 8
PEER_N_KEYS = 128
PEER_TOPK = 16

LANES = 128
SUBLANES = 8
NEG_BIG = -1e30
LOG2_E = 1.4426950408889634
VMEM_LIMIT = 56 * 1024 * 1024


def _params(sem, vmem=VMEM_LIMIT):
    return pltpu.CompilerParams(dimension_semantics=sem, vmem_limit_bytes=vmem)


def _rms(x, g):
    ms = jnp.mean(x * x, axis=-1, keepdims=True)
    return x * lax.rsqrt(ms + NORM_EPS) * g


def _inproj_kernel(x_ref, g_ref, w_ref, o_ref, h_ref):
    @pl.when(pl.program_id(1) == 0)
    def _():
        h_ref[...] = _rms(x_ref[...], g_ref[...]).astype(BF16)

    o_ref[...] = jnp.dot(h_ref[...], w_ref[...], preferred_element_type=F32)


def in_projection(x, g, w_bf16, tm=1024, tn=512):
    s, d = x.shape
    n = w_bf16.shape[1]
    return pl.pallas_call(
        _inproj_kernel,
        grid=(s // tm, n // tn),
        in_specs=[pl.BlockSpec((tm, d), lambda i, j: (i, 0)),
                  pl.BlockSpec((1, d), lambda i, j: (0, 0)),
                  pl.BlockSpec((d, tn), lambda i, j: (0, j))],
        out_specs=pl.BlockSpec((tm, tn), lambda i, j: (i, j)),
        out_shape=jax.ShapeDtypeStruct((s, n), F32),
        scratch_shapes=[pltpu.VMEM((tm, d), BF16)],
        compiler_params=_params(("parallel", "arbitrary")),
        name="in_projection",
    )(x, g.reshape(1, d), w_bf16)


DIL_QBLK = 2048
DIL_SUB = 128
DIL_KWIN = 256
DIL_UNROLL = 4
DIL_EDGE_CASES = 3


def _dilated_kernel(q_ref, k_ref, v_ref, o_ref, m_ref, l_ref, acc_ref, bias_ref, *, seq):
    h = pl.program_id(0)
    base = pl.program_id(1) * DIL_QBLK
    slope = jnp.exp2(-(h + 1).astype(F32))
    scale = HEAD_DIM ** -0.5

    m_ref[...] = jnp.full(m_ref.shape, NEG_BIG, F32)
    l_ref[...] = jnp.zeros(l_ref.shape, F32)
    acc_ref[...] = jnp.zeros(acc_ref.shape, F32)

    row = lax.broadcasted_iota(jnp.int32, (DIL_SUB, DIL_KWIN), 0)
    col = lax.broadcasted_iota(jnp.int32, (DIL_SUB, DIL_KWIN), 1)
    col_minus_row = col - row

    for b_idx, (w, d) in enumerate(DIL_PATTERNS):
        half = (w // 2) // d
        assert half * 2 + DIL_SUB == DIL_KWIN
        seg_len = seq // d

        for v_idx in range(DIL_EDGE_CASES):
            rel = jnp.abs(col_minus_row - v_idx * half)
            bias_ref[b_idx * DIL_EDGE_CASES + v_idx] = jnp.where(
                rel <= half, -(slope * d) * rel.astype(F32), NEG_BIG)

        def rows(start, n, d=d):
            return pl.ds(start, n, stride=d) if d > 1 else pl.ds(start, n)

        def body(c, carry, d=d, half=half, seg_len=seg_len, rows=rows, b_idx=b_idx):
            r = c % d
            cc = c // d
            p0 = base // d + cc * DIL_SUB
            kp = jnp.clip(p0 - half, 0, seg_len - DIL_KWIN)
            lq = r + d * (cc * DIL_SUB)
            ks = r + d * kp

            q = (q_ref[rows(lq, DIL_SUB), :] * scale).astype(BF16)
            k = k_ref[rows(ks, DIL_KWIN), :].astype(BF16)
            v = v_ref[rows(ks, DIL_KWIN), :].astype(BF16)
            s = lax.dot_general(q, k, (((1,), (1,)), ((), ())), preferred_element_type=F32)
            s = s + bias_ref[b_idx * DIL_EDGE_CASES + (p0 - kp) // half]
            mb = jnp.max(s, axis=1, keepdims=True)
            p = jnp.exp(s - mb)
            lb = jnp.sum(p, axis=1, keepdims=True)
            ob = jnp.dot(p.astype(BF16), v, preferred_element_type=F32)

            idx = rows(lq, DIL_SUB)
            m_old = m_ref[idx, :]
            m_new = jnp.maximum(m_old, mb)
            a_old = jnp.exp(m_old - m_new)
            a_blk = jnp.exp(mb - m_new)
            m_ref[idx, :] = m_new
            l_ref[idx, :] = a_old * l_ref[idx, :] + a_blk * lb
            acc_ref[idx, :] = a_old * acc_ref[idx, :] + a_blk * ob
            return carry

        def group(cg, carry, body=body):
            for u in range(DIL_UNROLL):
                body(cg * DIL_UNROLL + u, carry)
            return carry

        lax.fori_loop(0, DIL_QBLK // DIL_SUB // DIL_UNROLL, group, 0)

    o_ref[...] = (acc_ref[...] / l_ref[...]).astype(o_ref.dtype)


def dilated_attention(proj, seq):
    assert seq % DIL_QBLK == 0
    for _, d in DIL_PATTERNS:
        assert DIL_QBLK // d >= DIL_SUB and seq // d >= DIL_KWIN
    nh = A_HEADS
    return pl.pallas_call(
        functools.partial(_dilated_kernel, seq=seq),
        grid=(nh, seq // DIL_QBLK),
        in_specs=[pl.BlockSpec((DIL_QBLK, HEAD_DIM), lambda h, i: (i, h)),
                  pl.BlockSpec((seq, HEAD_DIM), lambda h, i: (0, nh + h)),
                  pl.BlockSpec((seq, HEAD_DIM), lambda h, i: (0, 2 * nh + h))],
        out_specs=pl.BlockSpec((DIL_QBLK, HEAD_DIM), lambda h, i: (i, h)),
        out_shape=jax.ShapeDtypeStruct((seq, nh * HEAD_DIM), BF16),
        scratch_shapes=[pltpu.VMEM((DIL_QBLK, LANES), F32),
                        pltpu.VMEM((DIL_QBLK, LANES), F32),
                        pltpu.VMEM((DIL_QBLK, HEAD_DIM), F32),
                        pltpu.VMEM((len(DIL_PATTERNS) * DIL_EDGE_CASES, DIL_SUB, DIL_KWIN), F32)],
        compiler_params=_params(("parallel", "arbitrary")),
        name="dilated_attention",
    )(proj, proj, proj)


def _rope_tables(seq):
    rows = seq // GRID_W
    row = jnp.repeat(jnp.arange(rows, dtype=F32), GRID_W)
    col = jnp.tile(jnp.arange(GRID_W, dtype=F32), rows)
    half = HEAD_DIM // 2
    inv = ROPE_THETA ** (-jnp.arange(0, half, 2, dtype=F32) / half)
    ang = jnp.concatenate([row[:, None] * inv, col[:, None] * inv], axis=-1)
    cos = jnp.repeat(jnp.cos(ang), 2, axis=-1)
    sign = jnp.tile(jnp.asarray([-1.0, 1.0], F32), half)
    sin = jnp.repeat(jnp.sin(ang), 2, axis=-1) * sign
    return cos, sin


def _gqa_prep_kernel(p_ref, cos_ref, sin_ref, qg_ref, kg_ref, qt_ref, k_ref, vt_ref):
    cos = cos_ref[...]
    sin = sin_ref[...]
    even = (lax.broadcasted_iota(jnp.int32, cos.shape, 1) % 2) == 0

    def norm_rope(xh, g):
        y = _rms(xh, g)
        partner = jnp.where(even, pltpu.roll(y, LANES - 1, 1), pltpu.roll(y, 1, 1))
        return y * cos + partner * sin

    scale = HEAD_DIM ** -0.5 * LOG2_E
    for hq in range(B_Q_HEADS):
        xh = p_ref[:, hq * HEAD_DIM:(hq + 1) * HEAD_DIM]
        y = norm_rope(xh, qg_ref[...]) * scale
        qt_ref[hq * HEAD_DIM:(hq + 1) * HEAD_DIM, :] = y.T.astype(BF16)
    k0 = B_Q_HEADS * HEAD_DIM
    v0 = k0 + B_KV_HEADS * HEAD_DIM
    for hk in range(B_KV_HEADS):
        xh = p_ref[:, k0 + hk * HEAD_DIM:k0 + (hk + 1) * HEAD_DIM]
        k_ref[:, hk * HEAD_DIM:(hk + 1) * HEAD_DIM] = norm_rope(xh, kg_ref[...]).astype(BF16)
        vh = p_ref[:, v0 + hk * HEAD_DIM:v0 + (hk + 1) * HEAD_DIM]
        vt_ref[hk * HEAD_DIM:(hk + 1) * HEAD_DIM, :] = vh.T.astype(BF16)


def gqa_prep(proj, q_g, k_g, seq, col_block, ts=512):
    wb = (B_Q_HEADS + 2 * B_KV_HEADS) * HEAD_DIM
    cos, sin = _rope_tables(seq)
    return pl.pallas_call(
        _gqa_prep_kernel,
        grid=(seq // ts,),
        in_specs=[pl.BlockSpec((ts, wb), lambda i: (i, col_block)),
                  pl.BlockSpec((ts, HEAD_DIM), lambda i: (i, 0)),
                  pl.BlockSpec((ts, HEAD_DIM), lambda i: (i, 0)),
                  pl.BlockSpec((1, HEAD_DIM), lambda i: (0, 0)),
                  pl.BlockSpec((1, HEAD_DIM), lambda i: (0, 0))],
        out_specs=[pl.BlockSpec((B_Q_HEADS * HEAD_DIM, ts), lambda i: (0, i)),
                   pl.BlockSpec((ts, B_KV_HEADS * HEAD_DIM), lambda i: (i, 0)),
                   pl.BlockSpec((B_KV_HEADS * HEAD_DIM, ts), lambda i: (0, i))],
        out_shape=[jax.ShapeDtypeStruct((B_Q_HEADS * HEAD_DIM, seq), BF16),
                   jax.ShapeDtypeStruct((seq, B_KV_HEADS * HEAD_DIM), BF16),
                   jax.ShapeDtypeStruct((B_KV_HEADS * HEAD_DIM, seq), BF16)],
        compiler_params=_params(("parallel",)),
        name="gqa_prep",
    )(proj, cos, sin, q_g.reshape(1, HEAD_DIM), k_g.reshape(1, HEAD_DIM))


STAGE_ROWS = 32


def _data_dependent_zero(tiles):
    bits = None
    for t in tiles:
        b = pltpu.bitcast(t, jnp.uint32)
        bits = b if bits is None else bits | b
    zero_words = (bits >> 16) >> 16
    return zero_words.astype(jnp.int32).astype(F32).astype(BF16)


def _gqa_kernel(qt_ref, k_ref, vt_ref, o_ref, m_ref, l_ref, acc_ref, st_ref, ks_ref,
                *, seq, tk, nh):
    n_chunks = seq // tk
    tq = qt_ref.shape[1]
    m_ref[...] = jnp.full(m_ref.shape, NEG_BIG, F32)
    l_ref[...] = jnp.zeros(l_ref.shape, F32)
    acc_ref[...] = jnp.zeros(acc_ref.shape, F32)

    def q_head(hh):
        return qt_ref[hh * HEAD_DIM:(hh + 1) * HEAD_DIM, :]

    def chunk_start(c):
        return pl.multiple_of(jnp.minimum(c, n_chunks - 1) * tk, tk)

    st_ref[...] = jnp.dot(k_ref[pl.ds(0, tk), :], q_head(0), preferred_element_type=F32)

    def body(c, carry):
        here = chunk_start(c)
        vt = vt_ref[:, pl.ds(here, tk)]
        st = st_ref[...]
        for hh in range(nh):
            m_old = m_ref[hh]
            m_new = jnp.maximum(m_old, jnp.max(st, axis=0, keepdims=True))
            alpha = jnp.exp2(m_old - m_new)
            p = jnp.exp2(st - m_new)
            pb = p.astype(BF16)

            nxt = here if hh + 1 < nh else chunk_start(c + 1)
            slot = ks_ref.at[hh % 2]
            for r in range(0, tk, STAGE_ROWS):
                strip = [pb[r:r + STAGE_ROWS, j:j + LANES] for j in range(0, tq, LANES)]
                zero = _data_dependent_zero(strip)
                piece = k_ref[pl.ds(pl.multiple_of(nxt + r, STAGE_ROWS), STAGE_ROWS), :]
                slot[r:r + STAGE_ROWS, :] = piece + pltpu.repeat(zero, STAGE_ROWS // zero.shape[0], axis=0)
            st = jnp.dot(slot[...], q_head((hh + 1) % nh), preferred_element_type=F32)

            l_ref[hh] = alpha * l_ref[hh] + jnp.sum(p, axis=0, keepdims=True)
            acc_ref[hh] = alpha * acc_ref[hh] + jnp.dot(vt, pb, preferred_element_type=F32)
            m_ref[hh] = m_new
        st_ref[...] = st
        return carry

    lax.fori_loop(0, n_chunks, body, 0)
    for hh in range(nh):
        o_ref[:, hh * HEAD_DIM:(hh + 1) * HEAD_DIM] = (acc_ref[hh] / l_ref[hh]).T.astype(o_ref.dtype)


def gqa_attention(qt, k, vt, seq, tq=512, tk=512):
    nh = B_Q_HEADS // B_KV_HEADS
    return pl.pallas_call(
        functools.partial(_gqa_kernel, seq=seq, tk=tk, nh=nh),
        grid=(B_KV_HEADS, seq // tq),
        in_specs=[pl.BlockSpec((nh * HEAD_DIM, tq), lambda g, i: (g, i)),
                  pl.BlockSpec((seq, HEAD_DIM), lambda g, i: (0, g)),
                  pl.BlockSpec((HEAD_DIM, seq), lambda g, i: (g, 0))],
        out_specs=pl.BlockSpec((tq, nh * HEAD_DIM), lambda g, i: (i, g)),
        out_shape=jax.ShapeDtypeStruct((seq, B_Q_HEADS * HEAD_DIM), BF16),
        scratch_shapes=[pltpu.VMEM((nh, 1, tq), F32),
                        pltpu.VMEM((nh, 1, tq), F32),
                        pltpu.VMEM((nh, HEAD_DIM, tq), F32),
                        pltpu.VMEM((tk, tq), F32),
                        pltpu.VMEM((2, tk, HEAD_DIM), BF16)],
        compiler_params=_params(("parallel", "arbitrary")),
        name="gqa_attention",
    )(qt, k, vt)


def _outproj_kernel(ma_ref, mb_ref, wa_ref, wb_ref, x_ref, g_ref, x1_ref, ht_ref):
    y = x_ref[...]
    y = y + jnp.dot(ma_ref[...], wa_ref[...], preferred_element_type=F32)
    y = y + jnp.dot(mb_ref[...], wb_ref[...], preferred_element_type=F32)
    x1_ref[...] = y
    ht_ref[...] = _rms(y, g_ref[...]).T.astype(BF16)


def out_projection(mixed_a, mixed_b, w_a, w_b, x, g, tm=512):
    s, d = x.shape
    wa, wb = mixed_a.shape[1], mixed_b.shape[1]
    return pl.pallas_call(
        _outproj_kernel,
        grid=(s // tm,),
        in_specs=[pl.BlockSpec((tm, wa), lambda i: (i, 0)),
                  pl.BlockSpec((tm, wb), lambda i: (i, 0)),
                  pl.BlockSpec((wa, d), lambda i: (0, 0)),
                  pl.BlockSpec((wb, d), lambda i: (0, 0)),
                  pl.BlockSpec((tm, d), lambda i: (i, 0)),
                  pl.BlockSpec((1, d), lambda i: (0, 0))],
        out_specs=[pl.BlockSpec((tm, d), lambda i: (i, 0)),
                   pl.BlockSpec((d, tm), lambda i: (0, i))],
        out_shape=[jax.ShapeDtypeStruct((s, d), F32),
                   jax.ShapeDtypeStruct((d, s), BF16)],
        compiler_params=_params(("parallel",)),
        name="out_projection",
    )(mixed_a, mixed_b, w_a, w_b, x, g.reshape(1, d))


N_RANK = PEER_TOPK + 1
RANK_ROWS = 24
CAND_COUNTS = tuple(N_RANK // (a + 1) for a in range(N_RANK))
CAND_ROWS = -(-sum(CAND_COUNTS) // 8) * 8


def _extract_sorted(work, n, emit):
    for j in range(n):
        mj = jnp.max(work, axis=0, keepdims=True)
        emit(j, mj)
        if j + 1 < n:
            work = jnp.where(work == mj, -jnp.inf, work)


def _odd_even_merge_sort_pairs(n):
    pairs = []

    def merge(lo, hi, r):
        step = r * 2
        if step < hi - lo:
            merge(lo, hi, step)
            merge(lo + r, hi, step)
            pairs.extend((i, i + r) for i in range(lo + r, hi - r, step))
        else:
            pairs.append((lo, lo + r))

    def sort(lo, hi):
        if hi - lo >= 1:
            mid = lo + (hi - lo) // 2
            sort(lo, mid)
            sort(mid + 1, hi)
            merge(lo, hi, 1)

    sort(0, n - 1)
    return pairs


def _extract_sorted_tile(tile, n, emit):
    depth = tile.shape[0] // SUBLANES
    lists = [tile[SUBLANES * j:SUBLANES * (j + 1)] for j in range(depth)]
    for i, j in _odd_even_merge_sort_pairs(depth):
        lists[i], lists[j] = jnp.maximum(lists[i], lists[j]), jnp.minimum(lists[i], lists[j])
    for r in range(n):
        mj = jnp.max(lists[0], axis=0, keepdims=True)
        emit(r, mj)
        took = lists[0] == mj
        for j in range(min(depth, n - (r + 1))):
            below = lists[j + 1] if j + 1 < depth else -jnp.inf
            lists[j] = jnp.where(took, below, lists[j])


def _router_kernel(ht_ref, wqt_ref, keys_ref, thr_ref, w0_ref, s1_ref, e1_ref,
                   q_scr, s_scr, top_scr, cand_scr, *, tb):
    q_scr[...] = jnp.dot(wqt_ref[...], ht_ref[...], preferred_element_type=F32).astype(BF16)
    lane_groups = [slice(lg * LANES, (lg + 1) * LANES) for lg in range(tb // LANES)]

    def half_body(hc, carry):
        qhc = q_scr[pl.ds(pl.multiple_of(hc * PEER_N_KEYS, PEER_N_KEYS), PEER_N_KEYS), :]
        s = jnp.dot(keys_ref[hc], qhc, preferred_element_type=F32)
        s_scr[hc] = s
        for ls in lane_groups:
            def emit(j, mj, ls=ls):
                top_scr[hc, pl.ds(j, 1), ls] = mj

            _extract_sorted_tile(s[:, ls], N_RANK, emit)
        return carry

    lax.fori_loop(0, 2 * PEER_HEADS, half_body, 0)

    def head_body(h, carry):
        t0 = top_scr[2 * h]
        t1 = top_scr[2 * h + 1]
        cand_scr[...] = jnp.full(cand_scr.shape, -jnp.inf, F32)
        off = 0
        for a, nb in enumerate(CAND_COUNTS):
            cand_scr[off:off + nb, :] = t0[a:a + 1, :] + t1[0:nb, :]
            off += nb

        best = []
        _extract_sorted(cand_scr[...], N_RANK, lambda j, mj: best.append(mj))
        theta = 0.5 * (best[PEER_TOPK - 1] + best[PEER_TOPK])
        z = jnp.zeros_like(theta)
        for j in range(PEER_TOPK):
            z = z + jnp.exp(best[j] - best[0])

        s0 = s_scr[2 * h]
        s1 = s_scr[2 * h + 1]
        thr_ref[h] = theta - s0
        w0_ref[h] = jnp.exp(s0 - t0[0:1, :]) * (0.5 / z)
        e1 = jnp.exp(s1 - t1[0:1, :])
        for lg, ls in enumerate(lane_groups):
            s1_ref[h, lg] = s1[:, ls]
            e1_ref[h, lg] = e1[:, ls]
        return carry

    lax.fori_loop(0, PEER_HEADS, head_body, 0)


def peer_router(ht, wqt, keys, tb=512):
    d, s = ht.shape
    nq = wqt.shape[0]
    tab = jax.ShapeDtypeStruct((PEER_HEADS, PEER_N_KEYS, s), F32)
    tab_spec = pl.BlockSpec((PEER_HEADS, PEER_N_KEYS, tb), lambda i: (0, 0, i))
    tile = jax.ShapeDtypeStruct((PEER_HEADS, s // LANES, PEER_N_KEYS, LANES), F32)
    tile_spec = pl.BlockSpec((PEER_HEADS, tb // LANES, PEER_N_KEYS, LANES), lambda i: (0, i, 0, 0))
    return pl.pallas_call(
        functools.partial(_router_kernel, tb=tb),
        grid=(s // tb,),
        in_specs=[pl.BlockSpec((d, tb), lambda i: (0, i)),
                  pl.BlockSpec((nq, d), lambda i: (0, 0)),
                  pl.BlockSpec(keys.shape, lambda i: (0, 0, 0))],
        out_specs=[tab_spec, tab_spec, tile_spec, tile_spec],
        out_shape=[tab, tab, tile, tile],
        scratch_shapes=[pltpu.VMEM((nq, tb), BF16),
                        pltpu.VMEM((2 * PEER_HEADS, PEER_N_KEYS, tb), F32),
                        pltpu.VMEM((2 * PEER_HEADS, RANK_ROWS, tb), F32),
                        pltpu.VMEM((CAND_ROWS, tb), F32)],
        compiler_params=_params(("parallel",)),
        name="peer_router",
    )(ht, wqt, keys)


GELU_C = 0.7978845608028654


def _gelu_tanh_x2(x):
    u = x * (GELU_C + (GELU_C * 0.044715) * (x * x))
    return x + x * jnp.tanh(u)


PEER_SUB = 256
GATE_ROWS = 32


def _peer_stage(ht_ref, u_ref, v_ref, thr_ref, w0_ref, s1_ref, e1_ref, o_ref,
                at_w, at_r, pt_w, pt_r, *, te, tb):
    d = o_ref.shape[1]
    n_sub = te // PEER_SUB
    col_w = d // n_sub

    def hidden_piece(j):
        rows = slice(j * PEER_SUB, (j + 1) * PEER_SUB)
        at = jnp.dot(u_ref[rows, :], ht_ref[...], preferred_element_type=F32)
        for lg in range(tb // LANES):
            at_w[lg, rows, :] = at[:, lg * LANES:(lg + 1) * LANES]

    def value_piece(c):
        cols = slice(c * col_w, (c + 1) * col_w)
        o_ref[:, cols] += lax.dot_general(pt_r[...], v_ref[:, cols], (((0,), (0,)), ((), ())),
                                          preferred_element_type=F32)

    n_i0 = te // PEER_N_KEYS
    n_q = PEER_N_KEYS // GATE_ROWS

    def gate_piece(lg, q):
        ls = slice(lg * LANES, (lg + 1) * LANES)
        qs = slice(q * GATE_ROWS, (q + 1) * GATE_ROWS)
        gates = [None] * n_i0
        for h in range(PEER_HEADS):
            s1 = s1_ref[h, lg, qs, :]
            e1 = e1_ref[h, lg, qs, :]
            for i0 in range(n_i0):
                thr = thr_ref[h, i0:i0 + 1, ls]
                w0 = w0_ref[h, i0:i0 + 1, ls]
                term = jnp.where(s1 >= thr, e1 * w0, 0.0)
                gates[i0] = term if gates[i0] is None else gates[i0] + term
        for i0 in range(n_i0):
            er = slice(i0 * PEER_N_KEYS + q * GATE_ROWS, i0 * PEER_N_KEYS + (q + 1) * GATE_ROWS)
            pt_w[er, ls] = (_gelu_tanh_x2(at_r[lg, er, :]) * gates[i0]).astype(BF16)

    mxu_pieces = [functools.partial(hidden_piece, j) for j in range(n_sub)]
    mxu_pieces += [functools.partial(value_piece, c) for c in range(n_sub)]
    vec_pieces = [(lg, q) for lg in range(tb // LANES) for q in range(n_q)]
    assert len(vec_pieces) % len(mxu_pieces) == 0
    per_mxu = len(vec_pieces) // len(mxu_pieces)
    for n, piece in enumerate(mxu_pieces):
        for lg, q in vec_pieces[n * per_mxu:(n + 1) * per_mxu]:
            gate_piece(lg, q)
        piece()


def _peer_kernel(ht_ref, u_ref, v_ref, thr_ref, w0_ref, s1_ref, e1_ref, x1_ref, g_ref,
                 o_ref, at0_ref, at1_ref, pt0_ref, pt1_ref, *, te, tb, n_e):
    g = pl.program_id(0)
    e_out = (g - 2) % n_e
    out_live = g >= 2
    at_refs = (at0_ref, at1_ref)
    pt_refs = (pt0_ref, pt1_ref)

    @pl.when(g == 0)
    def _():
        for r in at_refs + pt_refs:
            r[...] = jnp.zeros(r.shape, r.dtype)

    @pl.when((g == 0) | (out_live & (e_out == 0)))
    def _():
        o_ref[...] = x1_ref[...]

    for parity in range(2):
        @pl.when(g % 2 == parity)
        def _(parity=parity):
            _peer_stage(ht_ref, u_ref, v_ref, thr_ref, w0_ref, s1_ref, e1_ref, o_ref,
                        at_refs[parity], at_refs[1 - parity],
                        pt_refs[1 - parity], pt_refs[parity], te=te, tb=tb)

    @pl.when(out_live & (e_out == n_e - 1))
    def _():
        o_ref[...] = _rms(o_ref[...], g_ref[...])


def peer_experts(ht, u, v, tables, x1, g, tb=512, te=1024):
    d, s = ht.shape
    n_exp = u.shape[0]
    n_i0 = te // PEER_N_KEYS
    n_e = n_exp // te
    n_items = (s // tb) * n_e
    thr, w0, s1, e1 = tables

    def item(gi, lag):
        n = jnp.clip(gi - lag, 0, n_items - 1)
        return n // n_e, n % n_e

    row_spec = pl.BlockSpec((PEER_HEADS, n_i0, tb), lambda gi: (0, item(gi, 1)[1], item(gi, 1)[0]))
    tab_spec = pl.BlockSpec((PEER_HEADS, tb // LANES, PEER_N_KEYS, LANES),
                            lambda gi: (0, item(gi, 1)[0], 0, 0))
    return pl.pallas_call(
        functools.partial(_peer_kernel, te=te, tb=tb, n_e=n_e),
        grid=(n_items + 2,),
        in_specs=[pl.BlockSpec((d, tb), lambda gi: (0, item(gi, 0)[0])),
                  pl.BlockSpec((te, d), lambda gi: (item(gi, 0)[1], 0)),
                  pl.BlockSpec((te, d), lambda gi: (item(gi, 2)[1], 0)),
                  row_spec, row_spec, tab_spec, tab_spec,
                  pl.BlockSpec((tb, d), lambda gi: (item(gi, 2)[0], 0),
                               pipeline_mode=pl.Buffered(1)),
                  pl.BlockSpec((1, d), lambda gi: (0, 0))],
        out_specs=pl.BlockSpec((tb, d), lambda gi: (item(gi, 2)[0], 0)),
        out_shape=jax.ShapeDtypeStruct((s, d), F32),
        scratch_shapes=[pltpu.VMEM((tb // LANES, te, LANES), F32),
                        pltpu.VMEM((tb // LANES, te, LANES), F32),
                        pltpu.VMEM((te, tb), BF16),
                        pltpu.VMEM((te, tb), BF16)],
        compiler_params=_params(("arbitrary",)),
        name="peer_experts",
    )(ht, u, v, thr, w0, s1, e1, x1, g.reshape(1, d))


def kernel(x, norm1_g, w_in, q_norm_g, k_norm_g, w_out, norm2_g, peer_w_query,
           peer_sub_keys, peer_u, peer_v, final_norm_g):
    b, s, d = x.shape
    assert b == 1
    a_width = A_HEADS * HEAD_DIM
    depth = w_in.shape[0]
    xs = x.reshape(s, d)
    for l in range(depth):
        proj = in_projection(xs, norm1_g[l], w_in[l].astype(BF16))
        mixed_a = dilated_attention(proj, s)
        qt, kb, vt = gqa_prep(proj, q_norm_g[l], k_norm_g[l], s, col_block=2)
        mixed_b = gqa_attention(qt, kb, vt, s)
        w_o = w_out[l].astype(BF16)
        x1, ht = out_projection(mixed_a, mixed_b, w_o[:a_width], w_o[a_width:], xs, norm2_g[l])
        keys = peer_sub_keys[l].reshape(2 * PEER_HEADS, PEER_N_KEYS, -1).astype(BF16)
        tables = peer_router(ht, peer_w_query[l].T.astype(BF16), keys)
        assert l == depth - 1
        xs = peer_experts(ht, peer_u[l].astype(BF16), peer_v[l].astype(BF16), tables, x1,
                          final_norm_g)
    return xs.reshape(b, s, d)
```

```python
import functools

import numpy as np
import jax
import jax.numpy as jnp
from jax import lax
from jax.experimental import pallas as pl
from jax.experimental.pallas import tpu as pltpu

F32 = jnp.float32
BF16 = jnp.bfloat16

HEAD_DIM = 128
A_HEADS = 8
DIL_PATTERNS = ((128, 1), (512, 4), (2048, 16))
B_Q_HEADS = 8
B_KV_HEADS = 2
GRID_W = 64
ROPE_THETA = 10000.0
NORM_EPS = 1e-6
PEER_HEADS = 8
PEER_N_KEYS = 128
PEER_TOPK = 16

LANES = 128
SUBLANES = 8
NEG_BIG = -1e30
LOG2_E = 1.4426950408889634
VMEM_LIMIT = 56 * 1024 * 1024


def _params(sem, vmem=VMEM_LIMIT):
    return pltpu.CompilerParams(dimension_semantics=sem, vmem_limit_bytes=vmem)


def _rms(x, g):
    ms = jnp.mean(x * x, axis=-1, keepdims=True)
    return x * lax.rsqrt(ms + NORM_EPS) * g


def _inproj_kernel(x_ref, g_ref, w_ref, o_ref, h_ref):
    @pl.when(pl.program_id(1) == 0)
    def _():
        h_ref[...] = _rms(x_ref[...], g_ref[...]).astype(BF16)

    o_ref[...] = jnp.dot(h_ref[...], w_ref[...], preferred_element_type=F32)


def in_projection(x, g, w_bf16, tm=1024, tn=512):
    s, d = x.shape
    n = w_bf16.shape[1]
    return pl.pallas_call(
        _inproj_kernel,
        grid=(s // tm, n // tn),
        in_specs=[pl.BlockSpec((tm, d), lambda i, j: (i, 0)),
                  pl.BlockSpec((1, d), lambda i, j: (0, 0)),
                  pl.BlockSpec((d, tn), lambda i, j: (0, j))],
        out_specs=pl.BlockSpec((tm, tn), lambda i, j: (i, j)),
        out_shape=jax.ShapeDtypeStruct((s, n), F32),
        scratch_shapes=[pltpu.VMEM((tm, d), BF16)],
        compiler_params=_params(("parallel", "arbitrary")),
        name="in_projection",
    )(x, g.reshape(1, d), w_bf16)


DIL_QBLK = 2048
DIL_SUB = 128
DIL_KWIN = 256
DIL_UNROLL = 16
DIL_EDGE_CASES = 3


def _dilated_kernel(q_ref, k_ref, v_ref, o_ref, m_ref, l_ref, acc_ref, bias_ref, *, seq):
    h = pl.program_id(0)
    base = pl.program_id(1) * DIL_QBLK
    slope = jnp.exp2(-(h + 1).astype(F32))
    scale = HEAD_DIM ** -0.5

    m_ref[...] = jnp.full(m_ref.shape, NEG_BIG, F32)
    l_ref[...] = jnp.zeros(l_ref.shape, F32)
    acc_ref[...] = jnp.zeros(acc_ref.shape, F32)

    row = lax.broadcasted_iota(jnp.int32, (DIL_SUB, DIL_KWIN), 0)
    col = lax.broadcasted_iota(jnp.int32, (DIL_SUB, DIL_KWIN), 1)
    col_minus_row = col - row

    for b_idx, (w, d) in enumerate(DIL_PATTERNS):
        half = (w // 2) // d
        assert half * 2 + DIL_SUB == DIL_KWIN
        seg_len = seq // d

        for v_idx in range(DIL_EDGE_CASES):
            rel = jnp.abs(col_minus_row - v_idx * half)
            bias_ref[b_idx * DIL_EDGE_CASES + v_idx] = jnp.where(
                rel <= half, -(slope * d) * rel.astype(F32), NEG_BIG)

        def rows(start, n, d=d):
            return pl.ds(start, n, stride=d) if d > 1 else pl.ds(start, n)

        def body(c, carry, d=d, half=half, seg_len=seg_len, rows=rows, b_idx=b_idx):
            r = c % d
            cc = c // d
            p0 = base // d + cc * DIL_SUB
            kp = jnp.clip(p0 - half, 0, seg_len - DIL_KWIN)
            lq = r + d * (cc * DIL_SUB)
            ks = r + d * kp

            q = (q_ref[rows(lq, DIL_SUB), :] * scale).astype(BF16)
            k = k_ref[rows(ks, DIL_KWIN), :].astype(BF16)
            v = v_ref[rows(ks, DIL_KWIN), :].astype(BF16)
            s = lax.dot_general(q, k, (((1,), (1,)), ((), ())), preferred_element_type=F32)
            s = s + bias_ref[b_idx * DIL_EDGE_CASES + (p0 - kp) // half]
            mb = jnp.max(s, axis=1, keepdims=True)
            p = jnp.exp(s - mb)
            lb = jnp.sum(p, axis=1, keepdims=True)
            ob = jnp.dot(p.astype(BF16), v, preferred_element_type=F32)

            idx = rows(lq, DIL_SUB)
            m_old = m_ref[idx, :]
            m_new = jnp.maximum(m_old, mb)
            a_old = jnp.exp(m_old - m_new)
            a_blk = jnp.exp(mb - m_new)
            m_ref[idx, :] = m_new
            l_ref[idx, :] = a_old * l_ref[idx, :] + a_blk * lb
            acc_ref[idx, :] = a_old * acc_ref[idx, :] + a_blk * ob
            return carry

        def group(cg, carry, body=body):
            for u in range(DIL_UNROLL):
                body(cg * DIL_UNROLL + u, carry)
            return carry

        lax.fori_loop(0, DIL_QBLK // DIL_SUB // DIL_UNROLL, group, 0)

    o_ref[...] = (acc_ref[...] / l_ref[...]).astype(o_ref.dtype)


def dilated_attention(proj, seq):
    assert seq % DIL_QBLK == 0
    for _, d in DIL_PATTERNS:
        assert DIL_QBLK // d >= DIL_SUB and seq // d >= DIL_KWIN
    nh = A_HEADS
    return pl.pallas_call(
        functools.partial(_dilated_kernel, seq=seq),
        grid=(nh, seq // DIL_QBLK),
        in_specs=[pl.BlockSpec((DIL_QBLK, HEAD_DIM), lambda h, i: (i, h)),
                  pl.BlockSpec((seq, HEAD_DIM), lambda h, i: (0, nh + h)),
                  pl.BlockSpec((seq, HEAD_DIM), lambda h, i: (0, 2 * nh + h))],
        out_specs=pl.BlockSpec((DIL_QBLK, HEAD_DIM), lambda h, i: (i, h)),
        out_shape=jax.ShapeDtypeStruct((seq, nh * HEAD_DIM), BF16),
        scratch_shapes=[pltpu.VMEM((DIL_QBLK, LANES), F32),
                        pltpu.VMEM((DIL_QBLK, LANES), F32),
                        pltpu.VMEM((DIL_QBLK, HEAD_DIM), F32),
                        pltpu.VMEM((len(DIL_PATTERNS) * DIL_EDGE_CASES, DIL_SUB, DIL_KWIN), F32)],
        compiler_params=_params(("parallel", "arbitrary")),
        name="dilated_attention",
    )(proj, proj, proj)


def _rope_tables(seq):
    rows = seq // GRID_W
    row = jnp.repeat(jnp.arange(rows, dtype=F32), GRID_W)
    col = jnp.tile(jnp.arange(GRID_W, dtype=F32), rows)
    half = HEAD_DIM // 2
    inv = ROPE_THETA ** (-jnp.arange(0, half, 2, dtype=F32) / half)
    ang = jnp.concatenate([row[:, None] * inv, col[:, None] * inv], axis=-1)
    cos = jnp.repeat(jnp.cos(ang), 2, axis=-1)
    sign = jnp.tile(jnp.asarray([-1.0, 1.0], F32), half)
    sin = jnp.repeat(jnp.sin(ang), 2, axis=-1) * sign
    return cos, sin


def _gqa_prep_kernel(p_ref, cos_ref, sin_ref, qg_ref, kg_ref, qt_ref, k_ref, vt_ref):
    cos = cos_ref[...]
    sin = sin_ref[...]
    even = (lax.broadcasted_iota(jnp.int32, cos.shape, 1) % 2) == 0

    def norm_rope(xh, g):
        y = _rms(xh, g)
        partner = jnp.where(even, pltpu.roll(y, LANES - 1, 1), pltpu.roll(y, 1, 1))
        return y * cos + partner * sin

    scale = HEAD_DIM ** -0.5 * LOG2_E
    for hq in range(B_Q_HEADS):
        xh = p_ref[:, hq * HEAD_DIM:(hq + 1) * HEAD_DIM]
        y = norm_rope(xh, qg_ref[...]) * scale
        qt_ref[hq * HEAD_DIM:(hq + 1) * HEAD_DIM, :] = y.T.astype(BF16)
    k0 = B_Q_HEADS * HEAD_DIM
    v0 = k0 + B_KV_HEADS * HEAD_DIM
    for hk in range(B_KV_HEADS):
        xh = p_ref[:, k0 + hk * HEAD_DIM:k0 + (hk + 1) * HEAD_DIM]
        k_ref[:, hk * HEAD_DIM:(hk + 1) * HEAD_DIM] = norm_rope(xh, kg_ref[...]).astype(BF16)
        vh = p_ref[:, v0 + hk * HEAD_DIM:v0 + (hk + 1) * HEAD_DIM]
        vt_ref[hk * HEAD_DIM:(hk + 1) * HEAD_DIM, :] = vh.T.astype(BF16)


def gqa_prep(proj, q_g, k_g, seq, col_block, ts=512):
    wb = (B_Q_HEADS + 2 * B_KV_HEADS) * HEAD_DIM
    cos, sin = _rope_tables(seq)
    return pl.pallas_call(
        _gqa_prep_kernel,
        grid=(seq // ts,),
        in_specs=[pl.BlockSpec((ts, wb), lambda i: (i, col_block)),
                  pl.BlockSpec((ts, HEAD_DIM), lambda i: (i, 0)),
                  pl.BlockSpec((ts, HEAD_DIM), lambda i: (i, 0)),
                  pl.BlockSpec((1, HEAD_DIM), lambda i: (0, 0)),
                  pl.BlockSpec((1, HEAD_DIM), lambda i: (0, 0))],
        out_specs=[pl.BlockSpec((B_Q_HEADS * HEAD_DIM, ts), lambda i: (0, i)),
                   pl.BlockSpec((ts, B_KV_HEADS * HEAD_DIM), lambda i: (i, 0)),
                   pl.BlockSpec((B_KV_HEADS * HEAD_DIM, ts), lambda i: (0, i))],
        out_shape=[jax.ShapeDtypeStruct((B_Q_HEADS * HEAD_DIM, seq), BF16),
                   jax.ShapeDtypeStruct((seq, B_KV_HEADS * HEAD_DIM), BF16),
                   jax.ShapeDtypeStruct((B_KV_HEADS * HEAD_DIM, seq), BF16)],
        compiler_params=_params(("parallel",)),
        name="gqa_prep",
    )(proj, cos, sin, q_g.reshape(1, HEAD_DIM), k_g.reshape(1, HEAD_DIM))


STAGE_ROWS = 32


def _data_dependent_zero(tiles):
    bits = None
    for t in tiles:
        b = pltpu.bitcast(t, jnp.uint32)
        bits = b if bits is None else bits | b
    zero_words = (bits >> 16) >> 16
    return zero_words.astype(jnp.int32).astype(F32).astype(BF16)


def _gqa_kernel(qt_ref, k_ref, vt_ref, o_ref, m_ref, l_ref, acc_ref, st_ref, ks_ref,
                *, seq, tk, nh):
    n_chunks = seq // tk
    tq = qt_ref.shape[1]
    m_ref[...] = jnp.full(m_ref.shape, NEG_BIG, F32)
    l_ref[...] = jnp.zeros(l_ref.shape, F32)
    acc_ref[...] = jnp.zeros(acc_ref.shape, F32)

    def q_head(hh):
        return qt_ref[hh * HEAD_DIM:(hh + 1) * HEAD_DIM, :]

    def chunk_start(c):
        return pl.multiple_of(jnp.minimum(c, n_chunks - 1) * tk, tk)

    st_ref[...] = jnp.dot(k_ref[pl.ds(0, tk), :], q_head(0), preferred_element_type=F32)

    def body(c, carry):
        here = chunk_start(c)
        vt = vt_ref[:, pl.ds(here, tk)]
        st = st_ref[...]
        for hh in range(nh):
            m_old = m_ref[hh]
            m_new = jnp.maximum(m_old, jnp.max(st, axis=0, keepdims=True))
            alpha = jnp.exp2(m_old - m_new)
            p = jnp.exp2(st - m_new)
            pb = p.astype(BF16)

            nxt = here if hh + 1 < nh else chunk_start(c + 1)
            slot = ks_ref.at[hh % 2]
            for r in range(0, tk, STAGE_ROWS):
                strip = [pb[r:r + STAGE_ROWS, j:j + LANES] for j in range(0, tq, LANES)]
                zero = _data_dependent_zero(strip)
                piece = k_ref[pl.ds(pl.multiple_of(nxt + r, STAGE_ROWS), STAGE_ROWS), :]
                slot[r:r + STAGE_ROWS, :] = piece + pltpu.repeat(zero, STAGE_ROWS // zero.shape[0], axis=0)
            st = jnp.dot(slot[...], q_head((hh + 1) % nh), preferred_element_type=F32)

            l_ref[hh] = alpha * l_ref[hh] + jnp.sum(p, axis=0, keepdims=True)
            acc_ref[hh] = alpha * acc_ref[hh] + jnp.dot(vt, pb, preferred_element_type=F32)
            m_ref[hh] = m_new
        st_ref[...] = st
        return carry

    lax.fori_loop(0, n_chunks, body, 0)
    for hh in range(nh):
        o_ref[:, hh * HEAD_DIM:(hh + 1) * HEAD_DIM] = (acc_ref[hh] / l_ref[hh]).T.astype(o_ref.dtype)


def gqa_attention(qt, k, vt, seq, tq=512, tk=512):
    nh = B_Q_HEADS // B_KV_HEADS
    return pl.pallas_call(
        functools.partial(_gqa_kernel, seq=seq, tk=tk, nh=nh),
        grid=(B_KV_HEADS, seq // tq),
        in_specs=[pl.BlockSpec((nh * HEAD_DIM, tq), lambda g, i: (g, i)),
                  pl.BlockSpec((seq, HEAD_DIM), lambda g, i: (0, g)),
                  pl.BlockSpec((HEAD_DIM, seq), lambda g, i: (g, 0))],
        out_specs=pl.BlockSpec((tq, nh * HEAD_DIM), lambda g, i: (i, g)),
        out_shape=jax.ShapeDtypeStruct((seq, B_Q_HEADS * HEAD_DIM), BF16),
        scratch_shapes=[pltpu.VMEM((nh, 1, tq), F32),
                        pltpu.VMEM((nh, 1, tq), F32),
                        pltpu.VMEM((nh, HEAD_DIM, tq), F32),
                        pltpu.VMEM((tk, tq), F32),
                        pltpu.VMEM((2, tk, HEAD_DIM), BF16)],
        compiler_params=_params(("parallel", "arbitrary")),
        name="gqa_attention",
    )(qt, k, vt)


def _outproj_kernel(ma_ref, mb_ref, wa_ref, wb_ref, x_ref, g_ref, x1_ref, ht_ref):
    y = x_ref[...]
    y = y + jnp.dot(ma_ref[...], wa_ref[...], preferred_element_type=F32)
    y = y + jnp.dot(mb_ref[...], wb_ref[...], preferred_element_type=F32)
    x1_ref[...] = y
    ht_ref[...] = _rms(y, g_ref[...]).T.astype(BF16)


def out_projection(mixed_a, mixed_b, w_a, w_b, x, g, tm=512):
    s, d = x.shape
    wa, wb = mixed_a.shape[1], mixed_b.shape[1]
    return pl.pallas_call(
        _outproj_kernel,
        grid=(s // tm,),
        in_specs=[pl.BlockSpec((tm, wa), lambda i: (i, 0)),
                  pl.BlockSpec((tm, wb), lambda i: (i, 0)),
                  pl.BlockSpec((wa, d), lambda i: (0, 0)),
                  pl.BlockSpec((wb, d), lambda i: (0, 0)),
                  pl.BlockSpec((tm, d), lambda i: (i, 0)),
                  pl.BlockSpec((1, d), lambda i: (0, 0))],
        out_specs=[pl.BlockSpec((tm, d), lambda i: (i, 0)),
                   pl.BlockSpec((d, tm), lambda i: (0, i))],
        out_shape=[jax.ShapeDtypeStruct((s, d), F32),
                   jax.ShapeDtypeStruct((d, s), BF16)],
        compiler_params=_params(("parallel",)),
        name="out_projection",
    )(mixed_a, mixed_b, w_a, w_b, x, g.reshape(1, d))


N_RANK = PEER_TOPK + 1
RANK_ROWS = 24
CAND_COUNTS = tuple(N_RANK // (a + 1) for a in range(N_RANK))
CAND_ROWS = -(-sum(CAND_COUNTS) // 8) * 8


def _extract_sorted(work, n, emit):
    for j in range(n):
        mj = jnp.max(work, axis=0, keepdims=True)
        emit(j, mj)
        if j + 1 < n:
            work = jnp.where(work == mj, -jnp.inf, work)


def _odd_even_merge_sort_pairs(n):
    pairs = []

    def merge(lo, hi, r):
        step = r * 2
        if step < hi - lo:
            merge(lo, hi, step)
            merge(lo + r, hi, step)
            pairs.extend((i, i + r) for i in range(lo + r, hi - r, step))
        else:
            pairs.append((lo, lo + r))

    def sort(lo, hi):
        if hi - lo >= 1:
            mid = lo + (hi - lo) // 2
            sort(lo, mid)
            sort(mid + 1, hi)
            merge(lo, hi, 1)

    sort(0, n - 1)
    return pairs


def _extract_sorted_tile(tile, n, emit):
    depth = tile.shape[0] // SUBLANES
    lists = [tile[SUBLANES * j:SUBLANES * (j + 1)] for j in range(depth)]
    for i, j in _odd_even_merge_sort_pairs(depth):
        lists[i], lists[j] = jnp.maximum(lists[i], lists[j]), jnp.minimum(lists[i], lists[j])
    for r in range(n):
        mj = jnp.max(lists[0], axis=0, keepdims=True)
        emit(r, mj)
        took = lists[0] == mj
        for j in range(min(depth, n - (r + 1))):
            below = lists[j + 1] if j + 1 < depth else -jnp.inf
            lists[j] = jnp.where(took, below, lists[j])


def _router_kernel(ht_ref, wqt_ref, keys_ref, thr_ref, w0_ref, s1_ref, e1_ref,
                   q_scr, s_scr, top_scr, cand_scr, *, tb):
    q_scr[...] = jnp.dot(wqt_ref[...], ht_ref[...], preferred_element_type=F32).astype(BF16)
    lane_groups = [slice(lg * LANES, (lg + 1) * LANES) for lg in range(tb // LANES)]

    def half_body(hc, carry):
        qhc = q_scr[pl.ds(pl.multiple_of(hc * PEER_N_KEYS, PEER_N_KEYS), PEER_N_KEYS), :]
        s = jnp.dot(keys_ref[hc], qhc, preferred_element_type=F32)
        s_scr[hc] = s
        for ls in lane_groups:
            def emit(j, mj, ls=ls):
                top_scr[hc, pl.ds(j, 1), ls] = mj

            _extract_sorted_tile(s[:, ls], N_RANK, emit)
        return carry

    lax.fori_loop(0, 2 * PEER_HEADS, half_body, 0)

    def head_body(h, carry):
        t0 = top_scr[2 * h]
        t1 = top_scr[2 * h + 1]
        cand_scr[...] = jnp.full(cand_scr.shape, -jnp.inf, F32)
        off = 0
        for a, nb in enumerate(CAND_COUNTS):
            cand_scr[off:off + nb, :] = t0[a:a + 1, :] + t1[0:nb, :]
            off += nb

        best = []
        _extract_sorted(cand_scr[...], N_RANK, lambda j, mj: best.append(mj))
        theta = 0.5 * (best[PEER_TOPK - 1] + best[PEER_TOPK])
        z = jnp.zeros_like(theta)
        for j in range(PEER_TOPK):
            z = z + jnp.exp(best[j] - best[0])

        s0 = s_scr[2 * h]
        s1 = s_scr[2 * h + 1]
        thr_ref[h] = theta - s0
        w0_ref[h] = jnp.exp(s0 - t0[0:1, :]) * (0.5 / z)
        e1 = jnp.exp(s1 - t1[0:1, :])
        for lg, ls in enumerate(lane_groups):
            s1_ref[h, lg] = s1[:, ls]
            e1_ref[h, lg] = e1[:, ls]
        return carry

    lax.fori_loop(0, PEER_HEADS, head_body, 0)


def peer_router(ht, wqt, keys, tb=512):
    d, s = ht.shape
    nq = wqt.shape[0]
    tab = jax.ShapeDtypeStruct((PEER_HEADS, PEER_N_KEYS, s), F32)
    tab_spec = pl.BlockSpec((PEER_HEADS, PEER_N_KEYS, tb), lambda i: (0, 0, i))
    tile = jax.ShapeDtypeStruct((PEER_HEADS, s // LANES, PEER_N_KEYS, LANES), F32)
    tile_spec = pl.BlockSpec((PEER_HEADS, tb // LANES, PEER_N_KEYS, LANES), lambda i: (0, i, 0, 0))
    return pl.pallas_call(
        functools.partial(_router_kernel, tb=tb),
        grid=(s // tb,),
        in_specs=[pl.BlockSpec((d, tb), lambda i: (0, i)),
                  pl.BlockSpec((nq, d), lambda i: (0, 0)),
                  pl.BlockSpec(keys.shape, lambda i: (0, 0, 0))],
        out_specs=[tab_spec, tab_spec, tile_spec, tile_spec],
        out_shape=[tab, tab, tile, tile],
        scratch_shapes=[pltpu.VMEM((nq, tb), BF16),
                        pltpu.VMEM((2 * PEER_HEADS, PEER_N_KEYS, tb), F32),
                        pltpu.VMEM((2 * PEER_HEADS, RANK_ROWS, tb), F32),
                        pltpu.VMEM((CAND_ROWS, tb), F32)],
        compiler_params=_params(("parallel",)),
        name="peer_router",
    )(ht, wqt, keys)


GELU_C = 0.7978845608028654


def _gelu_tanh_x2(x):
    u = x * (GELU_C + (GELU_C * 0.044715) * (x * x))
    return x + x * jnp.tanh(u)


PEER_SUB = 256
GATE_ROWS = 32


def _peer_stage(ht_ref, u_ref, v_ref, thr_ref, w0_ref, s1_ref, e1_ref, o_ref,
                at_w, at_r, pt_w, pt_r, *, te, tb):
    d = o_ref.shape[1]
    n_sub = te // PEER_SUB
    col_w = d // n_sub

    def hidden_piece(j):
        rows = slice(j * PEER_SUB, (j + 1) * PEER_SUB)
        at = jnp.dot(u_ref[rows, :], ht_ref[...], preferred_element_type=F32)
        for lg in range(tb // LANES):
            at_w[lg, rows, :] = at[:, lg * LANES:(lg + 1) * LANES]

    def value_piece(c):
        cols = slice(c * col_w, (c + 1) * col_w)
        o_ref[:, cols] += lax.dot_general(pt_r[...], v_ref[:, cols], (((0,), (0,)), ((), ())),
                                          preferred_element_type=F32)

    n_i0 = te // PEER_N_KEYS
    n_q = PEER_N_KEYS // GATE_ROWS

    def gate_piece(lg, q):
        ls = slice(lg * LANES, (lg + 1) * LANES)
        qs = slice(q * GATE_ROWS, (q + 1) * GATE_ROWS)
        gates = [None] * n_i0
        for h in range(PEER_HEADS):
            s1 = s1_ref[h, lg, qs, :]
            e1 = e1_ref[h, lg, qs, :]
            for i0 in range(n_i0):
                thr = thr_ref[h, i0:i0 + 1, ls]
                w0 = w0_ref[h, i0:i0 + 1, ls]
                term = jnp.where(s1 >= thr, e1 * w0, 0.0)
                gates[i0] = term if gates[i0] is None else gates[i0] + term
        for i0 in range(n_i0):
            er = slice(i0 * PEER_N_KEYS + q * GATE_ROWS, i0 * PEER_N_KEYS + (q + 1) * GATE_ROWS)
            pt_w[er, ls] = (_gelu_tanh_x2(at_r[lg, er, :]) * gates[i0]).astype(BF16)

    mxu_pieces = [functools.partial(hidden_piece, j) for j in range(n_sub)]
    mxu_pieces += [functools.partial(value_piece, c) for c in range(n_sub)]
    vec_pieces = [(lg, q) for lg in range(tb // LANES) for q in range(n_q)]
    assert len(vec_pieces) % len(mxu_pieces) == 0
    per_mxu = len(vec_pieces) // len(mxu_pieces)
    for n, piece in enumerate(mxu_pieces):
        for lg, q in vec_pieces[n * per_mxu:(n + 1) * per_mxu]:
            gate_piece(lg, q)
        piece()


def _peer_kernel(ht_ref, u_ref, v_ref, thr_ref, w0_ref, s1_ref, e1_ref, x1_ref, g_ref,
                 o_ref, at0_ref, at1_ref, pt0_ref, pt1_ref, *, te, tb, n_e):
    g = pl.program_id(0)
    e_out = (g - 2) % n_e
    out_live = g >= 2
    at_refs = (at0_ref, at1_ref)
    pt_refs = (pt0_ref, pt1_ref)

    @pl.when(g == 0)
    def _():
        for r in at_refs + pt_refs:
            r[...] = jnp.zeros(r.shape, r.dtype)

    @pl.when((g == 0) | (out_live & (e_out == 0)))
    def _():
        o_ref[...] = x1_ref[...]

    for parity in range(2):
        @pl.when(g % 2 == parity)
        def _(parity=parity):
            _peer_stage(ht_ref, u_ref, v_ref, thr_ref, w0_ref, s1_ref, e1_ref, o_ref,
                        at_refs[parity], at_refs[1 - parity],
                        pt_refs[1 - parity], pt_refs[parity], te=te, tb=tb)

    @pl.when(out_live & (e_out == n_e - 1))
    def _():
        o_ref[...] = _rms(o_ref[...], g_ref[...])


def peer_experts(ht, u, v, tables, x1, g, tb=512, te=1024):
    d, s = ht.shape
    n_exp = u.shape[0]
    n_i0 = te // PEER_N_KEYS
    n_e = n_exp // te
    n_items = (s // tb) * n_e
    thr, w0, s1, e1 = tables

    def item(gi, lag):
        n = jnp.clip(gi - lag, 0, n_items - 1)
        return n // n_e, n % n_e

    row_spec = pl.BlockSpec((PEER_HEADS, n_i0, tb), lambda gi: (0, item(gi, 1)[1], item(gi, 1)[0]))
    tab_spec = pl.BlockSpec((PEER_HEADS, tb // LANES, PEER_N_KEYS, LANES),
                            lambda gi: (0, item(gi, 1)[0], 0, 0))
    return pl.pallas_call(
        functools.partial(_peer_kernel, te=te, tb=tb, n_e=n_e),
        grid=(n_items + 2,),
        in_specs=[pl.BlockSpec((d, tb), lambda gi: (0, item(gi, 0)[0])),
                  pl.BlockSpec((te, d), lambda gi: (item(gi, 0)[1], 0)),
                  pl.BlockSpec((te, d), lambda gi: (item(gi, 2)[1], 0)),
                  row_spec, row_spec, tab_spec, tab_spec,
                  pl.BlockSpec((tb, d), lambda gi: (item(gi, 2)[0], 0),
                               pipeline_mode=pl.Buffered(1)),
                  pl.BlockSpec((1, d), lambda gi: (0, 0))],
        out_specs=pl.BlockSpec((tb, d), lambda gi: (item(gi, 2)[0], 0)),
        out_shape=jax.ShapeDtypeStruct((s, d), F32),
        scratch_shapes=[pltpu.VMEM((tb // LANES, te, LANES), F32),
                        pltpu.VMEM((tb // LANES, te, LANES), F32),
                        pltpu.VMEM((te, tb), BF16),
                        pltpu.VMEM((te, tb), BF16)],
        compiler_params=_params(("arbitrary",)),
        name="peer_experts",
    )(ht, u, v, thr, w0, s1, e1, x1, g.reshape(1, d))


def kernel(x, norm1_g, w_in, q_norm_g, k_norm_g, w_out, norm2_g, peer_w_query,
           peer_sub_keys, peer_u, peer_v, final_norm_g):
    b, s, d = x.shape
    assert b == 1
    a_width = A_HEADS * HEAD_DIM
    depth = w_in.shape[0]
    xs = x.reshape(s, d)
    for l in range(depth):
        proj = in_projection(xs, norm1_g[l], w_in[l].astype(BF16))
        mixed_a = dilated_attention(proj, s)
        qt, kb, vt = gqa_prep(proj, q_norm_g[l], k_norm_g[l], s, col_block=2)
        mixed_b = gqa_attention(qt, kb, vt, s)
        w_o = w_out[l].astype(BF16)
        x1, ht = out_projection(mixed_a, mixed_b, w_o[:a_width], w_o[a_width:], xs, norm2_g[l])
        keys = peer_sub_keys[l].reshape(2 * PEER_HEADS, PEER_N_KEYS, -1).astype(BF16)
        tables = peer_router(ht, peer_w_query[l].T.astype(BF16), keys)
        assert l == depth - 1
        xs = peer_experts(ht, peer_u[l].astype(BF16), peer_v[l].astype(BF16), tables, x1,
                          final_norm_g)
    return xs.reshape(b, s, d)
```

```python
import functools

import numpy as np
import jax
import jax.numpy as jnp
from jax import lax
from jax.experimental import pallas as pl
from jax.experimental.pallas import tpu as pltpu

F32 = jnp.float32
BF16 = jnp.bfloat16

HEAD_DIM = 128
A_HEADS = 8
DIL_PATTERNS = ((128, 1), (512, 4), (2048, 16))
B_Q_HEADS = 8
B_KV_HEADS = 2
GRID_W = 64
ROPE_THETA = 10000.0
NORM_EPS = 1e-6
PEER_HEADS = 8
PEER_N_KEYS = 128
PEER_TOPK = 16

LANES = 128
SUBLANES = 8
NEG_BIG = -1e30
LOG2_E = 1.4426950408889634
VMEM_LIMIT = 56 * 1024 * 1024


def _params(sem, vmem=VMEM_LIMIT):
    return pltpu.CompilerParams(dimension_semantics=sem, vmem_limit_bytes=vmem)


def _rms(x, g):
    ms = jnp.mean(x * x, axis=-1, keepdims=True)
    return x * lax.rsqrt(ms + NORM_EPS) * g


def _inproj_kernel(x_ref, g_ref, w_ref, o_ref, h_ref):
    @pl.when(pl.program_id(1) == 0)
    def _():
        h_ref[...] = _rms(x_ref[...], g_ref[...]).astype(BF16)

    o_ref[...] = jnp.dot(h_ref[...], w_ref[...], preferred_element_type=F32)


def in_projection(x, g, w_bf16, tm=1024, tn=512):
    s, d = x.shape
    n = w_bf16.shape[1]
    return pl.pallas_call(
        _inproj_kernel,
        grid=(s // tm, n // tn),
        in_specs=[pl.BlockSpec((tm, d), lambda i, j: (i, 0)),
                  pl.BlockSpec((1, d), lambda i, j: (0, 0)),
                  pl.BlockSpec((d, tn), lambda i, j: (0, j))],
        out_specs=pl.BlockSpec((tm, tn), lambda i, j: (i, j)),
        out_shape=jax.ShapeDtypeStruct((s, n), F32),
        scratch_shapes=[pltpu.VMEM((tm, d), BF16)],
        compiler_params=_params(("parallel", "arbitrary")),
        name="in_projection",
    )(x, g.reshape(1, d), w_bf16)


DIL_QBLK = 2048
DIL_SUB = 128
DIL_KWIN = 256
DIL_UNROLL = 16
DIL_EDGE_CASES = 3


def _dilated_kernel(q_ref, k_ref, v_ref, o_ref, m_ref, l_ref, acc_ref, bias_ref, *, seq):
    h = pl.program_id(0)
    base = pl.program_id(1) * DIL_QBLK
    slope = jnp.exp2(-(h + 1).astype(F32))
    scale = HEAD_DIM ** -0.5

    m_ref[...] = jnp.full(m_ref.shape, NEG_BIG, F32)
    l_ref[...] = jnp.zeros(l_ref.shape, F32)
    acc_ref[...] = jnp.zeros(acc_ref.shape, F32)

    row = lax.broadcasted_iota(jnp.int32, (DIL_SUB, DIL_KWIN), 0)
    col = lax.broadcasted_iota(jnp.int32, (DIL_SUB, DIL_KWIN), 1)
    col_minus_row = col - row

    for b_idx, (w, d) in enumerate(DIL_PATTERNS):
        half = (w // 2) // d
        assert half * 2 + DIL_SUB == DIL_KWIN
        seg_len = seq // d

        for v_idx in range(DIL_EDGE_CASES):
            rel = jnp.abs(col_minus_row - v_idx * half)
            bias_ref[b_idx * DIL_EDGE_CASES + v_idx] = jnp.where(
                rel <= half, -(slope * d) * rel.astype(F32), NEG_BIG)

        def rows(start, n, d=d):
            return pl.ds(start, n, stride=d) if d > 1 else pl.ds(start, n)

        def body(c, carry, d=d, half=half, seg_len=seg_len, rows=rows, b_idx=b_idx):
            r = c % d
            cc = c // d
            p0 = base // d + cc * DIL_SUB
            kp = jnp.clip(p0 - half, 0, seg_len - DIL_KWIN)
            lq = r + d * (cc * DIL_SUB)
            ks = r + d * kp

            q = (q_ref[rows(lq, DIL_SUB), :] * scale).astype(BF16)
            k = k_ref[rows(ks, DIL_KWIN), :].astype(BF16)
            v = v_ref[rows(ks, DIL_KWIN), :].astype(BF16)
            s = lax.dot_general(q, k, (((1,), (1,)), ((), ())), preferred_element_type=F32)
            s = s + bias_ref[b_idx * DIL_EDGE_CASES + (p0 - kp) // half]
            mb = jnp.max(s, axis=1, keepdims=True)
            p = jnp.exp(s - mb)
            lb = jnp.sum(p, axis=1, keepdims=True)
            ob = jnp.dot(p.astype(BF16), v, preferred_element_type=F32)

            idx = rows(lq, DIL_SUB)
            m_old = m_ref[idx, :]
            m_new = jnp.maximum(m_old, mb)
            a_old = jnp.exp(m_old - m_new)
            a_blk = jnp.exp(mb - m_new)
            m_ref[idx, :] = m_new
            l_ref[idx, :] = a_old * l_ref[idx, :] + a_blk * lb
            acc_ref[idx, :] = a_old * acc_ref[idx, :] + a_blk * ob
            return carry

        def group(cg, carry, body=body):
            for u in range(DIL_UNROLL):
                body(cg * DIL_UNROLL + u, carry)
            return carry

        lax.fori_loop(0, DIL_QBLK // DIL_SUB // DIL_UNROLL, group, 0)

    o_ref[...] = (acc_ref[...] / l_ref[...]).astype(o_ref.dtype)


def dilated_attention(proj, seq):
    assert seq % DIL_QBLK == 0
    for _, d in DIL_PATTERNS:
        assert DIL_QBLK // d >= DIL_SUB and seq // d >= DIL_KWIN
    nh = A_HEADS
    return pl.pallas_call(
        functools.partial(_dilated_kernel, seq=seq),
        grid=(nh, seq // DIL_QBLK),
        in_specs=[pl.BlockSpec((DIL_QBLK, HEAD_DIM), lambda h, i: (i, h)),
                  pl.BlockSpec((seq, HEAD_DIM), lambda h, i: (0, nh + h)),
                  pl.BlockSpec((seq, HEAD_DIM), lambda h, i: (0, 2 * nh + h))],
        out_specs=pl.BlockSpec((DIL_QBLK, HEAD_DIM), lambda h, i: (i, h)),
        out_shape=jax.ShapeDtypeStruct((seq, nh * HEAD_DIM), BF16),
        scratch_shapes=[pltpu.VMEM((DIL_QBLK, LANES), F32),
                        pltpu.VMEM((DIL_QBLK, LANES), F32),
                        pltpu.VMEM((DIL_QBLK, HEAD_DIM), F32),
                        pltpu.VMEM((len(DIL_PATTERNS) * DIL_EDGE_CASES, DIL_SUB, DIL_KWIN), F32)],
        compiler_params=_params(("parallel", "arbitrary")),
        name="dilated_attention",
    )(proj, proj, proj)


def _rope_tables(seq):
    rows = seq // GRID_W
    row = jnp.repeat(jnp.arange(rows, dtype=F32), GRID_W)
    col = jnp.tile(jnp.arange(GRID_W, dtype=F32), rows)
    half = HEAD_DIM // 2
    inv = ROPE_THETA ** (-jnp.arange(0, half, 2, dtype=F32) / half)
    ang = jnp.concatenate([row[:, None] * inv, col[:, None] * inv], axis=-1)
    cos = jnp.repeat(jnp.cos(ang), 2, axis=-1)
    sign = jnp.tile(jnp.asarray([-1.0, 1.0], F32), half)
    sin = jnp.repeat(jnp.sin(ang), 2, axis=-1) * sign
    return cos, sin


def _gqa_prep_kernel(p_ref, cos_ref, sin_ref, qg_ref, kg_ref, qt_ref, k_ref, vt_ref):
    cos = cos_ref[...]
    sin = sin_ref[...]
    even = (lax.broadcasted_iota(jnp.int32, cos.shape, 1) % 2) == 0

    def norm_rope(xh, g):
        y = _rms(xh, g)
        partner = jnp.where(even, pltpu.roll(y, LANES - 1, 1), pltpu.roll(y, 1, 1))
        return y * cos + partner * sin

    scale = HEAD_DIM ** -0.5 * LOG2_E
    for hq in range(B_Q_HEADS):
        xh = p_ref[:, hq * HEAD_DIM:(hq + 1) * HEAD_DIM]
        y = norm_rope(xh, qg_ref[...]) * scale
        qt_ref[hq * HEAD_DIM:(hq + 1) * HEAD_DIM, :] = y.T.astype(BF16)
    k0 = B_Q_HEADS * HEAD_DIM
    v0 = k0 + B_KV_HEADS * HEAD_DIM
    for hk in range(B_KV_HEADS):
        xh = p_ref[:, k0 + hk * HEAD_DIM:k0 + (hk + 1) * HEAD_DIM]
        k_ref[:, hk * HEAD_DIM:(hk + 1) * HEAD_DIM] = norm_rope(xh, kg_ref[...]).astype(BF16)
        vh = p_ref[:, v0 + hk * HEAD_DIM:v0 + (hk + 1) * HEAD_DIM]
        vt_ref[hk * HEAD_DIM:(hk + 1) * HEAD_DIM, :] = vh.T.astype(BF16)


def gqa_prep(proj, q_g, k_g, seq, col_block, ts=512):
    wb = (B_Q_HEADS + 2 * B_KV_HEADS) * HEAD_DIM
    cos, sin = _rope_tables(seq)
    return pl.pallas_call(
        _gqa_prep_kernel,
        grid=(seq // ts,),
        in_specs=[pl.BlockSpec((ts, wb), lambda i: (i, col_block)),
                  pl.BlockSpec((ts, HEAD_DIM), lambda i: (i, 0)),
                  pl.BlockSpec((ts, HEAD_DIM), lambda i: (i, 0)),
                  pl.BlockSpec((1, HEAD_DIM), lambda i: (0, 0)),
                  pl.BlockSpec((1, HEAD_DIM), lambda i: (0, 0))],
        out_specs=[pl.BlockSpec((B_Q_HEADS * HEAD_DIM, ts), lambda i: (0, i)),
                   pl.BlockSpec((ts, B_KV_HEADS * HEAD_DIM), lambda i: (i, 0)),
                   pl.BlockSpec((B_KV_HEADS * HEAD_DIM, ts), lambda i: (0, i))],
        out_shape=[jax.ShapeDtypeStruct((B_Q_HEADS * HEAD_DIM, seq), BF16),
                   jax.ShapeDtypeStruct((seq, B_KV_HEADS * HEAD_DIM), BF16),
                   jax.ShapeDtypeStruct((B_KV_HEADS * HEAD_DIM, seq), BF16)],
        compiler_params=_params(("parallel",)),
        name="gqa_prep",
    )(proj, cos, sin, q_g.reshape(1, HEAD_DIM), k_g.reshape(1, HEAD_DIM))


STAGE_ROWS = 32
GQA_UNROLL = 1


def _data_dependent_zero(tiles):
    bits = None
    for t in tiles:
        b = pltpu.bitcast(t, jnp.uint32)
        bits = b if bits is None else bits | b
    zero_words = (bits >> 16) >> 16
    return zero_words.astype(jnp.int32).astype(F32).astype(BF16)


def _gqa_kernel(qt_ref, k_ref, vt_ref, o_ref, m_ref, l_ref, acc_ref, st_ref, ks_ref,
                *, seq, tk, nh):
    n_chunks = seq // tk
    tq = qt_ref.shape[1]
    m_ref[...] = jnp.full(m_ref.shape, NEG_BIG, F32)
    l_ref[...] = jnp.zeros(l_ref.shape, F32)
    acc_ref[...] = jnp.zeros(acc_ref.shape, F32)

    def q_head(hh):
        return qt_ref[hh * HEAD_DIM:(hh + 1) * HEAD_DIM, :]

    def chunk_start(c):
        return pl.multiple_of(jnp.minimum(c, n_chunks - 1) * tk, tk)

    st_ref[...] = jnp.dot(k_ref[pl.ds(0, tk), :], q_head(0), preferred_element_type=F32)

    def chunk_items(c, st):
        here = chunk_start(c)
        vt = vt_ref[:, pl.ds(here, tk)]
        for hh in range(nh):
            m_old = m_ref[hh]
            m_new = jnp.maximum(m_old, jnp.max(st, axis=0, keepdims=True))
            alpha = jnp.exp2(m_old - m_new)
            p = jnp.exp2(st - m_new)
            pb = p.astype(BF16)

            nxt = here if hh + 1 < nh else chunk_start(c + 1)
            slot = ks_ref.at[hh % 2]
            for r in range(0, tk, STAGE_ROWS):
                strip = [pb[r:r + STAGE_ROWS, j:j + LANES] for j in range(0, tq, LANES)]
                zero = _data_dependent_zero(strip)
                piece = k_ref[pl.ds(pl.multiple_of(nxt + r, STAGE_ROWS), STAGE_ROWS), :]
                slot[r:r + STAGE_ROWS, :] = piece + pltpu.repeat(zero, STAGE_ROWS // zero.shape[0], axis=0)
            st = jnp.dot(slot[...], q_head((hh + 1) % nh), preferred_element_type=F32)

            l_ref[hh] = alpha * l_ref[hh] + jnp.sum(p, axis=0, keepdims=True)
            acc_ref[hh] = alpha * acc_ref[hh] + jnp.dot(vt, pb, preferred_element_type=F32)
            m_ref[hh] = m_new
        return st

    def body(i, carry):
        st = st_ref[...]
        for u in range(GQA_UNROLL):
            st = chunk_items(i * GQA_UNROLL + u, st)
        st_ref[...] = st
        return carry

    lax.fori_loop(0, n_chunks // GQA_UNROLL, body, 0)
    for hh in range(nh):
        o_ref[:, hh * HEAD_DIM:(hh + 1) * HEAD_DIM] = (acc_ref[hh] / l_ref[hh]).T.astype(o_ref.dtype)


def gqa_attention(qt, k, vt, seq, tq=512, tk=512):
    nh = B_Q_HEADS // B_KV_HEADS
    return pl.pallas_call(
        functools.partial(_gqa_kernel, seq=seq, tk=tk, nh=nh),
        grid=(B_KV_HEADS, seq // tq),
        in_specs=[pl.BlockSpec((nh * HEAD_DIM, tq), lambda g, i: (g, i)),
                  pl.BlockSpec((seq, HEAD_DIM), lambda g, i: (0, g)),
                  pl.BlockSpec((HEAD_DIM, seq), lambda g, i: (g, 0))],
        out_specs=pl.BlockSpec((tq, nh * HEAD_DIM), lambda g, i: (i, g)),
        out_shape=jax.ShapeDtypeStruct((seq, B_Q_HEADS * HEAD_DIM), BF16),
        scratch_shapes=[pltpu.VMEM((nh, 1, tq), F32),
                        pltpu.VMEM((nh, 1, tq), F32),
                        pltpu.VMEM((nh, HEAD_DIM, tq), F32),
                        pltpu.VMEM((tk, tq), F32),
                        pltpu.VMEM((2, tk, HEAD_DIM), BF16)],
        compiler_params=_params(("parallel", "arbitrary")),
        name="gqa_attention",
    )(qt, k, vt)


def _outproj_kernel(ma_ref, mb_ref, wa_ref, wb_ref, x_ref, g_ref, x1_ref, ht_ref):
    y = x_ref[...]
    y = y + jnp.dot(ma_ref[...], wa_ref[...], preferred_element_type=F32)
    y = y + jnp.dot(mb_ref[...], wb_ref[...], preferred_element_type=F32)
    x1_ref[...] = y
    ht_ref[...] = _rms(y, g_ref[...]).T.astype(BF16)


def out_projection(mixed_a, mixed_b, w_a, w_b, x, g, tm=512):
    s, d = x.shape
    wa, wb = mixed_a.shape[1], mixed_b.shape[1]
    return pl.pallas_call(
        _outproj_kernel,
        grid=(s // tm,),
        in_specs=[pl.BlockSpec((tm, wa), lambda i: (i, 0)),
                  pl.BlockSpec((tm, wb), lambda i: (i, 0)),
                  pl.BlockSpec((wa, d), lambda i: (0, 0)),
                  pl.BlockSpec((wb, d), lambda i: (0, 0)),
                  pl.BlockSpec((tm, d), lambda i: (i, 0)),
                  pl.BlockSpec((1, d), lambda i: (0, 0))],
        out_specs=[pl.BlockSpec((tm, d), lambda i: (i, 0)),
                   pl.BlockSpec((d, tm), lambda i: (0, i))],
        out_shape=[jax.ShapeDtypeStruct((s, d), F32),
                   jax.ShapeDtypeStruct((d, s), BF16)],
        compiler_params=_params(("parallel",)),
        name="out_projection",
    )(mixed_a, mixed_b, w_a, w_b, x, g.reshape(1, d))


N_RANK = PEER_TOPK + 1
RANK_ROWS = 24
CAND_COUNTS = tuple(N_RANK // (a + 1) for a in range(N_RANK))
CAND_ROWS = 8 * (1 << (-(-sum(CAND_COUNTS) // 8) - 1).bit_length())
ROUTER_UNROLL = 8


def _odd_even_merge_sort_pairs(n):
    pairs = []

    def merge(lo, hi, r):
        step = r * 2
        if step < hi - lo:
            merge(lo, hi, step)
            merge(lo + r, hi, step)
            pairs.extend((i, i + r) for i in range(lo + r, hi - r, step))
        else:
            pairs.append((lo, lo + r))

    def sort(lo, hi):
        if hi - lo >= 1:
            mid = lo + (hi - lo) // 2
            sort(lo, mid)
            sort(mid + 1, hi)
            merge(lo, hi, 1)

    sort(0, n - 1)
    return pairs


def _extract_sorted_tile(tile, n, emit):
    depth = tile.shape[0] // SUBLANES
    lists = [tile[SUBLANES * j:SUBLANES * (j + 1)] for j in range(depth)]
    for i, j in _odd_even_merge_sort_pairs(depth):
        lists[i], lists[j] = jnp.maximum(lists[i], lists[j]), jnp.minimum(lists[i], lists[j])
    for r in range(n):
        mj = jnp.max(lists[0], axis=0, keepdims=True)
        emit(r, mj)
        took = lists[0] == mj
        for j in range(min(depth, n - (r + 1))):
            below = lists[j + 1] if j + 1 < depth else -jnp.inf
            lists[j] = jnp.where(took, below, lists[j])


def _router_kernel(ht_ref, wqt_ref, keys_ref, thr_ref, w0_ref, s1_ref, e1_ref,
                   q_scr, s_scr, top_scr, cand_scr, best_scr, *, tb):
    q_scr[...] = jnp.dot(wqt_ref[...], ht_ref[...], preferred_element_type=F32).astype(BF16)
    lane_groups = [slice(lg * LANES, (lg + 1) * LANES) for lg in range(tb // LANES)]

    def half_body(hc):
        qhc = q_scr[pl.ds(pl.multiple_of(hc * PEER_N_KEYS, PEER_N_KEYS), PEER_N_KEYS), :]
        s = jnp.dot(keys_ref[hc], qhc, preferred_element_type=F32)
        s_scr[hc] = s
        for ls in lane_groups:
            def emit(j, mj, ls=ls):
                top_scr[hc, pl.ds(j, 1), ls] = mj

            _extract_sorted_tile(s[:, ls], N_RANK, emit)

    def half_group(i, carry):
        for u in range(ROUTER_UNROLL):
            half_body(i * ROUTER_UNROLL + u)
        return carry

    lax.fori_loop(0, 2 * PEER_HEADS // ROUTER_UNROLL, half_group, 0)

    def head_body(h, carry):
        t0 = top_scr[2 * h]
        t1 = top_scr[2 * h + 1]
        cand_scr[...] = jnp.full(cand_scr.shape, -jnp.inf, F32)
        off = 0
        for a, nb in enumerate(CAND_COUNTS):
            cand_scr[off:off + nb, :] = t0[a:a + 1, :] + t1[0:nb, :]
            off += nb

        for ls in lane_groups:
            def emit(j, mj, ls=ls):
                best_scr[pl.ds(j, 1), ls] = mj

            _extract_sorted_tile(cand_scr[:, ls], N_RANK, emit)
        best = [best_scr[j:j + 1, :] for j in range(N_RANK)]
        theta = 0.5 * (best[PEER_TOPK - 1] + best[PEER_TOPK])
        z = jnp.zeros_like(theta)
        for j in range(PEER_TOPK):
            z = z + jnp.exp(best[j] - best[0])

        s0 = s_scr[2 * h]
        s1 = s_scr[2 * h + 1]
        thr_ref[h] = theta - s0
        w0_ref[h] = jnp.exp(s0 - t0[0:1, :]) * (0.5 / z)
        e1 = jnp.exp(s1 - t1[0:1, :])
        for lg, ls in enumerate(lane_groups):
            s1_ref[h, lg] = s1[:, ls]
            e1_ref[h, lg] = e1[:, ls]
        return carry

    lax.fori_loop(0, PEER_HEADS, head_body, 0)


def peer_router(ht, wqt, keys, tb=512):
    d, s = ht.shape
    nq = wqt.shape[0]
    tab = jax.ShapeDtypeStruct((PEER_HEADS, PEER_N_KEYS, s), F32)
    tab_spec = pl.BlockSpec((PEER_HEADS, PEER_N_KEYS, tb), lambda i: (0, 0, i))
    tile = jax.ShapeDtypeStruct((PEER_HEADS, s // LANES, PEER_N_KEYS, LANES), F32)
    tile_spec = pl.BlockSpec((PEER_HEADS, tb // LANES, PEER_N_KEYS, LANES), lambda i: (0, i, 0, 0))
    return pl.pallas_call(
        functools.partial(_router_kernel, tb=tb),
        grid=(s // tb,),
        in_specs=[pl.BlockSpec((d, tb), lambda i: (0, i)),
                  pl.BlockSpec((nq, d), lambda i: (0, 0)),
                  pl.BlockSpec(keys.shape, lambda i: (0, 0, 0))],
        out_specs=[tab_spec, tab_spec, tile_spec, tile_spec],
        out_shape=[tab, tab, tile, tile],
        scratch_shapes=[pltpu.VMEM((nq, tb), BF16),
                        pltpu.VMEM((2 * PEER_HEADS, PEER_N_KEYS, tb), F32),
                        pltpu.VMEM((2 * PEER_HEADS, RANK_ROWS, tb), F32),
                        pltpu.VMEM((CAND_ROWS, tb), F32),
                        pltpu.VMEM((RANK_ROWS, tb), F32)],
        compiler_params=_params(("parallel",)),
        name="peer_router",
    )(ht, wqt, keys)


GELU_C = 0.7978845608028654


def _gelu_tanh_x2(x):
    u = x * (GELU_C + (GELU_C * 0.044715) * (x * x))
    return x + x * jnp.tanh(u)


PEER_SUB = 256
GATE_ROWS = 32


def _peer_stage(ht_ref, u_ref, v_ref, thr_ref, w0_ref, s1_ref, e1_ref, o_ref,
                at_w, at_r, pt_w, pt_r, *, te, tb):
    d = o_ref.shape[1]
    n_sub = te // PEER_SUB
    col_w = d // n_sub

    def hidden_piece(j):
        rows = slice(j * PEER_SUB, (j + 1) * PEER_SUB)
        at = jnp.dot(u_ref[rows, :], ht_ref[...], preferred_element_type=F32)
        for lg in range(tb // LANES):
            at_w[lg, rows, :] = at[:, lg * LANES:(lg + 1) * LANES]

    def value_piece(c):
        cols = slice(c * col_w, (c + 1) * col_w)
        o_ref[:, cols] += lax.dot_general(pt_r[...], v_ref[:, cols], (((0,), (0,)), ((), ())),
                                          preferred_element_type=F32)

    n_i0 = te // PEER_N_KEYS
    n_q = PEER_N_KEYS // GATE_ROWS

    def gate_piece(lg, q):
        ls = slice(lg * LANES, (lg + 1) * LANES)
        qs = slice(q * GATE_ROWS, (q + 1) * GATE_ROWS)
        gates = [None] * n_i0
        for h in range(PEER_HEADS):
            s1 = s1_ref[h, lg, qs, :]
            e1 = e1_ref[h, lg, qs, :]
            for i0 in range(n_i0):
                thr = thr_ref[h, i0:i0 + 1, ls]
                w0 = w0_ref[h, i0:i0 + 1, ls]
                term = jnp.where(s1 >= thr, e1 * w0, 0.0)
                gates[i0] = term if gates[i0] is None else gates[i0] + term
        for i0 in range(n_i0):
            er = slice(i0 * PEER_N_KEYS + q * GATE_ROWS, i0 * PEER_N_KEYS + (q + 1) * GATE_ROWS)
            pt_w[er, ls] = (_gelu_tanh_x2(at_r[lg, er, :]) * gates[i0]).astype(BF16)

    mxu_pieces = [functools.partial(hidden_piece, j) for j in range(n_sub)]
    mxu_pieces += [functools.partial(value_piece, c) for c in range(n_sub)]
    vec_pieces = [(lg, q) for lg in range(tb // LANES) for q in range(n_q)]
    assert len(vec_pieces) % len(mxu_pieces) == 0
    per_mxu = len(vec_pieces) // len(mxu_pieces)
    for n, piece in enumerate(mxu_pieces):
        for lg, q in vec_pieces[n * per_mxu:(n + 1) * per_mxu]:
            gate_piece(lg, q)
        piece()


def _peer_kernel(ht_ref, u_ref, v_ref, thr_ref, w0_ref, s1_ref, e1_ref, x1_ref, g_ref,
                 o_ref, at0_ref, at1_ref, pt0_ref, pt1_ref, *, te, tb, n_e):
    g = pl.program_id(0)
    e_out = (g - 2) % n_e
    out_live = g >= 2
    at_refs = (at0_ref, at1_ref)
    pt_refs = (pt0_ref, pt1_ref)

    @pl.when(g == 0)
    def _():
        for r in at_refs + pt_refs:
            r[...] = jnp.zeros(r.shape, r.dtype)

    @pl.when((g == 0) | (out_live & (e_out == 0)))
    def _():
        o_ref[...] = x1_ref[...]

    for parity in range(2):
        @pl.when(g % 2 == parity)
        def _(parity=parity):
            _peer_stage(ht_ref, u_ref, v_ref, thr_ref, w0_ref, s1_ref, e1_ref, o_ref,
                        at_refs[parity], at_refs[1 - parity],
                        pt_refs[1 - parity], pt_refs[parity], te=te, tb=tb)

    @pl.when(out_live & (e_out == n_e - 1))
    def _():
        o_ref[...] = _rms(o_ref[...], g_ref[...])


def peer_experts(ht, u, v, tables, x1, g, tb=512, te=1024):
    d, s = ht.shape
    n_exp = u.shape[0]
    n_i0 = te // PEER_N_KEYS
    n_e = n_exp // te
    n_items = (s // tb) * n_e
    thr, w0, s1, e1 = tables

    def item(gi, lag):
        n = jnp.clip(gi - lag, 0, n_items - 1)
        return n // n_e, n % n_e

    row_spec = pl.BlockSpec((PEER_HEADS, n_i0, tb), lambda gi: (0, item(gi, 1)[1], item(gi, 1)[0]))
    tab_spec = pl.BlockSpec((PEER_HEADS, tb // LANES, PEER_N_KEYS, LANES),
                            lambda gi: (0, item(gi, 1)[0], 0, 0))
    return pl.pallas_call(
        functools.partial(_peer_kernel, te=te, tb=tb, n_e=n_e),
        grid=(n_items + 2,),
        in_specs=[pl.BlockSpec((d, tb), lambda gi: (0, item(gi, 0)[0])),
                  pl.BlockSpec((te, d), lambda gi: (item(gi, 0)[1], 0)),
                  pl.BlockSpec((te, d), lambda gi: (item(gi, 2)[1], 0)),
                  row_spec, row_spec, tab_spec, tab_spec,
                  pl.BlockSpec((tb, d), lambda gi: (item(gi, 2)[0], 0),
                               pipeline_mode=pl.Buffered(1)),
                  pl.BlockSpec((1, d), lambda gi: (0, 0))],
        out_specs=pl.BlockSpec((tb, d), lambda gi: (item(gi, 2)[0], 0)),
        out_shape=jax.ShapeDtypeStruct((s, d), F32),
        scratch_shapes=[pltpu.VMEM((tb // LANES, te, LANES), F32),
                        pltpu.VMEM((tb // LANES, te, LANES), F32),
                        pltpu.VMEM((te, tb), BF16),
                        pltpu.VMEM((te, tb), BF16)],
        compiler_params=_params(("arbitrary",)),
        name="peer_experts",
    )(ht, u, v, thr, w0, s1, e1, x1, g.reshape(1, d))


def kernel(x, norm1_g, w_in, q_norm_g, k_norm_g, w_out, norm2_g, peer_w_query,
           peer_sub_keys, peer_u, peer_v, final_norm_g):
    b, s, d = x.shape
    assert b == 1
    a_width = A_HEADS * HEAD_DIM
    depth = w_in.shape[0]
    xs = x.reshape(s, d)
    for l in range(depth):
        proj = in_projection(xs, norm1_g[l], w_in[l].astype(BF16))
        mixed_a = dilated_attention(proj, s)
        qt, kb, vt = gqa_prep(proj, q_norm_g[l], k_norm_g[l], s, col_block=2)
        mixed_b = gqa_attention(qt, kb, vt, s)
        w_o = w_out[l].astype(BF16)
        x1, ht = out_projection(mixed_a, mixed_b, w_o[:a_width], w_o[a_width:], xs, norm2_g[l])
        keys = peer_sub_keys[l].reshape(2 * PEER_HEADS, PEER_N_KEYS, -1).astype(BF16)
        tables = peer_router(ht, peer_w_query[l].T.astype(BF16), keys)
        assert l == depth - 1
        xs = peer_experts(ht, peer_u[l].astype(BF16), peer_v[l].astype(BF16), tables, x1,
                          final_norm_g)
    return xs.reshape(b, s, d)
```

```python
import functools

import numpy as np
import jax
import jax.numpy as jnp
from jax import lax
from jax.experimental import pallas as pl
from jax.experimental.pallas import tpu as pltpu

F32 = jnp.float32
BF16 = jnp.bfloat16

HEAD_DIM = 128
A_HEADS = 8
DIL_PATTERNS = ((128, 1), (512, 4), (2048, 16))
B_Q_HEADS = 8
B_KV_HEADS = 2
GRID_W = 64
ROPE_THETA = 10000.0
NORM_EPS = 1e-6
PEER_HEADS = 8
PEER_N_KEYS = 128
PEER_TOPK = 16

LANES = 128
SUBLANES = 8
BF16_ROWS = 16
NEG_BIG = -1e30
LOG2_E = 1.4426950408889634
VMEM_LIMIT = 56 * 1024 * 1024


def _params(sem, vmem=VMEM_LIMIT):
    return pltpu.CompilerParams(dimension_semantics=sem, vmem_limit_bytes=vmem)


def _rms(x, g):
    ms = jnp.mean(x * x, axis=-1, keepdims=True)
    return x * lax.rsqrt(ms + NORM_EPS) * g


def _inproj_kernel(x_ref, g_ref, w_ref, o_ref, h_ref):
    @pl.when(pl.program_id(1) == 0)
    def _():
        h_ref[...] = _rms(x_ref[...], g_ref[...]).astype(BF16)

    o_ref[...] = jnp.dot(h_ref[...], w_ref[...], preferred_element_type=F32)


def in_projection(x, g, w_bf16, tm=1024, tn=512):
    s, d = x.shape
    n = w_bf16.shape[1]
    return pl.pallas_call(
        _inproj_kernel,
        grid=(s // tm, n // tn),
        in_specs=[pl.BlockSpec((tm, d), lambda i, j: (i, 0)),
                  pl.BlockSpec((1, d), lambda i, j: (0, 0)),
                  pl.BlockSpec((d, tn), lambda i, j: (0, j))],
        out_specs=pl.BlockSpec((tm, tn), lambda i, j: (i, j)),
        out_shape=jax.ShapeDtypeStruct((s, n), F32),
        scratch_shapes=[pltpu.VMEM((tm, d), BF16)],
        compiler_params=_params(("parallel", "arbitrary")),
        name="in_projection",
    )(x, g.reshape(1, d), w_bf16)


DIL_QBLK = 2048
DIL_SUB = 128
DIL_KWIN = 256
DIL_UNROLL = 16
DIL_EDGE_CASES = 3


def _dilated_kernel(q_ref, k_ref, v_ref, o_ref, m_ref, l_ref, acc_ref, bias_ref, *, seq):
    h = pl.program_id(0)
    base = pl.program_id(1) * DIL_QBLK
    slope = jnp.exp2(-(h + 1).astype(F32))
    scale = HEAD_DIM ** -0.5

    m_ref[...] = jnp.full(m_ref.shape, NEG_BIG, F32)
    l_ref[...] = jnp.zeros(l_ref.shape, F32)
    acc_ref[...] = jnp.zeros(acc_ref.shape, F32)

    row = lax.broadcasted_iota(jnp.int32, (DIL_SUB, DIL_KWIN), 0)
    col = lax.broadcasted_iota(jnp.int32, (DIL_SUB, DIL_KWIN), 1)
    col_minus_row = col - row

    for b_idx, (w, d) in enumerate(DIL_PATTERNS):
        half = (w // 2) // d
        assert half * 2 + DIL_SUB == DIL_KWIN
        seg_len = seq // d

        for v_idx in range(DIL_EDGE_CASES):
            rel = jnp.abs(col_minus_row - v_idx * half)
            bias_ref[b_idx * DIL_EDGE_CASES + v_idx] = jnp.where(
                rel <= half, -(slope * d) * rel.astype(F32), NEG_BIG)

        def rows(start, n, d=d):
            return pl.ds(start, n, stride=d) if d > 1 else pl.ds(start, n)

        def body(c, carry, d=d, half=half, seg_len=seg_len, rows=rows, b_idx=b_idx):
            r = c % d
            cc = c // d
            p0 = base // d + cc * DIL_SUB
            kp = jnp.clip(p0 - half, 0, seg_len - DIL_KWIN)
            lq = r + d * (cc * DIL_SUB)
            ks = r + d * kp

            q = (q_ref[rows(lq, DIL_SUB), :] * scale).astype(BF16)
            k = k_ref[rows(ks, DIL_KWIN), :].astype(BF16)
            v = v_ref[rows(ks, DIL_KWIN), :].astype(BF16)
            s = lax.dot_general(q, k, (((1,), (1,)), ((), ())), preferred_element_type=F32)
            s = s + bias_ref[b_idx * DIL_EDGE_CASES + (p0 - kp) // half]
            mb = jnp.max(s, axis=1, keepdims=True)
            p = jnp.exp(s - mb)
            lb = jnp.sum(p, axis=1, keepdims=True)
            ob = jnp.dot(p.astype(BF16), v, preferred_element_type=F32)

            idx = rows(lq, DIL_SUB)
            m_old = m_ref[idx, :]
            m_new = jnp.maximum(m_old, mb)
            a_old = jnp.exp(m_old - m_new)
            a_blk = jnp.exp(mb - m_new)
            m_ref[idx, :] = m_new
            l_ref[idx, :] = a_old * l_ref[idx, :] + a_blk * lb
            acc_ref[idx, :] = a_old * acc_ref[idx, :] + a_blk * ob
            return carry

        def group(cg, carry, body=body):
            for u in range(DIL_UNROLL):
                body(cg * DIL_UNROLL + u, carry)
            return carry

        lax.fori_loop(0, DIL_QBLK // DIL_SUB // DIL_UNROLL, group, 0)

    o_ref[...] = (acc_ref[...] / l_ref[...]).astype(o_ref.dtype)


def dilated_attention(proj, seq):
    assert seq % DIL_QBLK == 0
    for _, d in DIL_PATTERNS:
        assert DIL_QBLK // d >= DIL_SUB and seq // d >= DIL_KWIN
    nh = A_HEADS
    return pl.pallas_call(
        functools.partial(_dilated_kernel, seq=seq),
        grid=(nh, seq // DIL_QBLK),
        in_specs=[pl.BlockSpec((DIL_QBLK, HEAD_DIM), lambda h, i: (i, h)),
                  pl.BlockSpec((seq, HEAD_DIM), lambda h, i: (0, nh + h)),
                  pl.BlockSpec((seq, HEAD_DIM), lambda h, i: (0, 2 * nh + h))],
        out_specs=pl.BlockSpec((DIL_QBLK, HEAD_DIM), lambda h, i: (i, h)),
        out_shape=jax.ShapeDtypeStruct((seq, nh * HEAD_DIM), BF16),
        scratch_shapes=[pltpu.VMEM((DIL_QBLK, LANES), F32),
                        pltpu.VMEM((DIL_QBLK, LANES), F32),
                        pltpu.VMEM((DIL_QBLK, HEAD_DIM), F32),
                        pltpu.VMEM((len(DIL_PATTERNS) * DIL_EDGE_CASES, DIL_SUB, DIL_KWIN), F32)],
        compiler_params=_params(("parallel", "arbitrary")),
        name="dilated_attention",
    )(proj, proj, proj)


def _rope_tables(seq):
    rows = seq // GRID_W
    row = jnp.repeat(jnp.arange(rows, dtype=F32), GRID_W)
    col = jnp.tile(jnp.arange(GRID_W, dtype=F32), rows)
    half = HEAD_DIM // 2
    inv = ROPE_THETA ** (-jnp.arange(0, half, 2, dtype=F32) / half)
    ang = jnp.concatenate([row[:, None] * inv, col[:, None] * inv], axis=-1)
    cos = jnp.repeat(jnp.cos(ang), 2, axis=-1)
    sign = jnp.tile(jnp.asarray([-1.0, 1.0], F32), half)
    sin = jnp.repeat(jnp.sin(ang), 2, axis=-1) * sign
    return cos, sin


def _gqa_prep_kernel(p_ref, cos_ref, sin_ref, qg_ref, kg_ref, qt_ref, k_ref, vt_ref):
    cos = cos_ref[...]
    sin = sin_ref[...]
    even = (lax.broadcasted_iota(jnp.int32, cos.shape, 1) % 2) == 0

    def norm_rope(xh, g):
        y = _rms(xh, g)
        partner = jnp.where(even, pltpu.roll(y, LANES - 1, 1), pltpu.roll(y, 1, 1))
        return y * cos + partner * sin

    scale = HEAD_DIM ** -0.5 * LOG2_E
    for hq in range(B_Q_HEADS):
        xh = p_ref[:, hq * HEAD_DIM:(hq + 1) * HEAD_DIM]
        y = norm_rope(xh, qg_ref[...]) * scale
        qt_ref[hq * HEAD_DIM:(hq + 1) * HEAD_DIM, :] = y.T.astype(BF16)
    k0 = B_Q_HEADS * HEAD_DIM
    v0 = k0 + B_KV_HEADS * HEAD_DIM
    for hk in range(B_KV_HEADS):
        xh = p_ref[:, k0 + hk * HEAD_DIM:k0 + (hk + 1) * HEAD_DIM]
        k_ref[:, hk * HEAD_DIM:(hk + 1) * HEAD_DIM] = norm_rope(xh, kg_ref[...]).astype(BF16)
        vh = p_ref[:, v0 + hk * HEAD_DIM:v0 + (hk + 1) * HEAD_DIM]
        vt_ref[hk * HEAD_DIM:(hk + 1) * HEAD_DIM, :] = vh.T.astype(BF16)


def gqa_prep(proj, q_g, k_g, seq, col_block, ts=512):
    wb = (B_Q_HEADS + 2 * B_KV_HEADS) * HEAD_DIM
    cos, sin = _rope_tables(seq)
    return pl.pallas_call(
        _gqa_prep_kernel,
        grid=(seq // ts,),
        in_specs=[pl.BlockSpec((ts, wb), lambda i: (i, col_block)),
                  pl.BlockSpec((ts, HEAD_DIM), lambda i: (i, 0)),
                  pl.BlockSpec((ts, HEAD_DIM), lambda i: (i, 0)),
                  pl.BlockSpec((1, HEAD_DIM), lambda i: (0, 0)),
                  pl.BlockSpec((1, HEAD_DIM), lambda i: (0, 0))],
        out_specs=[pl.BlockSpec((B_Q_HEADS * HEAD_DIM, ts), lambda i: (0, i)),
                   pl.BlockSpec((ts, B_KV_HEADS * HEAD_DIM), lambda i: (i, 0)),
                   pl.BlockSpec((B_KV_HEADS * HEAD_DIM, ts), lambda i: (0, i))],
        out_shape=[jax.ShapeDtypeStruct((B_Q_HEADS * HEAD_DIM, seq), BF16),
                   jax.ShapeDtypeStruct((seq, B_KV_HEADS * HEAD_DIM), BF16),
                   jax.ShapeDtypeStruct((B_KV_HEADS * HEAD_DIM, seq), BF16)],
        compiler_params=_params(("parallel",)),
        name="gqa_prep",
    )(proj, cos, sin, q_g.reshape(1, HEAD_DIM), k_g.reshape(1, HEAD_DIM))


STAGE_ROWS = 32
GQA_UNROLL = 1


def _data_dependent_zero(tiles):
    bits = None
    for t in tiles:
        b = pltpu.bitcast(t, jnp.uint32)
        bits = b if bits is None else bits | b
    zero_words = (bits >> 16) >> 16
    return zero_words.astype(jnp.int32).astype(F32).astype(BF16)


def _gqa_kernel(qt_ref, k_ref, vt_ref, o_ref, m_ref, acc_ref, st_ref, ks_ref, *, seq, tk, nh):
    n_chunks = seq // tk
    tq = qt_ref.shape[1]
    m_ref[...] = jnp.full(m_ref.shape, NEG_BIG, F32)
    acc_ref[...] = jnp.zeros(acc_ref.shape, F32)

    def q_head(hh):
        return qt_ref[hh * HEAD_DIM:(hh + 1) * HEAD_DIM, :]

    def chunk_start(c):
        return pl.multiple_of(jnp.minimum(c, n_chunks - 1) * tk, tk)

    st_ref[...] = jnp.dot(k_ref[pl.ds(0, tk), :], q_head(0), preferred_element_type=F32)

    def chunk_items(c, st):
        here = chunk_start(c)
        vt = jnp.concatenate([vt_ref[:, pl.ds(here, tk)], jnp.ones((BF16_ROWS, tk), BF16)], axis=0)
        for hh in range(nh):
            m_old = m_ref[hh]
            m_new = jnp.maximum(m_old, jnp.max(st, axis=0, keepdims=True))
            alpha = jnp.exp2(m_old - m_new)
            p = jnp.exp2(st - m_new)
            pb = p.astype(BF16)

            nxt = here if hh + 1 < nh else chunk_start(c + 1)
            slot = ks_ref.at[hh % 2]
            for r in range(0, tk, STAGE_ROWS):
                strip = [pb[r:r + STAGE_ROWS, j:j + LANES] for j in range(0, tq, LANES)]
                zero = _data_dependent_zero(strip)
                piece = k_ref[pl.ds(pl.multiple_of(nxt + r, STAGE_ROWS), STAGE_ROWS), :]
                slot[r:r + STAGE_ROWS, :] = piece + pltpu.repeat(zero, STAGE_ROWS // zero.shape[0], axis=0)
            st = jnp.dot(slot[...], q_head((hh + 1) % nh), preferred_element_type=F32)

            acc_ref[hh] = alpha * acc_ref[hh] + jnp.dot(vt, pb, preferred_element_type=F32)
            m_ref[hh] = m_new
        return st

    def body(i, carry):
        st = st_ref[...]
        for u in range(GQA_UNROLL):
            st = chunk_items(i * GQA_UNROLL + u, st)
        st_ref[...] = st
        return carry

    lax.fori_loop(0, n_chunks // GQA_UNROLL, body, 0)
    for hh in range(nh):
        acc = acc_ref[hh]
        out = acc[:HEAD_DIM] / acc[HEAD_DIM:HEAD_DIM + 1]
        o_ref[:, hh * HEAD_DIM:(hh + 1) * HEAD_DIM] = out.T.astype(o_ref.dtype)


def gqa_attention(qt, k, vt, seq, tq=512, tk=512):
    nh = B_Q_HEADS // B_KV_HEADS
    return pl.pallas_call(
        functools.partial(_gqa_kernel, seq=seq, tk=tk, nh=nh),
        grid=(B_KV_HEADS, seq // tq),
        in_specs=[pl.BlockSpec((nh * HEAD_DIM, tq), lambda g, i: (g, i)),
                  pl.BlockSpec((seq, HEAD_DIM), lambda g, i: (0, g)),
                  pl.BlockSpec((HEAD_DIM, seq), lambda g, i: (g, 0))],
        out_specs=pl.BlockSpec((tq, nh * HEAD_DIM), lambda g, i: (i, g)),
        out_shape=jax.ShapeDtypeStruct((seq, B_Q_HEADS * HEAD_DIM), BF16),
        scratch_shapes=[pltpu.VMEM((nh, 1, tq), F32),
                        pltpu.VMEM((nh, HEAD_DIM + BF16_ROWS, tq), F32),
                        pltpu.VMEM((tk, tq), F32),
                        pltpu.VMEM((2, tk, HEAD_DIM), BF16)],
        compiler_params=_params(("parallel", "arbitrary")),
        name="gqa_attention",
    )(qt, k, vt)


def _outproj_kernel(ma_ref, mb_ref, wa_ref, wb_ref, x_ref, g_ref, x1_ref, ht_ref):
    y = x_ref[...]
    y = y + jnp.dot(ma_ref[...], wa_ref[...], preferred_element_type=F32)
    y = y + jnp.dot(mb_ref[...], wb_ref[...], preferred_element_type=F32)
    x1_ref[...] = y
    ht_ref[...] = _rms(y, g_ref[...]).T.astype(BF16)


def out_projection(mixed_a, mixed_b, w_a, w_b, x, g, tm=512):
    s, d = x.shape
    wa, wb = mixed_a.shape[1], mixed_b.shape[1]
    return pl.pallas_call(
        _outproj_kernel,
        grid=(s // tm,),
        in_specs=[pl.BlockSpec((tm, wa), lambda i: (i, 0)),
                  pl.BlockSpec((tm, wb), lambda i: (i, 0)),
                  pl.BlockSpec((wa, d), lambda i: (0, 0)),
                  pl.BlockSpec((wb, d), lambda i: (0, 0)),
                  pl.BlockSpec((tm, d), lambda i: (i, 0)),
                  pl.BlockSpec((1, d), lambda i: (0, 0))],
        out_specs=[pl.BlockSpec((tm, d), lambda i: (i, 0)),
                   pl.BlockSpec((d, tm), lambda i: (0, i))],
        out_shape=[jax.ShapeDtypeStruct((s, d), F32),
                   jax.ShapeDtypeStruct((d, s), BF16)],
        compiler_params=_params(("parallel",)),
        name="out_projection",
    )(mixed_a, mixed_b, w_a, w_b, x, g.reshape(1, d))


N_RANK = PEER_TOPK + 1
RANK_ROWS = 24
CAND_COUNTS = tuple(N_RANK // (a + 1) for a in range(N_RANK))
CAND_ROWS = 8 * (1 << (-(-sum(CAND_COUNTS) // 8) - 1).bit_length())
ROUTER_UNROLL = 8


def _odd_even_merge_sort_pairs(n):
    pairs = []

    def merge(lo, hi, r):
        step = r * 2
        if step < hi - lo:
            merge(lo, hi, step)
            merge(lo + r, hi, step)
            pairs.extend((i, i + r) for i in range(lo + r, hi - r, step))
        else:
            pairs.append((lo, lo + r))

    def sort(lo, hi):
        if hi - lo >= 1:
            mid = lo + (hi - lo) // 2
            sort(lo, mid)
            sort(mid + 1, hi)
            merge(lo, hi, 1)

    sort(0, n - 1)
    return pairs


def _extract_sorted_tile(tile, n, emit):
    depth = tile.shape[0] // SUBLANES
    lists = [tile[SUBLANES * j:SUBLANES * (j + 1)] for j in range(depth)]
    for i, j in _odd_even_merge_sort_pairs(depth):
        lists[i], lists[j] = jnp.maximum(lists[i], lists[j]), jnp.minimum(lists[i], lists[j])
    for r in range(n):
        mj = jnp.max(lists[0], axis=0, keepdims=True)
        emit(r, mj)
        took = lists[0] == mj
        for j in range(min(depth, n - (r + 1))):
            below = lists[j + 1] if j + 1 < depth else -jnp.inf
            lists[j] = jnp.where(took, below, lists[j])


def _router_kernel(ht_ref, wqt_ref, keys_ref, thr_ref, w0_ref, s1_ref, e1_ref,
                   q_scr, s_scr, top_scr, cand_scr, best_scr, *, tb):
    q_scr[...] = jnp.dot(wqt_ref[...], ht_ref[...], preferred_element_type=F32).astype(BF16)
    lane_groups = [slice(lg * LANES, (lg + 1) * LANES) for lg in range(tb // LANES)]

    def half_body(hc):
        qhc = q_scr[pl.ds(pl.multiple_of(hc * PEER_N_KEYS, PEER_N_KEYS), PEER_N_KEYS), :]
        s = jnp.dot(keys_ref[hc], qhc, preferred_element_type=F32)
        s_scr[hc] = s
        for ls in lane_groups:
            def emit(j, mj, ls=ls):
                top_scr[hc, pl.ds(j, 1), ls] = mj

            _extract_sorted_tile(s[:, ls], N_RANK, emit)

    def half_group(i, carry):
        for u in range(ROUTER_UNROLL):
            half_body(i * ROUTER_UNROLL + u)
        return carry

    lax.fori_loop(0, 2 * PEER_HEADS // ROUTER_UNROLL, half_group, 0)

    def head_body(h, carry):
        t0 = top_scr[2 * h]
        t1 = top_scr[2 * h + 1]
        cand_scr[...] = jnp.full(cand_scr.shape, -jnp.inf, F32)
        off = 0
        for a, nb in enumerate(CAND_COUNTS):
            cand_scr[off:off + nb, :] = t0[a:a + 1, :] + t1[0:nb, :]
            off += nb

        for ls in lane_groups:
            def emit(j, mj, ls=ls):
                best_scr[pl.ds(j, 1), ls] = mj

            _extract_sorted_tile(cand_scr[:, ls], N_RANK, emit)
        best = [best_scr[j:j + 1, :] for j in range(N_RANK)]
        theta = 0.5 * (best[PEER_TOPK - 1] + best[PEER_TOPK])
        z = jnp.zeros_like(theta)
        for j in range(PEER_TOPK):
            z = z + jnp.exp(best[j] - best[0])

        s0 = s_scr[2 * h]
        s1 = s_scr[2 * h + 1]
        thr_ref[h] = theta - s0
        w0_ref[h] = jnp.exp(s0 - t0[0:1, :]) * (0.5 / z)
        e1 = jnp.exp(s1 - t1[0:1, :])
        for lg, ls in enumerate(lane_groups):
            s1_ref[h, lg] = s1[:, ls]
            e1_ref[h, lg] = e1[:, ls]
        return carry

    lax.fori_loop(0, PEER_HEADS, head_body, 0)


def peer_router(ht, wqt, keys, tb=512):
    d, s = ht.shape
    nq = wqt.shape[0]
    tab = jax.ShapeDtypeStruct((PEER_HEADS, PEER_N_KEYS, s), F32)
    tab_spec = pl.BlockSpec((PEER_HEADS, PEER_N_KEYS, tb), lambda i: (0, 0, i))
    tile = jax.ShapeDtypeStruct((PEER_HEADS, s // LANES, PEER_N_KEYS, LANES), F32)
    tile_spec = pl.BlockSpec((PEER_HEADS, tb // LANES, PEER_N_KEYS, LANES), lambda i: (0, i, 0, 0))
    return pl.pallas_call(
        functools.partial(_router_kernel, tb=tb),
        grid=(s // tb,),
        in_specs=[pl.BlockSpec((d, tb), lambda i: (0, i)),
                  pl.BlockSpec((nq, d), lambda i: (0, 0)),
                  pl.BlockSpec(keys.shape, lambda i: (0, 0, 0))],
        out_specs=[tab_spec, tab_spec, tile_spec, tile_spec],
        out_shape=[tab, tab, tile, tile],
        scratch_shapes=[pltpu.VMEM((nq, tb), BF16),
                        pltpu.VMEM((2 * PEER_HEADS, PEER_N_KEYS, tb), F32),
                        pltpu.VMEM((2 * PEER_HEADS, RANK_ROWS, tb), F32),
                        pltpu.VMEM((CAND_ROWS, tb), F32),
                        pltpu.VMEM((RANK_ROWS, tb), F32)],
        compiler_params=_params(("parallel",)),
        name="peer_router",
    )(ht, wqt, keys)


GELU_C = 0.7978845608028654


def _gelu_tanh_x2(x):
    u = x * (GELU_C + (GELU_C * 0.044715) * (x * x))
    return x + x * jnp.tanh(u)


PEER_SUB = 256
GATE_ROWS = 32


def _peer_stage(ht_ref, u_ref, v_ref, thr_ref, w0_ref, s1_ref, e1_ref, o_ref,
                at_w, at_r, pt_w, pt_r, *, te, tb):
    d = o_ref.shape[1]
    n_sub = te // PEER_SUB
    col_w = d // n_sub

    def hidden_piece(j):
        rows = slice(j * PEER_SUB, (j + 1) * PEER_SUB)
        at = jnp.dot(u_ref[rows, :], ht_ref[...], preferred_element_type=F32)
        for lg in range(tb // LANES):
            at_w[lg, rows, :] = at[:, lg * LANES:(lg + 1) * LANES]

    def value_piece(c):
        cols = slice(c * col_w, (c + 1) * col_w)
        o_ref[:, cols] += lax.dot_general(pt_r[...], v_ref[:, cols], (((0,), (0,)), ((), ())),
                                          preferred_element_type=F32)

    n_i0 = te // PEER_N_KEYS
    n_q = PEER_N_KEYS // GATE_ROWS

    def gate_piece(lg, q):
        ls = slice(lg * LANES, (lg + 1) * LANES)
        qs = slice(q * GATE_ROWS, (q + 1) * GATE_ROWS)
        gates = [None] * n_i0
        for h in range(PEER_HEADS):
            s1 = s1_ref[h, lg, qs, :]
            e1 = e1_ref[h, lg, qs, :]
            for i0 in range(n_i0):
                thr = thr_ref[h, i0:i0 + 1, ls]
                w0 = w0_ref[h, i0:i0 + 1, ls]
                term = jnp.where(s1 >= thr, e1 * w0, 0.0)
                gates[i0] = term if gates[i0] is None else gates[i0] + term
        for i0 in range(n_i0):
            er = slice(i0 * PEER_N_KEYS + q * GATE_ROWS, i0 * PEER_N_KEYS + (q + 1) * GATE_ROWS)
            pt_w[er, ls] = (_gelu_tanh_x2(at_r[lg, er, :]) * gates[i0]).astype(BF16)

    mxu_pieces = [functools.partial(hidden_piece, j) for j in range(n_sub)]
    mxu_pieces += [functools.partial(value_piece, c) for c in range(n_sub)]
    vec_pieces = [(lg, q) for lg in range(tb // LANES) for q in range(n_q)]
    assert len(vec_pieces) % len(mxu_pieces) == 0
    per_mxu = len(vec_pieces) // len(mxu_pieces)
    for n, piece in enumerate(mxu_pieces):
        for lg, q in vec_pieces[n * per_mxu:(n + 1) * per_mxu]:
            gate_piece(lg, q)
        piece()


def _peer_kernel(ht_ref, u_ref, v_ref, thr_ref, w0_ref, s1_ref, e1_ref, x1_ref, g_ref,
                 o_ref, at0_ref, at1_ref, pt0_ref, pt1_ref, *, te, tb, n_e):
    g = pl.program_id(0)
    e_out = (g - 2) % n_e
    out_live = g >= 2
    at_refs = (at0_ref, at1_ref)
    pt_refs = (pt0_ref, pt1_ref)

    @pl.when(g == 0)
    def _():
        for r in at_refs + pt_refs:
            r[...] = jnp.zeros(r.shape, r.dtype)

    @pl.when((g == 0) | (out_live & (e_out == 0)))
    def _():
        o_ref[...] = x1_ref[...]

    for parity in range(2):
        @pl.when(g % 2 == parity)
        def _(parity=parity):
            _peer_stage(ht_ref, u_ref, v_ref, thr_ref, w0_ref, s1_ref, e1_ref, o_ref,
                        at_refs[parity], at_refs[1 - parity],
                        pt_refs[1 - parity], pt_refs[parity], te=te, tb=tb)

    @pl.when(out_live & (e_out == n_e - 1))
    def _():
        o_ref[...] = _rms(o_ref[...], g_ref[...])


def peer_experts(ht, u, v, tables, x1, g, tb=512, te=1024):
    d, s = ht.shape
    n_exp = u.shape[0]
    n_i0 = te // PEER_N_KEYS
    n_e = n_exp // te
    n_items = (s // tb) * n_e
    thr, w0, s1, e1 = tables

    def item(gi, lag):
        n = jnp.clip(gi - lag, 0, n_items - 1)
        return n // n_e, n % n_e

    row_spec = pl.BlockSpec((PEER_HEADS, n_i0, tb), lambda gi: (0, item(gi, 1)[1], item(gi, 1)[0]))
    tab_spec = pl.BlockSpec((PEER_HEADS, tb // LANES, PEER_N_KEYS, LANES),
                            lambda gi: (0, item(gi, 1)[0], 0, 0))
    return pl.pallas_call(
        functools.partial(_peer_kernel, te=te, tb=tb, n_e=n_e),
        grid=(n_items + 2,),
        in_specs=[pl.BlockSpec((d, tb), lambda gi: (0, item(gi, 0)[0])),
                  pl.BlockSpec((te, d), lambda gi: (item(gi, 0)[1], 0)),
                  pl.BlockSpec((te, d), lambda gi: (item(gi, 2)[1], 0)),
                  row_spec, row_spec, tab_spec, tab_spec,
                  pl.BlockSpec((tb, d), lambda gi: (item(gi, 2)[0], 0),
                               pipeline_mode=pl.Buffered(1)),
                  pl.BlockSpec((1, d), lambda gi: (0, 0))],
        out_specs=pl.BlockSpec((tb, d), lambda gi: (item(gi, 2)[0], 0)),
        out_shape=jax.ShapeDtypeStruct((s, d), F32),
        scratch_shapes=[pltpu.VMEM((tb // LANES, te, LANES), F32),
                        pltpu.VMEM((tb // LANES, te, LANES), F32),
                        pltpu.VMEM((te, tb), BF16),
                        pltpu.VMEM((te, tb), BF16)],
        compiler_params=_params(("arbitrary",)),
        name="peer_experts",
    )(ht, u, v, thr, w0, s1, e1, x1, g.reshape(1, d))


def kernel(x, norm1_g, w_in, q_norm_g, k_norm_g, w_out, norm2_g, peer_w_query,
           peer_sub_keys, peer_u, peer_v, final_norm_g):
    b, s, d = x.shape
    assert b == 1
    a_width = A_HEADS * HEAD_DIM
    depth = w_in.shape[0]
    xs = x.reshape(s, d)
    for l in range(depth):
        proj = in_projection(xs, norm1_g[l], w_in[l].astype(BF16))
        mixed_a = dilated_attention(proj, s)
        qt, kb, vt = gqa_prep(proj, q_norm_g[l], k_norm_g[l], s, col_block=2)
        mixed_b = gqa_attention(qt, kb, vt, s)
        w_o = w_out[l].astype(BF16)
        x1, ht = out_projection(mixed_a, mixed_b, w_o[:a_width], w_o[a_width:], xs, norm2_g[l])
        keys = peer_sub_keys[l].reshape(2 * PEER_HEADS, PEER_N_KEYS, -1).astype(BF16)
        tables = peer_router(ht, peer_w_query[l].T.astype(BF16), keys)
        assert l == depth - 1
        xs = peer_experts(ht, peer_u[l].astype(BF16), peer_v[l].astype(BF16), tables, x1,
                          final_norm_g)
    return xs.reshape(b, s, d)
```

```python
import functools

import numpy as np
import jax
import jax.numpy as jnp
from jax import lax
from jax.experimental import pallas as pl
from jax.experimental.pallas import tpu as pltpu

F32 = jnp.float32
BF16 = jnp.bfloat16

HEAD_DIM = 128
A_HEADS = 8
DIL_PATTERNS = ((128, 1), (512, 4), (2048, 16))
B_Q_HEADS = 8
B_KV_HEADS = 2
GRID_W = 64
ROPE_THETA = 10000.0
NORM_EPS = 1e-6
PEER_HEADS = 8
PEER_N_KEYS = 128
PEER_TOPK = 16

LANES = 128
SUBLANES = 8
BF16_ROWS = 16
NEG_BIG = -1e30
LOG2_E = 1.4426950408889634
VMEM_LIMIT = 56 * 1024 * 1024


def _params(sem, vmem=VMEM_LIMIT):
    return pltpu.CompilerParams(dimension_semantics=sem, vmem_limit_bytes=vmem)


def _rms(x, g):
    ms = jnp.mean(x * x, axis=-1, keepdims=True)
    return x * lax.rsqrt(ms + NORM_EPS) * g


def _inproj_kernel(x_ref, g_ref, w_ref, o_ref, h_ref):
    @pl.when(pl.program_id(1) == 0)
    def _():
        h_ref[...] = _rms(x_ref[...], g_ref[...]).astype(BF16)

    o_ref[...] = jnp.dot(h_ref[...], w_ref[...], preferred_element_type=F32)


def in_projection(x, g, w_bf16, tm=1024, tn=512):
    s, d = x.shape
    n = w_bf16.shape[1]
    return pl.pallas_call(
        _inproj_kernel,
        grid=(s // tm, n // tn),
        in_specs=[pl.BlockSpec((tm, d), lambda i, j: (i, 0)),
                  pl.BlockSpec((1, d), lambda i, j: (0, 0)),
                  pl.BlockSpec((d, tn), lambda i, j: (0, j))],
        out_specs=pl.BlockSpec((tm, tn), lambda i, j: (i, j)),
        out_shape=jax.ShapeDtypeStruct((s, n), F32),
        scratch_shapes=[pltpu.VMEM((tm, d), BF16)],
        compiler_params=_params(("parallel", "arbitrary")),
        name="in_projection",
    )(x, g.reshape(1, d), w_bf16)


DIL_QBLK = 2048
DIL_SUB = 128
DIL_KWIN = 256
DIL_UNROLL = 16
DIL_EDGE_CASES = 3


def _dilated_kernel(q_ref, k_ref, v_ref, *rest, seq, n_cast):
    cast_src, rest = rest[:n_cast], rest[n_cast:]
    o_ref, cast_dst, rest = rest[0], rest[1:1 + n_cast], rest[1 + n_cast:]
    m_ref, l_ref, acc_ref, bias_ref = rest
    for src, dst in zip(cast_src, cast_dst):
        dst[...] = src[...].astype(dst.dtype)

    h = pl.program_id(0)
    base = pl.program_id(1) * DIL_QBLK
    slope = jnp.exp2(-(h + 1).astype(F32))
    scale = HEAD_DIM ** -0.5

    m_ref[...] = jnp.full(m_ref.shape, NEG_BIG, F32)
    l_ref[...] = jnp.zeros(l_ref.shape, F32)
    acc_ref[...] = jnp.zeros(acc_ref.shape, F32)

    row = lax.broadcasted_iota(jnp.int32, (DIL_SUB, DIL_KWIN), 0)
    col = lax.broadcasted_iota(jnp.int32, (DIL_SUB, DIL_KWIN), 1)
    col_minus_row = col - row

    for b_idx, (w, d) in enumerate(DIL_PATTERNS):
        half = (w // 2) // d
        assert half * 2 + DIL_SUB == DIL_KWIN
        seg_len = seq // d

        for v_idx in range(DIL_EDGE_CASES):
            rel = jnp.abs(col_minus_row - v_idx * half)
            bias_ref[b_idx * DIL_EDGE_CASES + v_idx] = jnp.where(
                rel <= half, -(slope * d) * rel.astype(F32), NEG_BIG)

        def rows(start, n, d=d):
            return pl.ds(start, n, stride=d) if d > 1 else pl.ds(start, n)

        def body(c, carry, d=d, half=half, seg_len=seg_len, rows=rows, b_idx=b_idx):
            r = c % d
            cc = c // d
            p0 = base // d + cc * DIL_SUB
            kp = jnp.clip(p0 - half, 0, seg_len - DIL_KWIN)
            lq = r + d * (cc * DIL_SUB)
            ks = r + d * kp

            q = (q_ref[rows(lq, DIL_SUB), :] * scale).astype(BF16)
            k = k_ref[rows(ks, DIL_KWIN), :].astype(BF16)
            v = v_ref[rows(ks, DIL_KWIN), :].astype(BF16)
            s = lax.dot_general(q, k, (((1,), (1,)), ((), ())), preferred_element_type=F32)
            s = s + bias_ref[b_idx * DIL_EDGE_CASES + (p0 - kp) // half]
            mb = jnp.max(s, axis=1, keepdims=True)
            p = jnp.exp(s - mb)
            lb = jnp.sum(p, axis=1, keepdims=True)
            ob = jnp.dot(p.astype(BF16), v, preferred_element_type=F32)

            idx = rows(lq, DIL_SUB)
            m_old = m_ref[idx, :]
            m_new = jnp.maximum(m_old, mb)
            a_old = jnp.exp(m_old - m_new)
            a_blk = jnp.exp(mb - m_new)
            m_ref[idx, :] = m_new
            l_ref[idx, :] = a_old * l_ref[idx, :] + a_blk * lb
            acc_ref[idx, :] = a_old * acc_ref[idx, :] + a_blk * ob
            return carry

        def group(cg, carry, body=body):
            for u in range(DIL_UNROLL):
                body(cg * DIL_UNROLL + u, carry)
            return carry

        lax.fori_loop(0, DIL_QBLK // DIL_SUB // DIL_UNROLL, group, 0)

    o_ref[...] = (acc_ref[...] / l_ref[...]).astype(o_ref.dtype)


def dilated_attention(proj, seq, to_bf16=()):
    assert seq % DIL_QBLK == 0
    for _, d in DIL_PATTERNS:
        assert DIL_QBLK // d >= DIL_SUB and seq // d >= DIL_KWIN
    nh = A_HEADS
    n_qb = seq // DIL_QBLK
    n_steps = nh * n_qb
    cast_specs = []
    for w in to_bf16:
        assert w.shape[0] % n_steps == 0
        cast_specs.append(pl.BlockSpec((w.shape[0] // n_steps, w.shape[1]),
                                       lambda h, i: (h * n_qb + i, 0)))
    outs = pl.pallas_call(
        functools.partial(_dilated_kernel, seq=seq, n_cast=len(to_bf16)),
        grid=(nh, n_qb),
        in_specs=[pl.BlockSpec((DIL_QBLK, HEAD_DIM), lambda h, i: (i, h)),
                  pl.BlockSpec((seq, HEAD_DIM), lambda h, i: (0, nh + h)),
                  pl.BlockSpec((seq, HEAD_DIM), lambda h, i: (0, 2 * nh + h))] + cast_specs,
        out_specs=[pl.BlockSpec((DIL_QBLK, HEAD_DIM), lambda h, i: (i, h))] + cast_specs,
        out_shape=[jax.ShapeDtypeStruct((seq, nh * HEAD_DIM), BF16)]
        + [jax.ShapeDtypeStruct(w.shape, BF16) for w in to_bf16],
        scratch_shapes=[pltpu.VMEM((DIL_QBLK, LANES), F32),
                        pltpu.VMEM((DIL_QBLK, LANES), F32),
                        pltpu.VMEM((DIL_QBLK, HEAD_DIM), F32),
                        pltpu.VMEM((len(DIL_PATTERNS) * DIL_EDGE_CASES, DIL_SUB, DIL_KWIN), F32)],
        compiler_params=_params(("parallel", "arbitrary")),
        name="dilated_attention",
    )(proj, proj, proj, *to_bf16)
    return outs[0], outs[1:]


def _rope_tables(seq):
    rows = seq // GRID_W
    row = jnp.repeat(jnp.arange(rows, dtype=F32), GRID_W)
    col = jnp.tile(jnp.arange(GRID_W, dtype=F32), rows)
    half = HEAD_DIM // 2
    inv = ROPE_THETA ** (-jnp.arange(0, half, 2, dtype=F32) / half)
    ang = jnp.concatenate([row[:, None] * inv, col[:, None] * inv], axis=-1)
    cos = jnp.repeat(jnp.cos(ang), 2, axis=-1)
    sign = jnp.tile(jnp.asarray([-1.0, 1.0], F32), half)
    sin = jnp.repeat(jnp.sin(ang), 2, axis=-1) * sign
    return cos, sin


def _gqa_prep_kernel(p_ref, cos_ref, sin_ref, qg_ref, kg_ref, qt_ref, k_ref, vt_ref):
    cos = cos_ref[...]
    sin = sin_ref[...]
    even = (lax.broadcasted_iota(jnp.int32, cos.shape, 1) % 2) == 0

    def norm_rope(xh, g):
        y = _rms(xh, g)
        partner = jnp.where(even, pltpu.roll(y, LANES - 1, 1), pltpu.roll(y, 1, 1))
        return y * cos + partner * sin

    scale = HEAD_DIM ** -0.5 * LOG2_E
    for hq in range(B_Q_HEADS):
        xh = p_ref[:, hq * HEAD_DIM:(hq + 1) * HEAD_DIM]
        y = norm_rope(xh, qg_ref[...]) * scale
        qt_ref[hq * HEAD_DIM:(hq + 1) * HEAD_DIM, :] = y.T.astype(BF16)
    k0 = B_Q_HEADS * HEAD_DIM
    v0 = k0 + B_KV_HEADS * HEAD_DIM
    for hk in range(B_KV_HEADS):
        xh = p_ref[:, k0 + hk * HEAD_DIM:k0 + (hk + 1) * HEAD_DIM]
        k_ref[:, hk * HEAD_DIM:(hk + 1) * HEAD_DIM] = norm_rope(xh, kg_ref[...]).astype(BF16)
        vh = p_ref[:, v0 + hk * HEAD_DIM:v0 + (hk + 1) * HEAD_DIM]
        vt_ref[hk * HEAD_DIM:(hk + 1) * HEAD_DIM, :] = vh.T.astype(BF16)


def gqa_prep(proj, q_g, k_g, seq, col_block, ts=512):
    wb = (B_Q_HEADS + 2 * B_KV_HEADS) * HEAD_DIM
    cos, sin = _rope_tables(seq)
    return pl.pallas_call(
        _gqa_prep_kernel,
        grid=(seq // ts,),
        in_specs=[pl.BlockSpec((ts, wb), lambda i: (i, col_block)),
                  pl.BlockSpec((ts, HEAD_DIM), lambda i: (i, 0)),
                  pl.BlockSpec((ts, HEAD_DIM), lambda i: (i, 0)),
                  pl.BlockSpec((1, HEAD_DIM), lambda i: (0, 0)),
                  pl.BlockSpec((1, HEAD_DIM), lambda i: (0, 0))],
        out_specs=[pl.BlockSpec((B_Q_HEADS * HEAD_DIM, ts), lambda i: (0, i)),
                   pl.BlockSpec((ts, B_KV_HEADS * HEAD_DIM), lambda i: (i, 0)),
                   pl.BlockSpec((B_KV_HEADS * HEAD_DIM, ts), lambda i: (0, i))],
        out_shape=[jax.ShapeDtypeStruct((B_Q_HEADS * HEAD_DIM, seq), BF16),
                   jax.ShapeDtypeStruct((seq, B_KV_HEADS * HEAD_DIM), BF16),
                   jax.ShapeDtypeStruct((B_KV_HEADS * HEAD_DIM, seq), BF16)],
        compiler_params=_params(("parallel",)),
        name="gqa_prep",
    )(proj, cos, sin, q_g.reshape(1, HEAD_DIM), k_g.reshape(1, HEAD_DIM))


STAGE_ROWS = 32
GQA_UNROLL = 1


def _data_dependent_zero(tiles):
    bits = None
    for t in tiles:
        b = pltpu.bitcast(t, jnp.uint32)
        bits = b if bits is None else bits | b
    zero_words = (bits >> 16) >> 16
    return zero_words.astype(jnp.int32).astype(F32).astype(BF16)


def _gqa_kernel(qt_ref, k_ref, vt_ref, o_ref, m_ref, acc_ref, st_ref, ks_ref, *, seq, tk, nh):
    n_chunks = seq // tk
    tq = qt_ref.shape[1]
    m_ref[...] = jnp.full(m_ref.shape, NEG_BIG, F32)
    acc_ref[...] = jnp.zeros(acc_ref.shape, F32)

    def q_head(hh):
        return qt_ref[hh * HEAD_DIM:(hh + 1) * HEAD_DIM, :]

    def chunk_start(c):
        return pl.multiple_of(jnp.minimum(c, n_chunks - 1) * tk, tk)

    st_ref[...] = jnp.dot(k_ref[pl.ds(0, tk), :], q_head(0), preferred_element_type=F32)

    def chunk_items(c, st):
        here = chunk_start(c)
        vt = jnp.concatenate([vt_ref[:, pl.ds(here, tk)], jnp.ones((BF16_ROWS, tk), BF16)], axis=0)
        for hh in range(nh):
            m_old = m_ref[hh]
            m_new = jnp.maximum(m_old, jnp.max(st, axis=0, keepdims=True))
            alpha = jnp.exp2(m_old - m_new)
            p = jnp.exp2(st - m_new)
            pb = p.astype(BF16)

            nxt = here if hh + 1 < nh else chunk_start(c + 1)
            slot = ks_ref.at[hh % 2]
            for r in range(0, tk, STAGE_ROWS):
                strip = [pb[r:r + STAGE_ROWS, j:j + LANES] for j in range(0, tq, LANES)]
                zero = _data_dependent_zero(strip)
                piece = k_ref[pl.ds(pl.multiple_of(nxt + r, STAGE_ROWS), STAGE_ROWS), :]
                slot[r:r + STAGE_ROWS, :] = piece + pltpu.repeat(zero, STAGE_ROWS // zero.shape[0], axis=0)
            st = jnp.dot(slot[...], q_head((hh + 1) % nh), preferred_element_type=F32)

            acc_ref[hh] = alpha * acc_ref[hh] + jnp.dot(vt, pb, preferred_element_type=F32)
            m_ref[hh] = m_new
        return st

    def body(i, carry):
        st = st_ref[...]
        for u in range(GQA_UNROLL):
            st = chunk_items(i * GQA_UNROLL + u, st)
        st_ref[...] = st
        return carry

    lax.fori_loop(0, n_chunks // GQA_UNROLL, body, 0)
    for hh in range(nh):
        acc = acc_ref[hh]
        out = acc[:HEAD_DIM] / acc[HEAD_DIM:HEAD_DIM + 1]
        o_ref[:, hh * HEAD_DIM:(hh + 1) * HEAD_DIM] = out.T.astype(o_ref.dtype)


def gqa_attention(qt, k, vt, seq, tq=512, tk=512):
    nh = B_Q_HEADS // B_KV_HEADS
    return pl.pallas_call(
        functools.partial(_gqa_kernel, seq=seq, tk=tk, nh=nh),
        grid=(B_KV_HEADS, seq // tq),
        in_specs=[pl.BlockSpec((nh * HEAD_DIM, tq), lambda g, i: (g, i)),
                  pl.BlockSpec((seq, HEAD_DIM), lambda g, i: (0, g)),
                  pl.BlockSpec((HEAD_DIM, seq), lambda g, i: (g, 0))],
        out_specs=pl.BlockSpec((tq, nh * HEAD_DIM), lambda g, i: (i, g)),
        out_shape=jax.ShapeDtypeStruct((seq, B_Q_HEADS * HEAD_DIM), BF16),
        scratch_shapes=[pltpu.VMEM((nh, 1, tq), F32),
                        pltpu.VMEM((nh, HEAD_DIM + BF16_ROWS, tq), F32),
                        pltpu.VMEM((tk, tq), F32),
                        pltpu.VMEM((2, tk, HEAD_DIM), BF16)],
        compiler_params=_params(("parallel", "arbitrary")),
        name="gqa_attention",
    )(qt, k, vt)


def _outproj_kernel(ma_ref, mb_ref, wa_ref, wb_ref, x_ref, g_ref, x1_ref, ht_ref):
    y = x_ref[...]
    y = y + jnp.dot(ma_ref[...], wa_ref[...], preferred_element_type=F32)
    y = y + jnp.dot(mb_ref[...], wb_ref[...], preferred_element_type=F32)
    x1_ref[...] = y
    ht_ref[...] = _rms(y, g_ref[...]).T.astype(BF16)


def out_projection(mixed_a, mixed_b, w_a, w_b, x, g, tm=512):
    s, d = x.shape
    wa, wb = mixed_a.shape[1], mixed_b.shape[1]
    return pl.pallas_call(
        _outproj_kernel,
        grid=(s // tm,),
        in_specs=[pl.BlockSpec((tm, wa), lambda i: (i, 0)),
                  pl.BlockSpec((tm, wb), lambda i: (i, 0)),
                  pl.BlockSpec((wa, d), lambda i: (0, 0)),
                  pl.BlockSpec((wb, d), lambda i: (0, 0)),
                  pl.BlockSpec((tm, d), lambda i: (i, 0)),
                  pl.BlockSpec((1, d), lambda i: (0, 0))],
        out_specs=[pl.BlockSpec((tm, d), lambda i: (i, 0)),
                   pl.BlockSpec((d, tm), lambda i: (0, i))],
        out_shape=[jax.ShapeDtypeStruct((s, d), F32),
                   jax.ShapeDtypeStruct((d, s), BF16)],
        compiler_params=_params(("parallel",)),
        name="out_projection",
    )(mixed_a, mixed_b, w_a, w_b, x, g.reshape(1, d))


N_RANK = PEER_TOPK + 1
RANK_ROWS = 24
CAND_COUNTS = tuple(N_RANK // (a + 1) for a in range(N_RANK))
CAND_ROWS = 8 * (1 << (-(-sum(CAND_COUNTS) // 8) - 1).bit_length())
ROUTER_UNROLL = 8


def _odd_even_merge_sort_pairs(n):
    pairs = []

    def merge(lo, hi, r):
        step = r * 2
        if step < hi - lo:
            merge(lo, hi, step)
            merge(lo + r, hi, step)
            pairs.extend((i, i + r) for i in range(lo + r, hi - r, step))
        else:
            pairs.append((lo, lo + r))

    def sort(lo, hi):
        if hi - lo >= 1:
            mid = lo + (hi - lo) // 2
            sort(lo, mid)
            sort(mid + 1, hi)
            merge(lo, hi, 1)

    sort(0, n - 1)
    return pairs


def _extract_sorted_tile(tile, n, emit):
    depth = tile.shape[0] // SUBLANES
    lists = [tile[SUBLANES * j:SUBLANES * (j + 1)] for j in range(depth)]
    for i, j in _odd_even_merge_sort_pairs(depth):
        lists[i], lists[j] = jnp.maximum(lists[i], lists[j]), jnp.minimum(lists[i], lists[j])
    for r in range(n):
        mj = jnp.max(lists[0], axis=0, keepdims=True)
        emit(r, mj)
        took = lists[0] == mj
        for j in range(min(depth, n - (r + 1))):
            below = lists[j + 1] if j + 1 < depth else -jnp.inf
            lists[j] = jnp.where(took, below, lists[j])


def _router_kernel(ht_ref, wqt_ref, keys_ref, thr_ref, w0_ref, s1_ref, e1_ref,
                   q_scr, s_scr, top_scr, cand_scr, best_scr, *, tb):
    q_scr[...] = jnp.dot(wqt_ref[...], ht_ref[...], preferred_element_type=F32).astype(BF16)
    lane_groups = [slice(lg * LANES, (lg + 1) * LANES) for lg in range(tb // LANES)]

    def half_body(hc):
        qhc = q_scr[pl.ds(pl.multiple_of(hc * PEER_N_KEYS, PEER_N_KEYS), PEER_N_KEYS), :]
        s = jnp.dot(keys_ref[hc], qhc, preferred_element_type=F32)
        s_scr[hc] = s
        for ls in lane_groups:
            def emit(j, mj, ls=ls):
                top_scr[hc, pl.ds(j, 1), ls] = mj

            _extract_sorted_tile(s[:, ls], N_RANK, emit)

    def half_group(i, carry):
        for u in range(ROUTER_UNROLL):
            half_body(i * ROUTER_UNROLL + u)
        return carry

    lax.fori_loop(0, 2 * PEER_HEADS // ROUTER_UNROLL, half_group, 0)

    def head_body(h, carry):
        t0 = top_scr[2 * h]
        t1 = top_scr[2 * h + 1]
        cand_scr[...] = jnp.full(cand_scr.shape, -jnp.inf, F32)
        off = 0
        for a, nb in enumerate(CAND_COUNTS):
            cand_scr[off:off + nb, :] = t0[a:a + 1, :] + t1[0:nb, :]
            off += nb

        for ls in lane_groups:
            def emit(j, mj, ls=ls):
                best_scr[pl.ds(j, 1), ls] = mj

            _extract_sorted_tile(cand_scr[:, ls], N_RANK, emit)
        best = [best_scr[j:j + 1, :] for j in range(N_RANK)]
        theta = 0.5 * (best[PEER_TOPK - 1] + best[PEER_TOPK])
        z = jnp.zeros_like(theta)
        for j in range(PEER_TOPK):
            z = z + jnp.exp(best[j] - best[0])

        s0 = s_scr[2 * h]
        s1 = s_scr[2 * h + 1]
        thr_ref[h] = theta - s0
        w0_ref[h] = jnp.exp(s0 - t0[0:1, :]) * (0.5 / z)
        e1 = jnp.exp(s1 - t1[0:1, :])
        for lg, ls in enumerate(lane_groups):
            s1_ref[h, lg] = s1[:, ls]
            e1_ref[h, lg] = e1[:, ls]
        return carry

    lax.fori_loop(0, PEER_HEADS, head_body, 0)


def peer_router(ht, wqt, keys, tb=512):
    d, s = ht.shape
    nq = wqt.shape[0]
    tab = jax.ShapeDtypeStruct((PEER_HEADS, PEER_N_KEYS, s), F32)
    tab_spec = pl.BlockSpec((PEER_HEADS, PEER_N_KEYS, tb), lambda i: (0, 0, i))
    tile = jax.ShapeDtypeStruct((PEER_HEADS, s // LANES, PEER_N_KEYS, LANES), F32)
    tile_spec = pl.BlockSpec((PEER_HEADS, tb // LANES, PEER_N_KEYS, LANES), lambda i: (0, i, 0, 0))
    return pl.pallas_call(
        functools.partial(_router_kernel, tb=tb),
        grid=(s // tb,),
        in_specs=[pl.BlockSpec((d, tb), lambda i: (0, i)),
                  pl.BlockSpec((nq, d), lambda i: (0, 0)),
                  pl.BlockSpec(keys.shape, lambda i: (0, 0, 0))],
        out_specs=[tab_spec, tab_spec, tile_spec, tile_spec],
        out_shape=[tab, tab, tile, tile],
        scratch_shapes=[pltpu.VMEM((nq, tb), BF16),
                        pltpu.VMEM((2 * PEER_HEADS, PEER_N_KEYS, tb), F32),
                        pltpu.VMEM((2 * PEER_HEADS, RANK_ROWS, tb), F32),
                        pltpu.VMEM((CAND_ROWS, tb), F32),
                        pltpu.VMEM((RANK_ROWS, tb), F32)],
        compiler_params=_params(("parallel",)),
        name="peer_router",
    )(ht, wqt, keys)


GELU_C = 0.7978845608028654


def _gelu_tanh_x2(x):
    u = x * (GELU_C + (GELU_C * 0.044715) * (x * x))
    return x + x * jnp.tanh(u)


PEER_SUB = 256
GATE_ROWS = 32


def _peer_stage(ht_ref, u_ref, v_ref, thr_ref, w0_ref, s1_ref, e1_ref, o_ref,
                at_w, at_r, pt_w, pt_r, *, te, tb):
    d = o_ref.shape[1]
    n_sub = te // PEER_SUB
    col_w = d // n_sub

    def hidden_piece(j):
        rows = slice(j * PEER_SUB, (j + 1) * PEER_SUB)
        at = jnp.dot(u_ref[rows, :], ht_ref[...], preferred_element_type=F32)
        for lg in range(tb // LANES):
            at_w[lg, rows, :] = at[:, lg * LANES:(lg + 1) * LANES]

    def value_piece(c):
        cols = slice(c * col_w, (c + 1) * col_w)
        o_ref[:, cols] += lax.dot_general(pt_r[...], v_ref[:, cols], (((0,), (0,)), ((), ())),
                                          preferred_element_type=F32)

    n_i0 = te // PEER_N_KEYS
    n_q = PEER_N_KEYS // GATE_ROWS

    def gate_piece(lg, q):
        ls = slice(lg * LANES, (lg + 1) * LANES)
        qs = slice(q * GATE_ROWS, (q + 1) * GATE_ROWS)
        gates = [None] * n_i0
        for h in range(PEER_HEADS):
            s1 = s1_ref[h, lg, qs, :]
            e1 = e1_ref[h, lg, qs, :]
            for i0 in range(n_i0):
                thr = thr_ref[h, i0:i0 + 1, ls]
                w0 = w0_ref[h, i0:i0 + 1, ls]
                term = jnp.where(s1 >= thr, e1 * w0, 0.0)
                gates[i0] = term if gates[i0] is None else gates[i0] + term
        for i0 in range(n_i0):
            er = slice(i0 * PEER_N_KEYS + q * GATE_ROWS, i0 * PEER_N_KEYS + (q + 1) * GATE_ROWS)
            pt_w[er, ls] = (_gelu_tanh_x2(at_r[lg, er, :]) * gates[i0]).astype(BF16)

    mxu_pieces = [functools.partial(hidden_piece, j) for j in range(n_sub)]
    mxu_pieces += [functools.partial(value_piece, c) for c in range(n_sub)]
    vec_pieces = [(lg, q) for lg in range(tb // LANES) for q in range(n_q)]
    assert len(vec_pieces) % len(mxu_pieces) == 0
    per_mxu = len(vec_pieces) // len(mxu_pieces)
    for n, piece in enumerate(mxu_pieces):
        for lg, q in vec_pieces[n * per_mxu:(n + 1) * per_mxu]:
            gate_piece(lg, q)
        piece()


def _peer_kernel(ht_ref, u_ref, v_ref, thr_ref, w0_ref, s1_ref, e1_ref, x1_ref, g_ref,
                 o_ref, at0_ref, at1_ref, pt0_ref, pt1_ref, *, te, tb, n_e):
    g = pl.program_id(0)
    e_out = (g - 2) % n_e
    out_live = g >= 2
    at_refs = (at0_ref, at1_ref)
    pt_refs = (pt0_ref, pt1_ref)

    @pl.when(g == 0)
    def _():
        for r in at_refs + pt_refs:
            r[...] = jnp.zeros(r.shape, r.dtype)

    @pl.when((g == 0) | (out_live & (e_out == 0)))
    def _():
        o_ref[...] = x1_ref[...]

    for parity in range(2):
        @pl.when(g % 2 == parity)
        def _(parity=parity):
            _peer_stage(ht_ref, u_ref, v_ref, thr_ref, w0_ref, s1_ref, e1_ref, o_ref,
                        at_refs[parity], at_refs[1 - parity],
                        pt_refs[1 - parity], pt_refs[parity], te=te, tb=tb)

    @pl.when(out_live & (e_out == n_e - 1))
    def _():
        o_ref[...] = _rms(o_ref[...], g_ref[...])


def peer_experts(ht, u, v, tables, x1, g, tb=512, te=1024):
    d, s = ht.shape
    n_exp = u.shape[0]
    n_i0 = te // PEER_N_KEYS
    n_e = n_exp // te
    n_items = (s // tb) * n_e
    thr, w0, s1, e1 = tables

    def item(gi, lag):
        n = jnp.clip(gi - lag, 0, n_items - 1)
        return n // n_e, n % n_e

    row_spec = pl.BlockSpec((PEER_HEADS, n_i0, tb), lambda gi: (0, item(gi, 1)[1], item(gi, 1)[0]))
    tab_spec = pl.BlockSpec((PEER_HEADS, tb // LANES, PEER_N_KEYS, LANES),
                            lambda gi: (0, item(gi, 1)[0], 0, 0))
    return pl.pallas_call(
        functools.partial(_peer_kernel, te=te, tb=tb, n_e=n_e),
        grid=(n_items + 2,),
        in_specs=[pl.BlockSpec((d, tb), lambda gi: (0, item(gi, 0)[0])),
                  pl.BlockSpec((te, d), lambda gi: (item(gi, 0)[1], 0)),
                  pl.BlockSpec((te, d), lambda gi: (item(gi, 2)[1], 0)),
                  row_spec, row_spec, tab_spec, tab_spec,
                  pl.BlockSpec((tb, d), lambda gi: (item(gi, 2)[0], 0),
                               pipeline_mode=pl.Buffered(1)),
                  pl.BlockSpec((1, d), lambda gi: (0, 0))],
        out_specs=pl.BlockSpec((tb, d), lambda gi: (item(gi, 2)[0], 0)),
        out_shape=jax.ShapeDtypeStruct((s, d), F32),
        scratch_shapes=[pltpu.VMEM((tb // LANES, te, LANES), F32),
                        pltpu.VMEM((tb // LANES, te, LANES), F32),
                        pltpu.VMEM((te, tb), BF16),
                        pltpu.VMEM((te, tb), BF16)],
        compiler_params=_params(("arbitrary",)),
        name="peer_experts",
    )(ht, u, v, thr, w0, s1, e1, x1, g.reshape(1, d))


def kernel(x, norm1_g, w_in, q_norm_g, k_norm_g, w_out, norm2_g, peer_w_query,
           peer_sub_keys, peer_u, peer_v, final_norm_g):
    b, s, d = x.shape
    assert b == 1
    a_width = A_HEADS * HEAD_DIM
    depth = w_in.shape[0]
    xs = x.reshape(s, d)
    for l in range(depth):
        proj = in_projection(xs, norm1_g[l], w_in[l].astype(BF16))
        mixed_a, (u_bf16, v_bf16) = dilated_attention(proj, s, to_bf16=(peer_u[l], peer_v[l]))
        qt, kb, vt = gqa_prep(proj, q_norm_g[l], k_norm_g[l], s, col_block=2)
        mixed_b = gqa_attention(qt, kb, vt, s)
        w_o = w_out[l].astype(BF16)
        x1, ht = out_projection(mixed_a, mixed_b, w_o[:a_width], w_o[a_width:], xs, norm2_g[l])
        keys = peer_sub_keys[l].reshape(2 * PEER_HEADS, PEER_N_KEYS, -1).astype(BF16)
        tables = peer_router(ht, peer_w_query[l].T.astype(BF16), keys)
        assert l == depth - 1
        xs = peer_experts(ht, u_bf16, v_bf16, tables, x1, final_norm_g)
    return xs.reshape(b, s, d)
```

```python
import functools

import numpy as np
import jax
import jax.numpy as jnp
from jax import lax
from jax.experimental import pallas as pl
from jax.experimental.pallas import tpu as pltpu

F32 = jnp.float32
BF16 = jnp.bfloat16

HEAD_DIM = 128
A_HEADS = 8
DIL_PATTERNS = ((128, 1), (512, 4), (2048, 16))
B_Q_HEADS = 8
B_KV_HEADS = 2
GRID_W = 64
ROPE_THETA = 10000.0
NORM_EPS = 1e-6
PEER_HEADS = 8
PEER_N_KEYS = 128
PEER_TOPK = 16

LANES = 128
SUBLANES = 8
BF16_ROWS = 16
NEG_BIG = -1e30
LOG2_E = 1.4426950408889634
VMEM_LIMIT = 56 * 1024 * 1024


def _params(sem, vmem=VMEM_LIMIT):
    return pltpu.CompilerParams(dimension_semantics=sem, vmem_limit_bytes=vmem)


def _rms(x, g):
    ms = jnp.mean(x * x, axis=-1, keepdims=True)
    return x * lax.rsqrt(ms + NORM_EPS) * g


def _inproj_kernel(x_ref, g_ref, w_ref, o_ref, h_ref):
    @pl.when(pl.program_id(1) == 0)
    def _():
        h_ref[...] = _rms(x_ref[...], g_ref[...]).astype(BF16)

    o_ref[...] = jnp.dot(h_ref[...], w_ref[...], preferred_element_type=F32)


def in_projection(x, g, w_bf16, tm=1024, tn=512):
    s, d = x.shape
    n = w_bf16.shape[1]
    return pl.pallas_call(
        _inproj_kernel,
        grid=(s // tm, n // tn),
        in_specs=[pl.BlockSpec((tm, d), lambda i, j: (i, 0)),
                  pl.BlockSpec((1, d), lambda i, j: (0, 0)),
                  pl.BlockSpec((d, tn), lambda i, j: (0, j))],
        out_specs=pl.BlockSpec((tm, tn), lambda i, j: (i, j)),
        out_shape=jax.ShapeDtypeStruct((s, n), F32),
        scratch_shapes=[pltpu.VMEM((tm, d), BF16)],
        compiler_params=_params(("parallel", "arbitrary")),
        name="in_projection",
    )(x, g.reshape(1, d), w_bf16)


DIL_QBLK = 2048
DIL_SUB = 128
DIL_KWIN = 256
DIL_UNROLL = 16
DIL_EDGE_CASES = 3


def _dilated_kernel(q_ref, k_ref, v_ref, *rest, seq, n_cast):
    cast_src, rest = rest[:n_cast], rest[n_cast:]
    o_ref, cast_dst, rest = rest[0], rest[1:1 + n_cast], rest[1 + n_cast:]
    m_ref, l_ref, acc_ref, bias_ref = rest
    for src, dst in zip(cast_src, cast_dst):
        dst[...] = src[...].astype(dst.dtype)

    h = pl.program_id(0)
    base = pl.program_id(1) * DIL_QBLK
    slope = jnp.exp2(-(h + 1).astype(F32)) * LOG2_E
    scale = HEAD_DIM ** -0.5 * LOG2_E

    m_ref[...] = jnp.full(m_ref.shape, NEG_BIG, F32)
    l_ref[...] = jnp.zeros(l_ref.shape, F32)
    acc_ref[...] = jnp.zeros(acc_ref.shape, F32)

    row = lax.broadcasted_iota(jnp.int32, (DIL_SUB, DIL_KWIN), 0)
    col = lax.broadcasted_iota(jnp.int32, (DIL_SUB, DIL_KWIN), 1)
    col_minus_row = col - row

    for b_idx, (w, d) in enumerate(DIL_PATTERNS):
        half = (w // 2) // d
        assert half * 2 + DIL_SUB == DIL_KWIN
        seg_len = seq // d

        for v_idx in range(DIL_EDGE_CASES):
            rel = jnp.abs(col_minus_row - v_idx * half)
            bias_ref[b_idx * DIL_EDGE_CASES + v_idx] = jnp.where(
                rel <= half, -(slope * d) * rel.astype(F32), NEG_BIG)

        def rows(start, n, d=d):
            return pl.ds(start, n, stride=d) if d > 1 else pl.ds(start, n)

        def body(c, carry, d=d, half=half, seg_len=seg_len, rows=rows, b_idx=b_idx):
            r = c % d
            cc = c // d
            p0 = base // d + cc * DIL_SUB
            kp = jnp.clip(p0 - half, 0, seg_len - DIL_KWIN)
            lq = r + d * (cc * DIL_SUB)
            ks = r + d * kp

            q = (q_ref[rows(lq, DIL_SUB), :] * scale).astype(BF16)
            k = k_ref[rows(ks, DIL_KWIN), :].astype(BF16)
            v = v_ref[rows(ks, DIL_KWIN), :].astype(BF16)
            s = lax.dot_general(q, k, (((1,), (1,)), ((), ())), preferred_element_type=F32)
            s = s + bias_ref[b_idx * DIL_EDGE_CASES + (p0 - kp) // half]
            mb = jnp.max(s, axis=1, keepdims=True)
            p = jnp.exp2(s - mb)
            lb = jnp.sum(p, axis=1, keepdims=True)
            ob = jnp.dot(p.astype(BF16), v, preferred_element_type=F32)

            idx = rows(lq, DIL_SUB)
            m_old = m_ref[idx, :]
            m_new = jnp.maximum(m_old, mb)
            a_old = jnp.exp2(m_old - m_new)
            a_blk = jnp.exp2(mb - m_new)
            m_ref[idx, :] = m_new
            l_ref[idx, :] = a_old * l_ref[idx, :] + a_blk * lb
            acc_ref[idx, :] = a_old * acc_ref[idx, :] + a_blk * ob
            return carry

        def group(cg, carry, body=body):
            for u in range(DIL_UNROLL):
                body(cg * DIL_UNROLL + u, carry)
            return carry

        lax.fori_loop(0, DIL_QBLK // DIL_SUB // DIL_UNROLL, group, 0)

    o_ref[...] = (acc_ref[...] / l_ref[...]).astype(o_ref.dtype)


def dilated_attention(proj, seq, to_bf16=()):
    assert seq % DIL_QBLK == 0
    for _, d in DIL_PATTERNS:
        assert DIL_QBLK // d >= DIL_SUB and seq // d >= DIL_KWIN
    nh = A_HEADS
    n_qb = seq // DIL_QBLK
    n_steps = nh * n_qb
    cast_specs = []
    for w in to_bf16:
        assert w.shape[0] % n_steps == 0
        cast_specs.append(pl.BlockSpec((w.shape[0] // n_steps, w.shape[1]),
                                       lambda h, i: (h * n_qb + i, 0)))
    outs = pl.pallas_call(
        functools.partial(_dilated_kernel, seq=seq, n_cast=len(to_bf16)),
        grid=(nh, n_qb),
        in_specs=[pl.BlockSpec((DIL_QBLK, HEAD_DIM), lambda h, i: (i, h)),
                  pl.BlockSpec((seq, HEAD_DIM), lambda h, i: (0, nh + h)),
                  pl.BlockSpec((seq, HEAD_DIM), lambda h, i: (0, 2 * nh + h))] + cast_specs,
        out_specs=[pl.BlockSpec((DIL_QBLK, HEAD_DIM), lambda h, i: (i, h))] + cast_specs,
        out_shape=[jax.ShapeDtypeStruct((seq, nh * HEAD_DIM), BF16)]
        + [jax.ShapeDtypeStruct(w.shape, BF16) for w in to_bf16],
        scratch_shapes=[pltpu.VMEM((DIL_QBLK, LANES), F32),
                        pltpu.VMEM((DIL_QBLK, LANES), F32),
                        pltpu.VMEM((DIL_QBLK, HEAD_DIM), F32),
                        pltpu.VMEM((len(DIL_PATTERNS) * DIL_EDGE_CASES, DIL_SUB, DIL_KWIN), F32)],
        compiler_params=_params(("parallel", "arbitrary")),
        name="dilated_attention",
    )(proj, proj, proj, *to_bf16)
    return outs[0], outs[1:]


def _rope_tables(seq):
    rows = seq // GRID_W
    row = jnp.repeat(jnp.arange(rows, dtype=F32), GRID_W)
    col = jnp.tile(jnp.arange(GRID_W, dtype=F32), rows)
    half = HEAD_DIM // 2
    inv = ROPE_THETA ** (-jnp.arange(0, half, 2, dtype=F32) / half)
    ang = jnp.concatenate([row[:, None] * inv, col[:, None] * inv], axis=-1)
    cos = jnp.repeat(jnp.cos(ang), 2, axis=-1)
    sign = jnp.tile(jnp.asarray([-1.0, 1.0], F32), half)
    sin = jnp.repeat(jnp.sin(ang), 2, axis=-1) * sign
    return cos, sin


def _gqa_prep_kernel(p_ref, cos_ref, sin_ref, qg_ref, kg_ref, qt_ref, k_ref, vt_ref):
    cos = cos_ref[...]
    sin = sin_ref[...]
    even = (lax.broadcasted_iota(jnp.int32, cos.shape, 1) % 2) == 0

    def norm_rope(xh, g):
        y = _rms(xh, g)
        partner = jnp.where(even, pltpu.roll(y, LANES - 1, 1), pltpu.roll(y, 1, 1))
        return y * cos + partner * sin

    scale = HEAD_DIM ** -0.5 * LOG2_E
    for hq in range(B_Q_HEADS):
        xh = p_ref[:, hq * HEAD_DIM:(hq + 1) * HEAD_DIM]
        y = norm_rope(xh, qg_ref[...]) * scale
        qt_ref[hq * HEAD_DIM:(hq + 1) * HEAD_DIM, :] = y.T.astype(BF16)
    k0 = B_Q_HEADS * HEAD_DIM
    v0 = k0 + B_KV_HEADS * HEAD_DIM
    for hk in range(B_KV_HEADS):
        xh = p_ref[:, k0 + hk * HEAD_DIM:k0 + (hk + 1) * HEAD_DIM]
        k_ref[:, hk * HEAD_DIM:(hk + 1) * HEAD_DIM] = norm_rope(xh, kg_ref[...]).astype(BF16)
        vh = p_ref[:, v0 + hk * HEAD_DIM:v0 + (hk + 1) * HEAD_DIM]
        vt_ref[hk * HEAD_DIM:(hk + 1) * HEAD_DIM, :] = vh.T.astype(BF16)


def gqa_prep(proj, q_g, k_g, seq, col_block, ts=512):
    wb = (B_Q_HEADS + 2 * B_KV_HEADS) * HEAD_DIM
    cos, sin = _rope_tables(seq)
    return pl.pallas_call(
        _gqa_prep_kernel,
        grid=(seq // ts,),
        in_specs=[pl.BlockSpec((ts, wb), lambda i: (i, col_block)),
                  pl.BlockSpec((ts, HEAD_DIM), lambda i: (i, 0)),
                  pl.BlockSpec((ts, HEAD_DIM), lambda i: (i, 0)),
                  pl.BlockSpec((1, HEAD_DIM), lambda i: (0, 0)),
                  pl.BlockSpec((1, HEAD_DIM), lambda i: (0, 0))],
        out_specs=[pl.BlockSpec((B_Q_HEADS * HEAD_DIM, ts), lambda i: (0, i)),
                   pl.BlockSpec((ts, B_KV_HEADS * HEAD_DIM), lambda i: (i, 0)),
                   pl.BlockSpec((B_KV_HEADS * HEAD_DIM, ts), lambda i: (0, i))],
        out_shape=[jax.ShapeDtypeStruct((B_Q_HEADS * HEAD_DIM, seq), BF16),
                   jax.ShapeDtypeStruct((seq, B_KV_HEADS * HEAD_DIM), BF16),
                   jax.ShapeDtypeStruct((B_KV_HEADS * HEAD_DIM, seq), BF16)],
        compiler_params=_params(("parallel",)),
        name="gqa_prep",
    )(proj, cos, sin, q_g.reshape(1, HEAD_DIM), k_g.reshape(1, HEAD_DIM))


STAGE_ROWS = 32
GQA_UNROLL = 1


def _data_dependent_zero(tiles):
    bits = None
    for t in tiles:
        b = pltpu.bitcast(t, jnp.uint32)
        bits = b if bits is None else bits | b
    zero_words = (bits >> 16) >> 16
    return zero_words.astype(jnp.int32).astype(F32).astype(BF16)


def _gqa_kernel(qt_ref, k_ref, vt_ref, o_ref, m_ref, acc_ref, st_ref, ks_ref, *, seq, tk, nh):
    n_chunks = seq // tk
    tq = qt_ref.shape[1]
    m_ref[...] = jnp.full(m_ref.shape, NEG_BIG, F32)
    acc_ref[...] = jnp.zeros(acc_ref.shape, F32)

    def q_head(hh):
        return qt_ref[hh * HEAD_DIM:(hh + 1) * HEAD_DIM, :]

    def chunk_start(c):
        return pl.multiple_of(jnp.minimum(c, n_chunks - 1) * tk, tk)

    st_ref[...] = jnp.dot(k_ref[pl.ds(0, tk), :], q_head(0), preferred_element_type=F32)

    def chunk_items(c, st):
        here = chunk_start(c)
        vt = jnp.concatenate([vt_ref[:, pl.ds(here, tk)], jnp.ones((BF16_ROWS, tk), BF16)], axis=0)
        for hh in range(nh):
            m_old = m_ref[hh]
            m_new = jnp.maximum(m_old, jnp.max(st, axis=0, keepdims=True))
            alpha = jnp.exp2(m_old - m_new)
            pb = jnp.exp2(st - m_new).astype(BF16)

            nxt = here if hh + 1 < nh else chunk_start(c + 1)
            slot = ks_ref.at[hh % 2]
            for r in range(0, tk, STAGE_ROWS):
                strip = [pb[r:r + STAGE_ROWS, j:j + LANES] for j in range(0, tq, LANES)]
                zero = _data_dependent_zero(strip)
                piece = k_ref[pl.ds(pl.multiple_of(nxt + r, STAGE_ROWS), STAGE_ROWS), :]
                slot[r:r + STAGE_ROWS, :] = piece + pltpu.repeat(zero, STAGE_ROWS // zero.shape[0], axis=0)
            st = jnp.dot(slot[...], q_head((hh + 1) % nh), preferred_element_type=F32)

            acc_ref[hh] = alpha * acc_ref[hh] + jnp.dot(vt, pb, preferred_element_type=F32)
            m_ref[hh] = m_new
        return st

    def body(i, carry):
        st = st_ref[...]
        for u in range(GQA_UNROLL):
            st = chunk_items(i * GQA_UNROLL + u, st)
        st_ref[...] = st
        return carry

    lax.fori_loop(0, n_chunks // GQA_UNROLL, body, 0)
    for hh in range(nh):
        acc = acc_ref[hh]
        out = acc[:HEAD_DIM] / acc[HEAD_DIM:HEAD_DIM + 1]
        o_ref[:, hh * HEAD_DIM:(hh + 1) * HEAD_DIM] = out.T.astype(o_ref.dtype)


def gqa_attention(qt, k, vt, seq, tq=512, tk=512):
    nh = B_Q_HEADS // B_KV_HEADS
    return pl.pallas_call(
        functools.partial(_gqa_kernel, seq=seq, tk=tk, nh=nh),
        grid=(B_KV_HEADS, seq // tq),
        in_specs=[pl.BlockSpec((nh * HEAD_DIM, tq), lambda g, i: (g, i)),
                  pl.BlockSpec((seq, HEAD_DIM), lambda g, i: (0, g)),
                  pl.BlockSpec((HEAD_DIM, seq), lambda g, i: (g, 0))],
        out_specs=pl.BlockSpec((tq, nh * HEAD_DIM), lambda g, i: (i, g)),
        out_shape=jax.ShapeDtypeStruct((seq, B_Q_HEADS * HEAD_DIM), BF16),
        scratch_shapes=[pltpu.VMEM((nh, 1, tq), F32),
                        pltpu.VMEM((nh, HEAD_DIM + BF16_ROWS, tq), F32),
                        pltpu.VMEM((tk, tq), F32),
                        pltpu.VMEM((2, tk, HEAD_DIM), BF16)],
        compiler_params=_params(("parallel", "arbitrary")),
        name="gqa_attention",
    )(qt, k, vt)


def _outproj_kernel(ma_ref, mb_ref, wa_ref, wb_ref, x_ref, g_ref, x1_ref, ht_ref):
    y = x_ref[...]
    y = y + jnp.dot(ma_ref[...], wa_ref[...], preferred_element_type=F32)
    y = y + jnp.dot(mb_ref[...], wb_ref[...], preferred_element_type=F32)
    x1_ref[...] = y
    ht_ref[...] = _rms(y, g_ref[...]).T.astype(BF16)


def out_projection(mixed_a, mixed_b, w, x, g, tm=512):
    s, d = x.shape
    wa, wb = mixed_a.shape[1], mixed_b.shape[1]
    assert wa == wb and w.shape[0] == wa + wb
    return pl.pallas_call(
        _outproj_kernel,
        grid=(s // tm,),
        in_specs=[pl.BlockSpec((tm, wa), lambda i: (i, 0)),
                  pl.BlockSpec((tm, wb), lambda i: (i, 0)),
                  pl.BlockSpec((wa, d), lambda i: (0, 0)),
                  pl.BlockSpec((wb, d), lambda i: (1, 0)),
                  pl.BlockSpec((tm, d), lambda i: (i, 0)),
                  pl.BlockSpec((1, d), lambda i: (0, 0))],
        out_specs=[pl.BlockSpec((tm, d), lambda i: (i, 0)),
                   pl.BlockSpec((d, tm), lambda i: (0, i))],
        out_shape=[jax.ShapeDtypeStruct((s, d), F32),
                   jax.ShapeDtypeStruct((d, s), BF16)],
        compiler_params=_params(("parallel",)),
        name="out_projection",
    )(mixed_a, mixed_b, w, w, x, g.reshape(1, d))


N_RANK = PEER_TOPK + 1
RANK_ROWS = 24
CAND_COUNTS = tuple(N_RANK // (a + 1) for a in range(N_RANK))
CAND_ROWS = 8 * (1 << (-(-sum(CAND_COUNTS) // 8) - 1).bit_length())
ROUTER_UNROLL = 8


def _odd_even_merge_sort_pairs(n):
    pairs = []

    def merge(lo, hi, r):
        step = r * 2
        if step < hi - lo:
            merge(lo, hi, step)
            merge(lo + r, hi, step)
            pairs.extend((i, i + r) for i in range(lo + r, hi - r, step))
        else:
            pairs.append((lo, lo + r))

    def sort(lo, hi):
        if hi - lo >= 1:
            mid = lo + (hi - lo) // 2
            sort(lo, mid)
            sort(mid + 1, hi)
            merge(lo, hi, 1)

    sort(0, n - 1)
    return pairs


def _extract_sorted_tile(tile, n, emit):
    depth = tile.shape[0] // SUBLANES
    lists = [tile[SUBLANES * j:SUBLANES * (j + 1)] for j in range(depth)]
    for i, j in _odd_even_merge_sort_pairs(depth):
        lists[i], lists[j] = jnp.maximum(lists[i], lists[j]), jnp.minimum(lists[i], lists[j])
    for r in range(n):
        mj = jnp.max(lists[0], axis=0, keepdims=True)
        emit(r, mj)
        took = lists[0] == mj
        for j in range(min(depth, n - (r + 1))):
            below = lists[j + 1] if j + 1 < depth else -jnp.inf
            lists[j] = jnp.where(took, below, lists[j])


def _router_kernel(ht_ref, wq_ref, keys_ref, thr_ref, w0_ref, s1_ref, e1_ref,
                   q_scr, s_scr, top_scr, cand_scr, best_scr, *, tb):
    q_scr[...] = lax.dot_general(wq_ref[...], ht_ref[...], (((0,), (0,)), ((), ())),
                                 preferred_element_type=F32).astype(BF16)
    lane_groups = [slice(lg * LANES, (lg + 1) * LANES) for lg in range(tb // LANES)]

    def half_body(hc):
        qhc = q_scr[pl.ds(pl.multiple_of(hc * PEER_N_KEYS, PEER_N_KEYS), PEER_N_KEYS), :]
        s = jnp.dot(keys_ref[hc], qhc, preferred_element_type=F32)
        s_scr[hc] = s
        for ls in lane_groups:
            def emit(j, mj, ls=ls):
                top_scr[hc, pl.ds(j, 1), ls] = mj

            _extract_sorted_tile(s[:, ls], N_RANK, emit)

    def half_group(i, carry):
        for u in range(ROUTER_UNROLL):
            half_body(i * ROUTER_UNROLL + u)
        return carry

    lax.fori_loop(0, 2 * PEER_HEADS // ROUTER_UNROLL, half_group, 0)

    def head_body(h, carry):
        t0 = top_scr[2 * h]
        t1 = top_scr[2 * h + 1]
        cand_scr[...] = jnp.full(cand_scr.shape, -jnp.inf, F32)
        off = 0
        for a, nb in enumerate(CAND_COUNTS):
            cand_scr[off:off + nb, :] = t0[a:a + 1, :] + t1[0:nb, :]
            off += nb

        for ls in lane_groups:
            def emit(j, mj, ls=ls):
                best_scr[pl.ds(j, 1), ls] = mj

            _extract_sorted_tile(cand_scr[:, ls], N_RANK, emit)
        best = [best_scr[j:j + 1, :] for j in range(N_RANK)]
        theta = 0.5 * (best[PEER_TOPK - 1] + best[PEER_TOPK])
        z = jnp.zeros_like(theta)
        for j in range(PEER_TOPK):
            z = z + jnp.exp(best[j] - best[0])

        s0 = s_scr[2 * h]
        s1 = s_scr[2 * h + 1]
        thr_ref[h] = theta - s0
        w0_ref[h] = jnp.exp(s0 - t0[0:1, :]) * (0.5 / z)
        e1 = jnp.exp(s1 - t1[0:1, :])
        for lg, ls in enumerate(lane_groups):
            s1_ref[h, lg] = s1[:, ls]
            e1_ref[h, lg] = e1[:, ls]
        return carry

    lax.fori_loop(0, PEER_HEADS, head_body, 0)


def peer_router(ht, wq, keys, tb=512):
    d, s = ht.shape
    nq = wq.shape[1]
    tab = jax.ShapeDtypeStruct((PEER_HEADS, PEER_N_KEYS, s), F32)
    tab_spec = pl.BlockSpec((PEER_HEADS, PEER_N_KEYS, tb), lambda i: (0, 0, i))
    tile = jax.ShapeDtypeStruct((PEER_HEADS, s // LANES, PEER_N_KEYS, LANES), F32)
    tile_spec = pl.BlockSpec((PEER_HEADS, tb // LANES, PEER_N_KEYS, LANES), lambda i: (0, i, 0, 0))
    return pl.pallas_call(
        functools.partial(_router_kernel, tb=tb),
        grid=(s // tb,),
        in_specs=[pl.BlockSpec((d, tb), lambda i: (0, i)),
                  pl.BlockSpec((d, nq), lambda i: (0, 0)),
                  pl.BlockSpec(keys.shape, lambda i: (0, 0, 0))],
        out_specs=[tab_spec, tab_spec, tile_spec, tile_spec],
        out_shape=[tab, tab, tile, tile],
        scratch_shapes=[pltpu.VMEM((nq, tb), BF16),
                        pltpu.VMEM((2 * PEER_HEADS, PEER_N_KEYS, tb), F32),
                        pltpu.VMEM((2 * PEER_HEADS, RANK_ROWS, tb), F32),
                        pltpu.VMEM((CAND_ROWS, tb), F32),
                        pltpu.VMEM((RANK_ROWS, tb), F32)],
        compiler_params=_params(("parallel",)),
        name="peer_router",
    )(ht, wq, keys)


GELU_C = 0.7978845608028654


def _gelu_tanh_x2(x):
    u = x * (GELU_C + (GELU_C * 0.044715) * (x * x))
    return x + x * jnp.tanh(u)


PEER_SUB = 256
GATE_ROWS = 32


def _peer_stage(ht_ref, u_ref, v_ref, thr_ref, w0_ref, s1_ref, e1_ref, o_ref,
                at_w, at_r, pt_w, pt_r, *, te, tb):
    d = o_ref.shape[1]
    n_sub = te // PEER_SUB
    col_w = d // n_sub

    def hidden_piece(j):
        rows = slice(j * PEER_SUB, (j + 1) * PEER_SUB)
        at = jnp.dot(u_ref[rows, :], ht_ref[...], preferred_element_type=F32)
        for lg in range(tb // LANES):
            at_w[lg, rows, :] = at[:, lg * LANES:(lg + 1) * LANES]

    def value_piece(c):
        cols = slice(c * col_w, (c + 1) * col_w)
        o_ref[:, cols] += lax.dot_general(pt_r[...], v_ref[:, cols], (((0,), (0,)), ((), ())),
                                          preferred_element_type=F32)

    n_i0 = te // PEER_N_KEYS
    n_q = PEER_N_KEYS // GATE_ROWS

    def gate_piece(lg, q):
        ls = slice(lg * LANES, (lg + 1) * LANES)
        qs = slice(q * GATE_ROWS, (q + 1) * GATE_ROWS)
        gates = [None] * n_i0
        for h in range(PEER_HEADS):
            s1 = s1_ref[h, lg, qs, :]
            e1 = e1_ref[h, lg, qs, :]
            for i0 in range(n_i0):
                thr = thr_ref[h, i0:i0 + 1, ls]
                w0 = w0_ref[h, i0:i0 + 1, ls]
                term = jnp.where(s1 >= thr, e1 * w0, 0.0)
                gates[i0] = term if gates[i0] is None else gates[i0] + term
        for i0 in range(n_i0):
            er = slice(i0 * PEER_N_KEYS + q * GATE_ROWS, i0 * PEER_N_KEYS + (q + 1) * GATE_ROWS)
            pt_w[er, ls] = (_gelu_tanh_x2(at_r[lg, er, :]) * gates[i0]).astype(BF16)

    mxu_pieces = [functools.partial(hidden_piece, j) for j in range(n_sub)]
    mxu_pieces += [functools.partial(value_piece, c) for c in range(n_sub)]
    vec_pieces = [(lg, q) for lg in range(tb // LANES) for q in range(n_q)]
    assert len(vec_pieces) % len(mxu_pieces) == 0
    per_mxu = len(vec_pieces) // len(mxu_pieces)
    for n, piece in enumerate(mxu_pieces):
        for lg, q in vec_pieces[n * per_mxu:(n + 1) * per_mxu]:
            gate_piece(lg, q)
        piece()


def _peer_kernel(ht_ref, u_ref, v_ref, thr_ref, w0_ref, s1_ref, e1_ref, x1_ref, g_ref,
                 o_ref, at0_ref, at1_ref, pt0_ref, pt1_ref, *, te, tb, n_e):
    g = pl.program_id(0)
    e_out = (g - 2) % n_e
    out_live = g >= 2
    at_refs = (at0_ref, at1_ref)
    pt_refs = (pt0_ref, pt1_ref)

    @pl.when(g == 0)
    def _():
        for r in at_refs + pt_refs:
            r[...] = jnp.zeros(r.shape, r.dtype)

    @pl.when((g == 0) | (out_live & (e_out == 0)))
    def _():
        o_ref[...] = x1_ref[...]

    for parity in range(2):
        @pl.when(g % 2 == parity)
        def _(parity=parity):
            _peer_stage(ht_ref, u_ref, v_ref, thr_ref, w0_ref, s1_ref, e1_ref, o_ref,
                        at_refs[parity], at_refs[1 - parity],
                        pt_refs[1 - parity], pt_refs[parity], te=te, tb=tb)

    @pl.when(out_live & (e_out == n_e - 1))
    def _():
        o_ref[...] = _rms(o_ref[...], g_ref[...])


def peer_experts(ht, u, v, tables, x1, g, tb=512, te=1024):
    d, s = ht.shape
    n_exp = u.shape[0]
    n_i0 = te // PEER_N_KEYS
    n_e = n_exp // te
    n_items = (s // tb) * n_e
    thr, w0, s1, e1 = tables

    def item(gi, lag):
        n = jnp.clip(gi - lag, 0, n_items - 1)
        return n // n_e, n % n_e

    row_spec = pl.BlockSpec((PEER_HEADS, n_i0, tb), lambda gi: (0, item(gi, 1)[1], item(gi, 1)[0]))
    tab_spec = pl.BlockSpec((PEER_HEADS, tb // LANES, PEER_N_KEYS, LANES),
                            lambda gi: (0, item(gi, 1)[0], 0, 0))
    return pl.pallas_call(
        functools.partial(_peer_kernel, te=te, tb=tb, n_e=n_e),
        grid=(n_items + 2,),
        in_specs=[pl.BlockSpec((d, tb), lambda gi: (0, item(gi, 0)[0])),
                  pl.BlockSpec((te, d), lambda gi: (item(gi, 0)[1], 0)),
                  pl.BlockSpec((te, d), lambda gi: (item(gi, 2)[1], 0)),
                  row_spec, row_spec, tab_spec, tab_spec,
                  pl.BlockSpec((tb, d), lambda gi: (item(gi, 2)[0], 0),
                               pipeline_mode=pl.Buffered(1)),
                  pl.BlockSpec((1, d), lambda gi: (0, 0))],
        out_specs=pl.BlockSpec((tb, d), lambda gi: (item(gi, 2)[0], 0)),
        out_shape=jax.ShapeDtypeStruct((s, d), F32),
        scratch_shapes=[pltpu.VMEM((tb // LANES, te, LANES), F32),
                        pltpu.VMEM((tb // LANES, te, LANES), F32),
                        pltpu.VMEM((te, tb), BF16),
                        pltpu.VMEM((te, tb), BF16)],
        compiler_params=_params(("arbitrary",)),
        name="peer_experts",
    )(ht, u, v, thr, w0, s1, e1, x1, g.reshape(1, d))


def kernel(x, norm1_g, w_in, q_norm_g, k_norm_g, w_out, norm2_g, peer_w_query,
           peer_sub_keys, peer_u, peer_v, final_norm_g):
    b, s, d = x.shape
    assert b == 1
    depth = w_in.shape[0]
    xs = x.reshape(s, d)
    for l in range(depth):
        proj = in_projection(xs, norm1_g[l], w_in[l].astype(BF16))
        mixed_a, (u_bf16, v_bf16, w_o, w_q) = dilated_attention(
            proj, s, to_bf16=(peer_u[l], peer_v[l], w_out[l], peer_w_query[l]))
        qt, kb, vt = gqa_prep(proj, q_norm_g[l], k_norm_g[l], s, col_block=2)
        mixed_b = gqa_attention(qt, kb, vt, s)
        x1, ht = out_projection(mixed_a, mixed_b, w_o, xs, norm2_g[l])
        keys = peer_sub_keys[l].reshape(2 * PEER_HEADS, PEER_N_KEYS, -1).astype(BF16)
        tables = peer_router(ht, w_q, keys)
        assert l == depth - 1
        xs = peer_experts(ht, u_bf16, v_bf16, tables, x1, final_norm_g)
    return xs.reshape(b, s, d)
```

```python
import functools

import numpy as np
import jax
import jax.numpy as jnp
from jax import lax
from jax.experimental import pallas as pl
from jax.experimental.pallas import tpu as pltpu

F32 = jnp.float32
BF16 = jnp.bfloat16

HEAD_DIM = 128
A_HEADS = 8
DIL_PATTERNS = ((128, 1), (512, 4), (2048, 16))
B_Q_HEADS = 8
B_KV_HEADS = 2
GRID_W = 64
ROPE_THETA = 10000.0
NORM_EPS = 1e-6
PEER_HEADS = 8
PEER_N_KEYS = 128
PEER_TOPK = 16

LANES = 128
SUBLANES = 8
BF16_ROWS = 16
NEG_BIG = -1e30
LOG2_E = 1.4426950408889634
VMEM_LIMIT = 56 * 1024 * 1024


def _params(sem, vmem=VMEM_LIMIT):
    return pltpu.CompilerParams(dimension_semantics=sem, vmem_limit_bytes=vmem)


def _rms(x, g):
    ms = jnp.mean(x * x, axis=-1, keepdims=True)
    return x * lax.rsqrt(ms + NORM_EPS) * g


def _inproj_kernel(x_ref, g_ref, w_ref, o_ref, h_ref):
    @pl.when(pl.program_id(1) == 0)
    def _():
        h_ref[...] = _rms(x_ref[...], g_ref[...]).astype(BF16)

    o_ref[...] = jnp.dot(h_ref[...], w_ref[...], preferred_element_type=F32)


def in_projection(x, g, w_bf16, tm=1024, tn=512):
    s, d = x.shape
    n = w_bf16.shape[1]
    return pl.pallas_call(
        _inproj_kernel,
        grid=(s // tm, n // tn),
        in_specs=[pl.BlockSpec((tm, d), lambda i, j: (i, 0)),
                  pl.BlockSpec((1, d), lambda i, j: (0, 0)),
                  pl.BlockSpec((d, tn), lambda i, j: (0, j))],
        out_specs=pl.BlockSpec((tm, tn), lambda i, j: (i, j)),
        out_shape=jax.ShapeDtypeStruct((s, n), F32),
        scratch_shapes=[pltpu.VMEM((tm, d), BF16)],
        compiler_params=_params(("parallel", "arbitrary")),
        name="in_projection",
    )(x, g.reshape(1, d), w_bf16)


DIL_QBLK = 2048
DIL_SUB = 128
DIL_KWIN = 256
DIL_UNROLL = 16
DIL_EDGE_CASES = 3


def _dilated_kernel(q_ref, k_ref, v_ref, *rest, seq, n_cast):
    cast_src, rest = rest[:n_cast], rest[n_cast:]
    o_ref, cast_dst, rest = rest[0], rest[1:1 + n_cast], rest[1 + n_cast:]
    m_ref, l_ref, acc_ref, bias_ref = rest
    for src, dst in zip(cast_src, cast_dst):
        dst[...] = src[...].astype(dst.dtype)

    h = pl.program_id(0)
    base = pl.program_id(1) * DIL_QBLK
    slope = jnp.exp2(-(h + 1).astype(F32)) * LOG2_E
    scale = HEAD_DIM ** -0.5 * LOG2_E

    m_ref[...] = jnp.full(m_ref.shape, NEG_BIG, F32)
    l_ref[...] = jnp.zeros(l_ref.shape, F32)
    acc_ref[...] = jnp.zeros(acc_ref.shape, F32)

    row = lax.broadcasted_iota(jnp.int32, (DIL_SUB, DIL_KWIN), 0)
    col = lax.broadcasted_iota(jnp.int32, (DIL_SUB, DIL_KWIN), 1)
    col_minus_row = col - row

    for b_idx, (w, d) in enumerate(DIL_PATTERNS):
        half = (w // 2) // d
        assert half * 2 + DIL_SUB == DIL_KWIN
        seg_len = seq // d

        for v_idx in range(DIL_EDGE_CASES):
            rel = jnp.abs(col_minus_row - v_idx * half)
            bias_ref[b_idx * DIL_EDGE_CASES + v_idx] = jnp.where(
                rel <= half, -(slope * d) * rel.astype(F32), NEG_BIG)

        def rows(start, n, d=d):
            return pl.ds(start, n, stride=d) if d > 1 else pl.ds(start, n)

        def body(c, carry, d=d, half=half, seg_len=seg_len, rows=rows, b_idx=b_idx):
            r = c % d
            cc = c // d
            p0 = base // d + cc * DIL_SUB
            kp = jnp.clip(p0 - half, 0, seg_len - DIL_KWIN)
            lq = r + d * (cc * DIL_SUB)
            ks = r + d * kp

            q = (q_ref[rows(lq, DIL_SUB), :] * scale).astype(BF16)
            k = k_ref[rows(ks, DIL_KWIN), :].astype(BF16)
            v = v_ref[rows(ks, DIL_KWIN), :].astype(BF16)
            s = lax.dot_general(q, k, (((1,), (1,)), ((), ())), preferred_element_type=F32)
            s = s + bias_ref[b_idx * DIL_EDGE_CASES + (p0 - kp) // half]
            mb = jnp.max(s, axis=1, keepdims=True)
            p = jnp.exp2(s - mb)
            lb = jnp.sum(p, axis=1, keepdims=True)
            ob = jnp.dot(p.astype(BF16), v, preferred_element_type=F32)

            idx = rows(lq, DIL_SUB)
            m_old = m_ref[idx, :]
            m_new = jnp.maximum(m_old, mb)
            a_old = jnp.exp2(m_old - m_new)
            a_blk = jnp.exp2(mb - m_new)
            m_ref[idx, :] = m_new
            l_ref[idx, :] = a_old * l_ref[idx, :] + a_blk * lb
            acc_ref[idx, :] = a_old * acc_ref[idx, :] + a_blk * ob
            return carry

        def group(cg, carry, body=body):
            for u in range(DIL_UNROLL):
                body(cg * DIL_UNROLL + u, carry)
            return carry

        lax.fori_loop(0, DIL_QBLK // DIL_SUB // DIL_UNROLL, group, 0)

    o_ref[...] = (acc_ref[...] / l_ref[...]).astype(o_ref.dtype)


def dilated_attention(proj, seq, to_bf16=()):
    assert seq % DIL_QBLK == 0
    for _, d in DIL_PATTERNS:
        assert DIL_QBLK // d >= DIL_SUB and seq // d >= DIL_KWIN
    nh = A_HEADS
    n_qb = seq // DIL_QBLK
    n_steps = nh * n_qb
    cast_specs = []
    for w in to_bf16:
        assert w.shape[0] % n_steps == 0
        cast_specs.append(pl.BlockSpec((w.shape[0] // n_steps, w.shape[1]),
                                       lambda h, i: (h * n_qb + i, 0)))
    outs = pl.pallas_call(
        functools.partial(_dilated_kernel, seq=seq, n_cast=len(to_bf16)),
        grid=(nh, n_qb),
        in_specs=[pl.BlockSpec((DIL_QBLK, HEAD_DIM), lambda h, i: (i, h)),
                  pl.BlockSpec((seq, HEAD_DIM), lambda h, i: (0, nh + h)),
                  pl.BlockSpec((seq, HEAD_DIM), lambda h, i: (0, 2 * nh + h))] + cast_specs,
        out_specs=[pl.BlockSpec((DIL_QBLK, HEAD_DIM), lambda h, i: (i, h))] + cast_specs,
        out_shape=[jax.ShapeDtypeStruct((seq, nh * HEAD_DIM), BF16)]
        + [jax.ShapeDtypeStruct(w.shape, BF16) for w in to_bf16],
        scratch_shapes=[pltpu.VMEM((DIL_QBLK, LANES), F32),
                        pltpu.VMEM((DIL_QBLK, LANES), F32),
                        pltpu.VMEM((DIL_QBLK, HEAD_DIM), F32),
                        pltpu.VMEM((len(DIL_PATTERNS) * DIL_EDGE_CASES, DIL_SUB, DIL_KWIN), F32)],
        compiler_params=_params(("parallel", "arbitrary")),
        name="dilated_attention",
    )(proj, proj, proj, *to_bf16)
    return outs[0], outs[1:]


def _rope_tables(seq):
    rows = seq // GRID_W
    row = np.repeat(np.arange(rows, dtype=np.float64), GRID_W)
    col = np.tile(np.arange(GRID_W, dtype=np.float64), rows)
    half = HEAD_DIM // 2
    inv = ROPE_THETA ** (-np.arange(0, half, 2, dtype=np.float64) / half)
    ang = np.concatenate([row[:, None] * inv, col[:, None] * inv], axis=-1)
    cos = np.repeat(np.cos(ang), 2, axis=-1).astype(np.float32)
    sign = np.tile(np.asarray([-1.0, 1.0]), half)
    sin = (np.repeat(np.sin(ang), 2, axis=-1) * sign).astype(np.float32)
    return cos, sin


def _gqa_prep_kernel(p_ref, cos_ref, sin_ref, qg_ref, kg_ref, qt_ref, k_ref, vt_ref):
    cos = cos_ref[...]
    sin = sin_ref[...]
    even = (lax.broadcasted_iota(jnp.int32, cos.shape, 1) % 2) == 0

    def norm_rope(xh, g):
        y = _rms(xh, g)
        partner = jnp.where(even, pltpu.roll(y, LANES - 1, 1), pltpu.roll(y, 1, 1))
        return y * cos + partner * sin

    scale = HEAD_DIM ** -0.5 * LOG2_E
    for hq in range(B_Q_HEADS):
        xh = p_ref[:, hq * HEAD_DIM:(hq + 1) * HEAD_DIM]
        y = norm_rope(xh, qg_ref[...]) * scale
        qt_ref[hq * HEAD_DIM:(hq + 1) * HEAD_DIM, :] = y.T.astype(BF16)
    k0 = B_Q_HEADS * HEAD_DIM
    v0 = k0 + B_KV_HEADS * HEAD_DIM
    for hk in range(B_KV_HEADS):
        xh = p_ref[:, k0 + hk * HEAD_DIM:k0 + (hk + 1) * HEAD_DIM]
        k_ref[:, hk * HEAD_DIM:(hk + 1) * HEAD_DIM] = norm_rope(xh, kg_ref[...]).astype(BF16)
        vh = p_ref[:, v0 + hk * HEAD_DIM:v0 + (hk + 1) * HEAD_DIM]
        vt_ref[hk * HEAD_DIM:(hk + 1) * HEAD_DIM, :] = vh.T.astype(BF16)


def gqa_prep(proj, q_g, k_g, seq, col_block, ts=512):
    wb = (B_Q_HEADS + 2 * B_KV_HEADS) * HEAD_DIM
    cos, sin = _rope_tables(seq)
    return pl.pallas_call(
        _gqa_prep_kernel,
        grid=(seq // ts,),
        in_specs=[pl.BlockSpec((ts, wb), lambda i: (i, col_block)),
                  pl.BlockSpec((ts, HEAD_DIM), lambda i: (i, 0)),
                  pl.BlockSpec((ts, HEAD_DIM), lambda i: (i, 0)),
                  pl.BlockSpec((1, HEAD_DIM), lambda i: (0, 0)),
                  pl.BlockSpec((1, HEAD_DIM), lambda i: (0, 0))],
        out_specs=[pl.BlockSpec((B_Q_HEADS * HEAD_DIM, ts), lambda i: (0, i)),
                   pl.BlockSpec((ts, B_KV_HEADS * HEAD_DIM), lambda i: (i, 0)),
                   pl.BlockSpec((B_KV_HEADS * HEAD_DIM, ts), lambda i: (0, i))],
        out_shape=[jax.ShapeDtypeStruct((B_Q_HEADS * HEAD_DIM, seq), BF16),
                   jax.ShapeDtypeStruct((seq, B_KV_HEADS * HEAD_DIM), BF16),
                   jax.ShapeDtypeStruct((B_KV_HEADS * HEAD_DIM, seq), BF16)],
        compiler_params=_params(("parallel",)),
        name="gqa_prep",
    )(proj, cos, sin, q_g.reshape(1, HEAD_DIM), k_g.reshape(1, HEAD_DIM))


STAGE_ROWS = 32
GQA_UNROLL = 1


def _data_dependent_zero(tiles):
    bits = None
    for t in tiles:
        b = pltpu.bitcast(t, jnp.uint32)
        bits = b if bits is None else bits | b
    zero_words = (bits >> 16) >> 16
    return zero_words.astype(jnp.int32).astype(F32).astype(BF16)


def _gqa_kernel(qt_ref, k_ref, vt_ref, o_ref, m_ref, acc_ref, st_ref, ks_ref, *, seq, tk, nh):
    n_chunks = seq // tk
    tq = qt_ref.shape[1]
    m_ref[...] = jnp.full(m_ref.shape, NEG_BIG, F32)
    acc_ref[...] = jnp.zeros(acc_ref.shape, F32)

    def q_head(hh):
        return qt_ref[hh * HEAD_DIM:(hh + 1) * HEAD_DIM, :]

    def chunk_start(c):
        return pl.multiple_of(jnp.minimum(c, n_chunks - 1) * tk, tk)

    st_ref[...] = jnp.dot(k_ref[pl.ds(0, tk), :], q_head(0), preferred_element_type=F32)

    def chunk_items(c, st):
        here = chunk_start(c)
        vt = jnp.concatenate([vt_ref[:, pl.ds(here, tk)], jnp.ones((BF16_ROWS, tk), BF16)], axis=0)
        for hh in range(nh):
            m_old = m_ref[hh]
            m_new = jnp.maximum(m_old, jnp.max(st, axis=0, keepdims=True))
            alpha = jnp.exp2(m_old - m_new)
            pb = jnp.exp2(st - m_new).astype(BF16)

            nxt = here if hh + 1 < nh else chunk_start(c + 1)
            slot = ks_ref.at[hh % 2]
            for r in range(0, tk, STAGE_ROWS):
                strip = [pb[r:r + STAGE_ROWS, j:j + LANES] for j in range(0, tq, LANES)]
                zero = _data_dependent_zero(strip)
                piece = k_ref[pl.ds(pl.multiple_of(nxt + r, STAGE_ROWS), STAGE_ROWS), :]
                slot[r:r + STAGE_ROWS, :] = piece + pltpu.repeat(zero, STAGE_ROWS // zero.shape[0], axis=0)
            st = jnp.dot(slot[...], q_head((hh + 1) % nh), preferred_element_type=F32)

            acc_ref[hh] = alpha * acc_ref[hh] + jnp.dot(vt, pb, preferred_element_type=F32)
            m_ref[hh] = m_new
        return st

    def body(i, carry):
        st = st_ref[...]
        for u in range(GQA_UNROLL):
            st = chunk_items(i * GQA_UNROLL + u, st)
        st_ref[...] = st
        return carry

    lax.fori_loop(0, n_chunks // GQA_UNROLL, body, 0)
    for hh in range(nh):
        acc = acc_ref[hh]
        out = acc[:HEAD_DIM] / acc[HEAD_DIM:HEAD_DIM + 1]
        o_ref[:, hh * HEAD_DIM:(hh + 1) * HEAD_DIM] = out.T.astype(o_ref.dtype)


def gqa_attention(qt, k, vt, seq, tq=512, tk=512):
    nh = B_Q_HEADS // B_KV_HEADS
    return pl.pallas_call(
        functools.partial(_gqa_kernel, seq=seq, tk=tk, nh=nh),
        grid=(B_KV_HEADS, seq // tq),
        in_specs=[pl.BlockSpec((nh * HEAD_DIM, tq), lambda g, i: (g, i)),
                  pl.BlockSpec((seq, HEAD_DIM), lambda g, i: (0, g)),
                  pl.BlockSpec((HEAD_DIM, seq), lambda g, i: (g, 0))],
        out_specs=pl.BlockSpec((tq, nh * HEAD_DIM), lambda g, i: (i, g)),
        out_shape=jax.ShapeDtypeStruct((seq, B_Q_HEADS * HEAD_DIM), BF16),
        scratch_shapes=[pltpu.VMEM((nh, 1, tq), F32),
                        pltpu.VMEM((nh, HEAD_DIM + BF16_ROWS, tq), F32),
                        pltpu.VMEM((tk, tq), F32),
                        pltpu.VMEM((2, tk, HEAD_DIM), BF16)],
        compiler_params=_params(("parallel", "arbitrary")),
        name="gqa_attention",
    )(qt, k, vt)


def _outproj_kernel(ma_ref, mb_ref, wa_ref, wb_ref, x_ref, g_ref, x1_ref, ht_ref):
    y = x_ref[...]
    y = y + jnp.dot(ma_ref[...], wa_ref[...], preferred_element_type=F32)
    y = y + jnp.dot(mb_ref[...], wb_ref[...], preferred_element_type=F32)
    x1_ref[...] = y
    ht_ref[...] = _rms(y, g_ref[...]).T.astype(BF16)


def out_projection(mixed_a, mixed_b, w, x, g, tm=512):
    s, d = x.shape
    wa, wb = mixed_a.shape[1], mixed_b.shape[1]
    assert wa == wb and w.shape[0] == wa + wb
    return pl.pallas_call(
        _outproj_kernel,
        grid=(s // tm,),
        in_specs=[pl.BlockSpec((tm, wa), lambda i: (i, 0)),
                  pl.BlockSpec((tm, wb), lambda i: (i, 0)),
                  pl.BlockSpec((wa, d), lambda i: (0, 0)),
                  pl.BlockSpec((wb, d), lambda i: (1, 0)),
                  pl.BlockSpec((tm, d), lambda i: (i, 0)),
                  pl.BlockSpec((1, d), lambda i: (0, 0))],
        out_specs=[pl.BlockSpec((tm, d), lambda i: (i, 0)),
                   pl.BlockSpec((d, tm), lambda i: (0, i))],
        out_shape=[jax.ShapeDtypeStruct((s, d), F32),
                   jax.ShapeDtypeStruct((d, s), BF16)],
        compiler_params=_params(("parallel",)),
        name="out_projection",
    )(mixed_a, mixed_b, w, w, x, g.reshape(1, d))


N_RANK = PEER_TOPK + 1
RANK_ROWS = 24
CAND_COUNTS = tuple(N_RANK // (a + 1) for a in range(N_RANK))
CAND_ROWS = 8 * (1 << (-(-sum(CAND_COUNTS) // 8) - 1).bit_length())
ROUTER_UNROLL = 8


def _odd_even_merge_sort_pairs(n):
    pairs = []

    def merge(lo, hi, r):
        step = r * 2
        if step < hi - lo:
            merge(lo, hi, step)
            merge(lo + r, hi, step)
            pairs.extend((i, i + r) for i in range(lo + r, hi - r, step))
        else:
            pairs.append((lo, lo + r))

    def sort(lo, hi):
        if hi - lo >= 1:
            mid = lo + (hi - lo) // 2
            sort(lo, mid)
            sort(mid + 1, hi)
            merge(lo, hi, 1)

    sort(0, n - 1)
    return pairs


def _extract_sorted_tile(tile, n, emit):
    depth = tile.shape[0] // SUBLANES
    lists = [tile[SUBLANES * j:SUBLANES * (j + 1)] for j in range(depth)]
    for i, j in _odd_even_merge_sort_pairs(depth):
        lists[i], lists[j] = jnp.maximum(lists[i], lists[j]), jnp.minimum(lists[i], lists[j])
    for r in range(n):
        mj = jnp.max(lists[0], axis=0, keepdims=True)
        emit(r, mj)
        took = lists[0] == mj
        for j in range(min(depth, n - (r + 1))):
            below = lists[j + 1] if j + 1 < depth else -jnp.inf
            lists[j] = jnp.where(took, below, lists[j])


def _router_kernel(ht_ref, wq_ref, keys_ref, thr_ref, w0_ref, s1_ref, e1_ref,
                   q_scr, s_scr, top_scr, cand_scr, best_scr, *, tb):
    q_scr[...] = lax.dot_general(wq_ref[...], ht_ref[...], (((0,), (0,)), ((), ())),
                                 preferred_element_type=F32).astype(BF16)
    lane_groups = [slice(lg * LANES, (lg + 1) * LANES) for lg in range(tb // LANES)]

    def half_body(hc):
        qhc = q_scr[pl.ds(pl.multiple_of(hc * PEER_N_KEYS, PEER_N_KEYS), PEER_N_KEYS), :]
        s = jnp.dot(keys_ref[hc], qhc, preferred_element_type=F32)
        s_scr[hc] = s
        for ls in lane_groups:
            def emit(j, mj, ls=ls):
                top_scr[hc, pl.ds(j, 1), ls] = mj

            _extract_sorted_tile(s[:, ls], N_RANK, emit)

    def half_group(i, carry):
        for u in range(ROUTER_UNROLL):
            half_body(i * ROUTER_UNROLL + u)
        return carry

    lax.fori_loop(0, 2 * PEER_HEADS // ROUTER_UNROLL, half_group, 0)

    def head_body(h, carry):
        t0 = top_scr[2 * h]
        t1 = top_scr[2 * h + 1]
        cand_scr[...] = jnp.full(cand_scr.shape, -jnp.inf, F32)
        off = 0
        for a, nb in enumerate(CAND_COUNTS):
            cand_scr[off:off + nb, :] = t0[a:a + 1, :] + t1[0:nb, :]
            off += nb

        for ls in lane_groups:
            def emit(j, mj, ls=ls):
                best_scr[pl.ds(j, 1), ls] = mj

            _extract_sorted_tile(cand_scr[:, ls], N_RANK, emit)
        best = [best_scr[j:j + 1, :] for j in range(N_RANK)]
        theta = 0.5 * (best[PEER_TOPK - 1] + best[PEER_TOPK])
        z = jnp.zeros_like(theta)
        for j in range(PEER_TOPK):
            z = z + jnp.exp(best[j] - best[0])

        s0 = s_scr[2 * h]
        s1 = s_scr[2 * h + 1]
        thr_ref[h] = theta - s0
        w0_ref[h] = jnp.exp(s0 - t0[0:1, :]) * (0.5 / z)
        e1 = jnp.exp(s1 - t1[0:1, :])
        for lg, ls in enumerate(lane_groups):
            s1_ref[h, lg] = s1[:, ls]
            e1_ref[h, lg] = e1[:, ls]
        return carry

    lax.fori_loop(0, PEER_HEADS, head_body, 0)


def peer_router(ht, wq, keys, tb=512):
    d, s = ht.shape
    nq = wq.shape[1]
    tab = jax.ShapeDtypeStruct((PEER_HEADS, PEER_N_KEYS, s), F32)
    tab_spec = pl.BlockSpec((PEER_HEADS, PEER_N_KEYS, tb), lambda i: (0, 0, i))
    tile = jax.ShapeDtypeStruct((PEER_HEADS, s // LANES, PEER_N_KEYS, LANES), F32)
    tile_spec = pl.BlockSpec((PEER_HEADS, tb // LANES, PEER_N_KEYS, LANES), lambda i: (0, i, 0, 0))
    return pl.pallas_call(
        functools.partial(_router_kernel, tb=tb),
        grid=(s // tb,),
        in_specs=[pl.BlockSpec((d, tb), lambda i: (0, i)),
                  pl.BlockSpec((d, nq), lambda i: (0, 0)),
                  pl.BlockSpec(keys.shape, lambda i: (0, 0, 0))],
        out_specs=[tab_spec, tab_spec, tile_spec, tile_spec],
        out_shape=[tab, tab, tile, tile],
        scratch_shapes=[pltpu.VMEM((nq, tb), BF16),
                        pltpu.VMEM((2 * PEER_HEADS, PEER_N_KEYS, tb), F32),
                        pltpu.VMEM((2 * PEER_HEADS, RANK_ROWS, tb), F32),
                        pltpu.VMEM((CAND_ROWS, tb), F32),
                        pltpu.VMEM((RANK_ROWS, tb), F32)],
        compiler_params=_params(("parallel",)),
        name="peer_router",
    )(ht, wq, keys)


GELU_C = 0.7978845608028654


def _gelu_tanh_x2(x):
    u = x * (GELU_C + (GELU_C * 0.044715) * (x * x))
    return x + x * jnp.tanh(u)


PEER_SUB = 256
GATE_ROWS = 32


def _peer_stage(ht_ref, u_ref, v_ref, thr_ref, w0_ref, s1_ref, e1_ref, o_ref,
                at_w, at_r, pt_w, pt_r, *, te, tb):
    d = o_ref.shape[1]
    n_sub = te // PEER_SUB
    col_w = d // n_sub

    def hidden_piece(j):
        rows = slice(j * PEER_SUB, (j + 1) * PEER_SUB)
        at = jnp.dot(u_ref[rows, :], ht_ref[...], preferred_element_type=F32)
        for lg in range(tb // LANES):
            at_w[lg, rows, :] = at[:, lg * LANES:(lg + 1) * LANES]

    def value_piece(c):
        cols = slice(c * col_w, (c + 1) * col_w)
        o_ref[:, cols] += lax.dot_general(pt_r[...], v_ref[:, cols], (((0,), (0,)), ((), ())),
                                          preferred_element_type=F32)

    n_i0 = te // PEER_N_KEYS
    n_q = PEER_N_KEYS // GATE_ROWS

    def gate_piece(lg, q):
        ls = slice(lg * LANES, (lg + 1) * LANES)
        qs = slice(q * GATE_ROWS, (q + 1) * GATE_ROWS)
        gates = [None] * n_i0
        for h in range(PEER_HEADS):
            s1 = s1_ref[h, lg, qs, :]
            e1 = e1_ref[h, lg, qs, :]
            for i0 in range(n_i0):
                thr = thr_ref[h, i0:i0 + 1, ls]
                w0 = w0_ref[h, i0:i0 + 1, ls]
                term = jnp.where(s1 >= thr, e1 * w0, 0.0)
                gates[i0] = term if gates[i0] is None else gates[i0] + term
        for i0 in range(n_i0):
            er = slice(i0 * PEER_N_KEYS + q * GATE_ROWS, i0 * PEER_N_KEYS + (q + 1) * GATE_ROWS)
            pt_w[er, ls] = (_gelu_tanh_x2(at_r[lg, er, :]) * gates[i0]).astype(BF16)

    mxu_pieces = [functools.partial(hidden_piece, j) for j in range(n_sub)]
    mxu_pieces += [functools.partial(value_piece, c) for c in range(n_sub)]
    vec_pieces = [(lg, q) for lg in range(tb // LANES) for q in range(n_q)]
    assert len(vec_pieces) % len(mxu_pieces) == 0
    per_mxu = len(vec_pieces) // len(mxu_pieces)
    for n, piece in enumerate(mxu_pieces):
        for lg, q in vec_pieces[n * per_mxu:(n + 1) * per_mxu]:
            gate_piece(lg, q)
        piece()


def _peer_kernel(ht_ref, u_ref, v_ref, thr_ref, w0_ref, s1_ref, e1_ref, x1_ref, g_ref,
                 o_ref, at0_ref, at1_ref, pt0_ref, pt1_ref, *, te, tb, n_e):
    g = pl.program_id(0)
    e_out = (g - 2) % n_e
    out_live = g >= 2
    at_refs = (at0_ref, at1_ref)
    pt_refs = (pt0_ref, pt1_ref)

    @pl.when(g == 0)
    def _():
        for r in at_refs + pt_refs:
            r[...] = jnp.zeros(r.shape, r.dtype)

    @pl.when((g == 0) | (out_live & (e_out == 0)))
    def _():
        o_ref[...] = x1_ref[...]

    for parity in range(2):
        @pl.when(g % 2 == parity)
        def _(parity=parity):
            _peer_stage(ht_ref, u_ref, v_ref, thr_ref, w0_ref, s1_ref, e1_ref, o_ref,
                        at_refs[parity], at_refs[1 - parity],
                        pt_refs[1 - parity], pt_refs[parity], te=te, tb=tb)

    @pl.when(out_live & (e_out == n_e - 1))
    def _():
        o_ref[...] = _rms(o_ref[...], g_ref[...])


def peer_experts(ht, u, v, tables, x1, g, tb=512, te=1024):
    d, s = ht.shape
    n_exp = u.shape[0]
    n_i0 = te // PEER_N_KEYS
    n_e = n_exp // te
    n_items = (s // tb) * n_e
    thr, w0, s1, e1 = tables

    def item(gi, lag):
        n = jnp.clip(gi - lag, 0, n_items - 1)
        return n // n_e, n % n_e

    row_spec = pl.BlockSpec((PEER_HEADS, n_i0, tb), lambda gi: (0, item(gi, 1)[1], item(gi, 1)[0]))
    tab_spec = pl.BlockSpec((PEER_HEADS, tb // LANES, PEER_N_KEYS, LANES),
                            lambda gi: (0, item(gi, 1)[0], 0, 0))
    return pl.pallas_call(
        functools.partial(_peer_kernel, te=te, tb=tb, n_e=n_e),
        grid=(n_items + 2,),
        in_specs=[pl.BlockSpec((d, tb), lambda gi: (0, item(gi, 0)[0])),
                  pl.BlockSpec((te, d), lambda gi: (item(gi, 0)[1], 0)),
                  pl.BlockSpec((te, d), lambda gi: (item(gi, 2)[1], 0)),
                  row_spec, row_spec, tab_spec, tab_spec,
                  pl.BlockSpec((tb, d), lambda gi: (item(gi, 2)[0], 0),
                               pipeline_mode=pl.Buffered(1)),
                  pl.BlockSpec((1, d), lambda gi: (0, 0))],
        out_specs=pl.BlockSpec((tb, d), lambda gi: (item(gi, 2)[0], 0)),
        out_shape=jax.ShapeDtypeStruct((s, d), F32),
        scratch_shapes=[pltpu.VMEM((tb // LANES, te, LANES), F32),
                        pltpu.VMEM((tb // LANES, te, LANES), F32),
                        pltpu.VMEM((te, tb), BF16),
                        pltpu.VMEM((te, tb), BF16)],
        compiler_params=_params(("arbitrary",)),
        name="peer_experts",
    )(ht, u, v, thr, w0, s1, e1, x1, g.reshape(1, d))


def kernel(x, norm1_g, w_in, q_norm_g, k_norm_g, w_out, norm2_g, peer_w_query,
           peer_sub_keys, peer_u, peer_v, final_norm_g):
    b, s, d = x.shape
    assert b == 1
    depth = w_in.shape[0]
    xs = x.reshape(s, d)
    for l in range(depth):
        proj = in_projection(xs, norm1_g[l], w_in[l].astype(BF16))
        mixed_a, (u_bf16, v_bf16, w_o, w_q) = dilated_attention(
            proj, s, to_bf16=(peer_u[l], peer_v[l], w_out[l], peer_w_query[l]))
        qt, kb, vt = gqa_prep(proj, q_norm_g[l], k_norm_g[l], s, col_block=2)
        mixed_b = gqa_attention(qt, kb, vt, s)
        x1, ht = out_projection(mixed_a, mixed_b, w_o, xs, norm2_g[l])
        keys = peer_sub_keys[l].reshape(2 * PEER_HEADS, PEER_N_KEYS, -1).astype(BF16)
        tables = peer_router(ht, w_q, keys)
        assert l == depth - 1
        xs = peer_experts(ht, u_bf16, v_bf16, tables, x1, final_norm_g)
    return xs.reshape(b, s, d)
```

```python
import functools

import numpy as np
import jax
import jax.numpy as jnp
from jax import lax
from jax.experimental import pallas as pl
from jax.experimental.pallas import tpu as pltpu

F32 = jnp.float32
BF16 = jnp.bfloat16

HEAD_DIM = 128
A_HEADS = 8
DIL_PATTERNS = ((128, 1), (512, 4), (2048, 16))
B_Q_HEADS = 8
B_KV_HEADS = 2
GRID_W = 64
ROPE_THETA = 10000.0
NORM_EPS = 1e-6
PEER_HEADS = 8
PEER_N_KEYS = 128
PEER_TOPK = 16

LANES = 128
SUBLANES = 8
BF16_ROWS = 16
NEG_BIG = -1e30
LOG2_E = 1.4426950408889634
VMEM_LIMIT = 56 * 1024 * 1024


def _params(sem, vmem=VMEM_LIMIT):
    return pltpu.CompilerParams(dimension_semantics=sem, vmem_limit_bytes=vmem)


def _rms(x, g):
    ms = jnp.mean(x * x, axis=-1, keepdims=True)
    return x * lax.rsqrt(ms + NORM_EPS) * g


def _inproj_kernel(x_ref, g_ref, w_ref, o_ref, h_ref):
    @pl.when(pl.program_id(1) == 0)
    def _():
        h_ref[...] = _rms(x_ref[...], g_ref[...]).astype(BF16)

    o_ref[...] = jnp.dot(h_ref[...], w_ref[...].astype(BF16), preferred_element_type=F32)


def in_projection(x, g, w, tm=1024, tn=512):
    s, d = x.shape
    n = w.shape[1]
    return pl.pallas_call(
        _inproj_kernel,
        grid=(s // tm, n // tn),
        in_specs=[pl.BlockSpec((tm, d), lambda i, j: (i, 0)),
                  pl.BlockSpec((1, d), lambda i, j: (0, 0)),
                  pl.BlockSpec((d, tn), lambda i, j: (0, j))],
        out_specs=pl.BlockSpec((tm, tn), lambda i, j: (i, j)),
        out_shape=jax.ShapeDtypeStruct((s, n), F32),
        scratch_shapes=[pltpu.VMEM((tm, d), BF16)],
        compiler_params=_params(("parallel", "arbitrary")),
        name="in_projection",
    )(x, g.reshape(1, d), w)


DIL_QBLK = 2048
DIL_SUB = 128
DIL_KWIN = 256
DIL_UNROLL = 16
DIL_EDGE_CASES = 3


def _dilated_kernel(q_ref, k_ref, v_ref, *rest, seq, n_cast):
    cast_src, rest = rest[:n_cast], rest[n_cast:]
    o_ref, cast_dst, rest = rest[0], rest[1:1 + n_cast], rest[1 + n_cast:]
    m_ref, l_ref, acc_ref, bias_ref = rest
    for src, dst in zip(cast_src, cast_dst):
        dst[...] = src[...].astype(dst.dtype)

    h = pl.program_id(0)
    base = pl.program_id(1) * DIL_QBLK
    slope = jnp.exp2(-(h + 1).astype(F32)) * LOG2_E
    scale = HEAD_DIM ** -0.5 * LOG2_E

    m_ref[...] = jnp.full(m_ref.shape, NEG_BIG, F32)
    l_ref[...] = jnp.zeros(l_ref.shape, F32)
    acc_ref[...] = jnp.zeros(acc_ref.shape, F32)

    row = lax.broadcasted_iota(jnp.int32, (DIL_SUB, DIL_KWIN), 0)
    col = lax.broadcasted_iota(jnp.int32, (DIL_SUB, DIL_KWIN), 1)
    col_minus_row = col - row

    for b_idx, (w, d) in enumerate(DIL_PATTERNS):
        half = (w // 2) // d
        assert half * 2 + DIL_SUB == DIL_KWIN
        seg_len = seq // d

        for v_idx in range(DIL_EDGE_CASES):
            rel = jnp.abs(col_minus_row - v_idx * half)
            bias_ref[b_idx * DIL_EDGE_CASES + v_idx] = jnp.where(
                rel <= half, -(slope * d) * rel.astype(F32), NEG_BIG)

        def rows(start, n, d=d):
            return pl.ds(start, n, stride=d) if d > 1 else pl.ds(start, n)

        def body(c, carry, d=d, half=half, seg_len=seg_len, rows=rows, b_idx=b_idx):
            r = c % d
            cc = c // d
            p0 = base // d + cc * DIL_SUB
            kp = jnp.clip(p0 - half, 0, seg_len - DIL_KWIN)
            lq = r + d * (cc * DIL_SUB)
            ks = r + d * kp

            q = (q_ref[rows(lq, DIL_SUB), :] * scale).astype(BF16)
            k = k_ref[rows(ks, DIL_KWIN), :].astype(BF16)
            v = v_ref[rows(ks, DIL_KWIN), :].astype(BF16)
            s = lax.dot_general(q, k, (((1,), (1,)), ((), ())), preferred_element_type=F32)
            s = s + bias_ref[b_idx * DIL_EDGE_CASES + (p0 - kp) // half]
            mb = jnp.max(s, axis=1, keepdims=True)
            p = jnp.exp2(s - mb)
            lb = jnp.sum(p, axis=1, keepdims=True)
            ob = jnp.dot(p.astype(BF16), v, preferred_element_type=F32)

            idx = rows(lq, DIL_SUB)
            m_old = m_ref[idx, :]
            m_new = jnp.maximum(m_old, mb)
            a_old = jnp.exp2(m_old - m_new)
            a_blk = jnp.exp2(mb - m_new)
            m_ref[idx, :] = m_new
            l_ref[idx, :] = a_old * l_ref[idx, :] + a_blk * lb
            acc_ref[idx, :] = a_old * acc_ref[idx, :] + a_blk * ob
            return carry

        def group(cg, carry, body=body):
            for u in range(DIL_UNROLL):
                body(cg * DIL_UNROLL + u, carry)
            return carry

        lax.fori_loop(0, DIL_QBLK // DIL_SUB // DIL_UNROLL, group, 0)

    o_ref[...] = (acc_ref[...] / l_ref[...]).astype(o_ref.dtype)


def dilated_attention(proj, seq, to_bf16=()):
    assert seq % DIL_QBLK == 0
    for _, d in DIL_PATTERNS:
        assert DIL_QBLK // d >= DIL_SUB and seq // d >= DIL_KWIN
    nh = A_HEADS
    n_qb = seq // DIL_QBLK
    n_steps = nh * n_qb
    cast_specs = []
    for w in to_bf16:
        assert w.shape[0] % n_steps == 0
        cast_specs.append(pl.BlockSpec((w.shape[0] // n_steps, w.shape[1]),
                                       lambda h, i: (h * n_qb + i, 0)))
    outs = pl.pallas_call(
        functools.partial(_dilated_kernel, seq=seq, n_cast=len(to_bf16)),
        grid=(nh, n_qb),
        in_specs=[pl.BlockSpec((DIL_QBLK, HEAD_DIM), lambda h, i: (i, h)),
                  pl.BlockSpec((seq, HEAD_DIM), lambda h, i: (0, nh + h)),
                  pl.BlockSpec((seq, HEAD_DIM), lambda h, i: (0, 2 * nh + h))] + cast_specs,
        out_specs=[pl.BlockSpec((DIL_QBLK, HEAD_DIM), lambda h, i: (i, h))] + cast_specs,
        out_shape=[jax.ShapeDtypeStruct((seq, nh * HEAD_DIM), BF16)]
        + [jax.ShapeDtypeStruct(w.shape, BF16) for w in to_bf16],
        scratch_shapes=[pltpu.VMEM((DIL_QBLK, LANES), F32),
                        pltpu.VMEM((DIL_QBLK, LANES), F32),
                        pltpu.VMEM((DIL_QBLK, HEAD_DIM), F32),
                        pltpu.VMEM((len(DIL_PATTERNS) * DIL_EDGE_CASES, DIL_SUB, DIL_KWIN), F32)],
        compiler_params=_params(("parallel", "arbitrary")),
        name="dilated_attention",
    )(proj, proj, proj, *to_bf16)
    return outs[0], outs[1:]


def _rope_tables(seq):
    rows = seq // GRID_W
    row = np.repeat(np.arange(rows, dtype=np.float64), GRID_W)
    col = np.tile(np.arange(GRID_W, dtype=np.float64), rows)
    half = HEAD_DIM // 2
    inv = ROPE_THETA ** (-np.arange(0, half, 2, dtype=np.float64) / half)
    ang = np.concatenate([row[:, None] * inv, col[:, None] * inv], axis=-1)
    cos = np.repeat(np.cos(ang), 2, axis=-1).astype(np.float32)
    sign = np.tile(np.asarray([-1.0, 1.0]), half)
    sin = (np.repeat(np.sin(ang), 2, axis=-1) * sign).astype(np.float32)
    return cos, sin


def _gqa_prep_kernel(p_ref, cos_ref, sin_ref, qg_ref, kg_ref, qt_ref, k_ref, vt_ref):
    cos = cos_ref[...]
    sin = sin_ref[...]
    even = (lax.broadcasted_iota(jnp.int32, cos.shape, 1) % 2) == 0

    def norm_rope(xh, g):
        y = _rms(xh, g)
        partner = jnp.where(even, pltpu.roll(y, LANES - 1, 1), pltpu.roll(y, 1, 1))
        return y * cos + partner * sin

    scale = HEAD_DIM ** -0.5 * LOG2_E
    for hq in range(B_Q_HEADS):
        xh = p_ref[:, hq * HEAD_DIM:(hq + 1) * HEAD_DIM]
        y = norm_rope(xh, qg_ref[...]) * scale
        qt_ref[hq * HEAD_DIM:(hq + 1) * HEAD_DIM, :] = y.T.astype(BF16)
    k0 = B_Q_HEADS * HEAD_DIM
    v0 = k0 + B_KV_HEADS * HEAD_DIM
    for hk in range(B_KV_HEADS):
        xh = p_ref[:, k0 + hk * HEAD_DIM:k0 + (hk + 1) * HEAD_DIM]
        k_ref[:, hk * HEAD_DIM:(hk + 1) * HEAD_DIM] = norm_rope(xh, kg_ref[...]).astype(BF16)
        vh = p_ref[:, v0 + hk * HEAD_DIM:v0 + (hk + 1) * HEAD_DIM]
        vt_ref[hk * HEAD_DIM:(hk + 1) * HEAD_DIM, :] = vh.T.astype(BF16)


def gqa_prep(proj, q_g, k_g, seq, col_block, ts=512):
    wb = (B_Q_HEADS + 2 * B_KV_HEADS) * HEAD_DIM
    cos, sin = _rope_tables(seq)
    return pl.pallas_call(
        _gqa_prep_kernel,
        grid=(seq // ts,),
        in_specs=[pl.BlockSpec((ts, wb), lambda i: (i, col_block)),
                  pl.BlockSpec((ts, HEAD_DIM), lambda i: (i, 0)),
                  pl.BlockSpec((ts, HEAD_DIM), lambda i: (i, 0)),
                  pl.BlockSpec((1, HEAD_DIM), lambda i: (0, 0)),
                  pl.BlockSpec((1, HEAD_DIM), lambda i: (0, 0))],
        out_specs=[pl.BlockSpec((B_Q_HEADS * HEAD_DIM, ts), lambda i: (0, i)),
                   pl.BlockSpec((ts, B_KV_HEADS * HEAD_DIM), lambda i: (i, 0)),
                   pl.BlockSpec((B_KV_HEADS * HEAD_DIM, ts), lambda i: (0, i))],
        out_shape=[jax.ShapeDtypeStruct((B_Q_HEADS * HEAD_DIM, seq), BF16),
                   jax.ShapeDtypeStruct((seq, B_KV_HEADS * HEAD_DIM), BF16),
                   jax.ShapeDtypeStruct((B_KV_HEADS * HEAD_DIM, seq), BF16)],
        compiler_params=_params(("parallel",)),
        name="gqa_prep",
    )(proj, cos, sin, q_g.reshape(1, HEAD_DIM), k_g.reshape(1, HEAD_DIM))


STAGE_ROWS = 32
GQA_UNROLL = 1


def _data_dependent_zero(tiles):
    bits = None
    for t in tiles:
        b = pltpu.bitcast(t, jnp.uint32)
        bits = b if bits is None else bits | b
    zero_words = (bits >> 16) >> 16
    return zero_words.astype(jnp.int32).astype(F32).astype(BF16)


def _gqa_kernel(qt_ref, k_ref, vt_ref, o_ref, m_ref, acc_ref, st_ref, ks_ref, *, seq, tk, nh):
    n_chunks = seq // tk
    tq = qt_ref.shape[1]
    m_ref[...] = jnp.full(m_ref.shape, NEG_BIG, F32)
    acc_ref[...] = jnp.zeros(acc_ref.shape, F32)

    def q_head(hh):
        return qt_ref[hh * HEAD_DIM:(hh + 1) * HEAD_DIM, :]

    def chunk_start(c):
        return pl.multiple_of(jnp.minimum(c, n_chunks - 1) * tk, tk)

    st_ref[...] = jnp.dot(k_ref[pl.ds(0, tk), :], q_head(0), preferred_element_type=F32)

    def chunk_items(c, st):
        here = chunk_start(c)
        vt = jnp.concatenate([vt_ref[:, pl.ds(here, tk)], jnp.ones((BF16_ROWS, tk), BF16)], axis=0)
        for hh in range(nh):
            m_old = m_ref[hh]
            m_new = jnp.maximum(m_old, jnp.max(st, axis=0, keepdims=True))
            alpha = jnp.exp2(m_old - m_new)
            pb = jnp.exp2(st - m_new).astype(BF16)

            nxt = here if hh + 1 < nh else chunk_start(c + 1)
            slot = ks_ref.at[hh % 2]
            for r in range(0, tk, STAGE_ROWS):
                strip = [pb[r:r + STAGE_ROWS, j:j + LANES] for j in range(0, tq, LANES)]
                zero = _data_dependent_zero(strip)
                piece = k_ref[pl.ds(pl.multiple_of(nxt + r, STAGE_ROWS), STAGE_ROWS), :]
                slot[r:r + STAGE_ROWS, :] = piece + pltpu.repeat(zero, STAGE_ROWS // zero.shape[0], axis=0)
            st = jnp.dot(slot[...], q_head((hh + 1) % nh), preferred_element_type=F32)

            acc_ref[hh] = alpha * acc_ref[hh] + jnp.dot(vt, pb, preferred_element_type=F32)
            m_ref[hh] = m_new
        return st

    def body(i, carry):
        st = st_ref[...]
        for u in range(GQA_UNROLL):
            st = chunk_items(i * GQA_UNROLL + u, st)
        st_ref[...] = st
        return carry

    lax.fori_loop(0, n_chunks // GQA_UNROLL, body, 0)
    for hh in range(nh):
        acc = acc_ref[hh]
        out = acc[:HEAD_DIM] / acc[HEAD_DIM:HEAD_DIM + 1]
        o_ref[:, hh * HEAD_DIM:(hh + 1) * HEAD_DIM] = out.T.astype(o_ref.dtype)


def gqa_attention(qt, k, vt, seq, tq=512, tk=512):
    nh = B_Q_HEADS // B_KV_HEADS
    return pl.pallas_call(
        functools.partial(_gqa_kernel, seq=seq, tk=tk, nh=nh),
        grid=(B_KV_HEADS, seq // tq),
        in_specs=[pl.BlockSpec((nh * HEAD_DIM, tq), lambda g, i: (g, i)),
                  pl.BlockSpec((seq, HEAD_DIM), lambda g, i: (0, g)),
                  pl.BlockSpec((HEAD_DIM, seq), lambda g, i: (g, 0))],
        out_specs=pl.BlockSpec((tq, nh * HEAD_DIM), lambda g, i: (i, g)),
        out_shape=jax.ShapeDtypeStruct((seq, B_Q_HEADS * HEAD_DIM), BF16),
        scratch_shapes=[pltpu.VMEM((nh, 1, tq), F32),
                        pltpu.VMEM((nh, HEAD_DIM + BF16_ROWS, tq), F32),
                        pltpu.VMEM((tk, tq), F32),
                        pltpu.VMEM((2, tk, HEAD_DIM), BF16)],
        compiler_params=_params(("parallel", "arbitrary")),
        name="gqa_attention",
    )(qt, k, vt)


def _outproj_kernel(ma_ref, mb_ref, wa_ref, wb_ref, x_ref, g_ref, x1_ref, ht_ref):
    y = x_ref[...]
    y = y + jnp.dot(ma_ref[...], wa_ref[...], preferred_element_type=F32)
    y = y + jnp.dot(mb_ref[...], wb_ref[...], preferred_element_type=F32)
    x1_ref[...] = y
    ht_ref[...] = _rms(y, g_ref[...]).T.astype(BF16)


def out_projection(mixed_a, mixed_b, w, x, g, tm=512):
    s, d = x.shape
    wa, wb = mixed_a.shape[1], mixed_b.shape[1]
    assert wa == wb and w.shape[0] == wa + wb
    return pl.pallas_call(
        _outproj_kernel,
        grid=(s // tm,),
        in_specs=[pl.BlockSpec((tm, wa), lambda i: (i, 0)),
                  pl.BlockSpec((tm, wb), lambda i: (i, 0)),
                  pl.BlockSpec((wa, d), lambda i: (0, 0)),
                  pl.BlockSpec((wb, d), lambda i: (1, 0)),
                  pl.BlockSpec((tm, d), lambda i: (i, 0)),
                  pl.BlockSpec((1, d), lambda i: (0, 0))],
        out_specs=[pl.BlockSpec((tm, d), lambda i: (i, 0)),
                   pl.BlockSpec((d, tm), lambda i: (0, i))],
        out_shape=[jax.ShapeDtypeStruct((s, d), F32),
                   jax.ShapeDtypeStruct((d, s), BF16)],
        compiler_params=_params(("parallel",)),
        name="out_projection",
    )(mixed_a, mixed_b, w, w, x, g.reshape(1, d))


N_RANK = PEER_TOPK + 1
RANK_ROWS = 24
CAND_COUNTS = tuple(N_RANK // (a + 1) for a in range(N_RANK))
CAND_ROWS = 8 * (1 << (-(-sum(CAND_COUNTS) // 8) - 1).bit_length())
ROUTER_UNROLL = 8


def _odd_even_merge_sort_pairs(n):
    pairs = []

    def merge(lo, hi, r):
        step = r * 2
        if step < hi - lo:
            merge(lo, hi, step)
            merge(lo + r, hi, step)
            pairs.extend((i, i + r) for i in range(lo + r, hi - r, step))
        else:
            pairs.append((lo, lo + r))

    def sort(lo, hi):
        if hi - lo >= 1:
            mid = lo + (hi - lo) // 2
            sort(lo, mid)
            sort(mid + 1, hi)
            merge(lo, hi, 1)

    sort(0, n - 1)
    return pairs


def _extract_sorted_tile(tile, n, emit):
    depth = tile.shape[0] // SUBLANES
    lists = [tile[SUBLANES * j:SUBLANES * (j + 1)] for j in range(depth)]
    for i, j in _odd_even_merge_sort_pairs(depth):
        lists[i], lists[j] = jnp.maximum(lists[i], lists[j]), jnp.minimum(lists[i], lists[j])
    for r in range(n):
        mj = jnp.max(lists[0], axis=0, keepdims=True)
        emit(r, mj)
        took = lists[0] == mj
        for j in range(min(depth, n - (r + 1))):
            below = lists[j + 1] if j + 1 < depth else -jnp.inf
            lists[j] = jnp.where(took, below, lists[j])


def _router_kernel(ht_ref, wq_ref, keys_ref, thr_ref, w0_ref, s1_ref, e1_ref,
                   q_scr, s_scr, top_scr, cand_scr, best_scr, *, tb):
    q_scr[...] = lax.dot_general(wq_ref[...], ht_ref[...], (((0,), (0,)), ((), ())),
                                 preferred_element_type=F32).astype(BF16)
    lane_groups = [slice(lg * LANES, (lg + 1) * LANES) for lg in range(tb // LANES)]

    def half_body(hc):
        qhc = q_scr[pl.ds(pl.multiple_of(hc * PEER_N_KEYS, PEER_N_KEYS), PEER_N_KEYS), :]
        s = jnp.dot(keys_ref[hc], qhc, preferred_element_type=F32)
        s_scr[hc] = s
        for ls in lane_groups:
            def emit(j, mj, ls=ls):
                top_scr[hc, pl.ds(j, 1), ls] = mj

            _extract_sorted_tile(s[:, ls], N_RANK, emit)

    def half_group(i, carry):
        for u in range(ROUTER_UNROLL):
            half_body(i * ROUTER_UNROLL + u)
        return carry

    lax.fori_loop(0, 2 * PEER_HEADS // ROUTER_UNROLL, half_group, 0)

    def head_body(h, carry):
        t0 = top_scr[2 * h]
        t1 = top_scr[2 * h + 1]
        cand_scr[...] = jnp.full(cand_scr.shape, -jnp.inf, F32)
        off = 0
        for a, nb in enumerate(CAND_COUNTS):
            cand_scr[off:off + nb, :] = t0[a:a + 1, :] + t1[0:nb, :]
            off += nb

        for ls in lane_groups:
            def emit(j, mj, ls=ls):
                best_scr[pl.ds(j, 1), ls] = mj

            _extract_sorted_tile(cand_scr[:, ls], N_RANK, emit)
        best = [best_scr[j:j + 1, :] for j in range(N_RANK)]
        theta = 0.5 * (best[PEER_TOPK - 1] + best[PEER_TOPK])
        z = jnp.zeros_like(theta)
        for j in range(PEER_TOPK):
            z = z + jnp.exp(best[j] - best[0])

        s0 = s_scr[2 * h]
        s1 = s_scr[2 * h + 1]
        thr_ref[h] = theta - s0
        w0_ref[h] = jnp.exp(s0 - t0[0:1, :]) * (0.5 / z)
        e1 = jnp.exp(s1 - t1[0:1, :])
        for lg, ls in enumerate(lane_groups):
            s1_ref[h, lg] = s1[:, ls]
            e1_ref[h, lg] = e1[:, ls]
        return carry

    lax.fori_loop(0, PEER_HEADS, head_body, 0)


def peer_router(ht, wq, keys, tb=512):
    d, s = ht.shape
    nq = wq.shape[1]
    tab = jax.ShapeDtypeStruct((PEER_HEADS, PEER_N_KEYS, s), F32)
    tab_spec = pl.BlockSpec((PEER_HEADS, PEER_N_KEYS, tb), lambda i: (0, 0, i))
    tile = jax.ShapeDtypeStruct((PEER_HEADS, s // LANES, PEER_N_KEYS, LANES), F32)
    tile_spec = pl.BlockSpec((PEER_HEADS, tb // LANES, PEER_N_KEYS, LANES), lambda i: (0, i, 0, 0))
    return pl.pallas_call(
        functools.partial(_router_kernel, tb=tb),
        grid=(s // tb,),
        in_specs=[pl.BlockSpec((d, tb), lambda i: (0, i)),
                  pl.BlockSpec((d, nq), lambda i: (0, 0)),
                  pl.BlockSpec(keys.shape, lambda i: (0, 0, 0))],
        out_specs=[tab_spec, tab_spec, tile_spec, tile_spec],
        out_shape=[tab, tab, tile, tile],
        scratch_shapes=[pltpu.VMEM((nq, tb), BF16),
                        pltpu.VMEM((2 * PEER_HEADS, PEER_N_KEYS, tb), F32),
                        pltpu.VMEM((2 * PEER_HEADS, RANK_ROWS, tb), F32),
                        pltpu.VMEM((CAND_ROWS, tb), F32),
                        pltpu.VMEM((RANK_ROWS, tb), F32)],
        compiler_params=_params(("parallel",)),
        name="peer_router",
    )(ht, wq, keys)


GELU_C = 0.7978845608028654


def _gelu_tanh_x2(x):
    u = x * (GELU_C + (GELU_C * 0.044715) * (x * x))
    return x + x * jnp.tanh(u)


PEER_SUB = 256
GATE_ROWS = 32


def _peer_stage(ht_ref, u_ref, v_ref, thr_ref, w0_ref, s1_ref, e1_ref, o_ref,
                at_w, at_r, pt_w, pt_r, *, te, tb):
    d = o_ref.shape[1]
    n_sub = te // PEER_SUB
    col_w = d // n_sub

    def hidden_piece(j):
        rows = slice(j * PEER_SUB, (j + 1) * PEER_SUB)
        at = jnp.dot(u_ref[rows, :], ht_ref[...], preferred_element_type=F32)
        for lg in range(tb // LANES):
            at_w[lg, rows, :] = at[:, lg * LANES:(lg + 1) * LANES]

    def value_piece(c):
        cols = slice(c * col_w, (c + 1) * col_w)
        o_ref[:, cols] += lax.dot_general(pt_r[...], v_ref[:, cols], (((0,), (0,)), ((), ())),
                                          preferred_element_type=F32)

    n_i0 = te // PEER_N_KEYS
    n_q = PEER_N_KEYS // GATE_ROWS

    def gate_piece(lg, q):
        ls = slice(lg * LANES, (lg + 1) * LANES)
        qs = slice(q * GATE_ROWS, (q + 1) * GATE_ROWS)
        gates = [None] * n_i0
        for h in range(PEER_HEADS):
            s1 = s1_ref[h, lg, qs, :]
            e1 = e1_ref[h, lg, qs, :]
            for i0 in range(n_i0):
                thr = thr_ref[h, i0:i0 + 1, ls]
                w0 = w0_ref[h, i0:i0 + 1, ls]
                term = jnp.where(s1 >= thr, e1 * w0, 0.0)
                gates[i0] = term if gates[i0] is None else gates[i0] + term
        for i0 in range(n_i0):
            er = slice(i0 * PEER_N_KEYS + q * GATE_ROWS, i0 * PEER_N_KEYS + (q + 1) * GATE_ROWS)
            pt_w[er, ls] = (_gelu_tanh_x2(at_r[lg, er, :]) * gates[i0]).astype(BF16)

    mxu_pieces = [functools.partial(hidden_piece, j) for j in range(n_sub)]
    mxu_pieces += [functools.partial(value_piece, c) for c in range(n_sub)]
    vec_pieces = [(lg, q) for lg in range(tb // LANES) for q in range(n_q)]
    assert len(vec_pieces) % len(mxu_pieces) == 0
    per_mxu = len(vec_pieces) // len(mxu_pieces)
    for n, piece in enumerate(mxu_pieces):
        for lg, q in vec_pieces[n * per_mxu:(n + 1) * per_mxu]:
            gate_piece(lg, q)
        piece()


def _peer_kernel(ht_ref, u_ref, v_ref, thr_ref, w0_ref, s1_ref, e1_ref, x1_ref, g_ref,
                 o_ref, at0_ref, at1_ref, pt0_ref, pt1_ref, *, te, tb, n_e):
    g = pl.program_id(0)
    e_out = (g - 2) % n_e
    out_live = g >= 2
    at_refs = (at0_ref, at1_ref)
    pt_refs = (pt0_ref, pt1_ref)

    @pl.when(g == 0)
    def _():
        for r in at_refs + pt_refs:
            r[...] = jnp.zeros(r.shape, r.dtype)

    @pl.when((g == 0) | (out_live & (e_out == 0)))
    def _():
        o_ref[...] = x1_ref[...]

    for parity in range(2):
        @pl.when(g % 2 == parity)
        def _(parity=parity):
            _peer_stage(ht_ref, u_ref, v_ref, thr_ref, w0_ref, s1_ref, e1_ref, o_ref,
                        at_refs[parity], at_refs[1 - parity],
                        pt_refs[1 - parity], pt_refs[parity], te=te, tb=tb)

    @pl.when(out_live & (e_out == n_e - 1))
    def _():
        o_ref[...] = _rms(o_ref[...], g_ref[...])


def peer_experts(ht, u, v, tables, x1, g, tb=512, te=1024):
    d, s = ht.shape
    n_exp = u.shape[0]
    n_i0 = te // PEER_N_KEYS
    n_e = n_exp // te
    n_items = (s // tb) * n_e
    thr, w0, s1, e1 = tables

    def item(gi, lag):
        n = jnp.clip(gi - lag, 0, n_items - 1)
        return n // n_e, n % n_e

    row_spec = pl.BlockSpec((PEER_HEADS, n_i0, tb), lambda gi: (0, item(gi, 1)[1], item(gi, 1)[0]))
    tab_spec = pl.BlockSpec((PEER_HEADS, tb // LANES, PEER_N_KEYS, LANES),
                            lambda gi: (0, item(gi, 1)[0], 0, 0))
    return pl.pallas_call(
        functools.partial(_peer_kernel, te=te, tb=tb, n_e=n_e),
        grid=(n_items + 2,),
        in_specs=[pl.BlockSpec((d, tb), lambda gi: (0, item(gi, 0)[0])),
                  pl.BlockSpec((te, d), lambda gi: (item(gi, 0)[1], 0)),
                  pl.BlockSpec((te, d), lambda gi: (item(gi, 2)[1], 0)),
                  row_spec, row_spec, tab_spec, tab_spec,
                  pl.BlockSpec((tb, d), lambda gi: (item(gi, 2)[0], 0),
                               pipeline_mode=pl.Buffered(1)),
                  pl.BlockSpec((1, d), lambda gi: (0, 0))],
        out_specs=pl.BlockSpec((tb, d), lambda gi: (item(gi, 2)[0], 0)),
        out_shape=jax.ShapeDtypeStruct((s, d), F32),
        scratch_shapes=[pltpu.VMEM((tb // LANES, te, LANES), F32),
                        pltpu.VMEM((tb // LANES, te, LANES), F32),
                        pltpu.VMEM((te, tb), BF16),
                        pltpu.VMEM((te, tb), BF16)],
        compiler_params=_params(("arbitrary",)),
        name="peer_experts",
    )(ht, u, v, thr, w0, s1, e1, x1, g.reshape(1, d))


def kernel(x, norm1_g, w_in, q_norm_g, k_norm_g, w_out, norm2_g, peer_w_query,
           peer_sub_keys, peer_u, peer_v, final_norm_g):
    b, s, d = x.shape
    assert b == 1
    depth = w_in.shape[0]
    xs = x.reshape(s, d)
    for l in range(depth):
        proj = in_projection(xs, norm1_g[l], w_in[l])
        mixed_a, (u_bf16, v_bf16, w_o, w_q) = dilated_attention(
            proj, s, to_bf16=(peer_u[l], peer_v[l], w_out[l], peer_w_query[l]))
        qt, kb, vt = gqa_prep(proj, q_norm_g[l], k_norm_g[l], s, col_block=2)
        mixed_b = gqa_attention(qt, kb, vt, s)
        x1, ht = out_projection(mixed_a, mixed_b, w_o, xs, norm2_g[l])
        keys = peer_sub_keys[l].reshape(2 * PEER_HEADS, PEER_N_KEYS, -1).astype(BF16)
        tables = peer_router(ht, w_q, keys)
        assert l == depth - 1
        xs = peer_experts(ht, u_bf16, v_bf16, tables, x1, final_norm_g)
    return xs.reshape(b, s, d)
```

```python
import functools

import numpy as np
import jax
import jax.numpy as jnp
from jax import lax
from jax.experimental import pallas as pl
from jax.experimental.pallas import tpu as pltpu

F32 = jnp.float32
BF16 = jnp.bfloat16

HEAD_DIM = 128
A_HEADS = 8
DIL_PATTERNS = ((128, 1), (512, 4), (2048, 16))
B_Q_HEADS = 8
B_KV_HEADS = 2
GRID_W = 64
ROPE_THETA = 10000.0
NORM_EPS = 1e-6
PEER_HEADS = 8
PEER_N_KEYS = 128
PEER_TOPK = 16

LANES = 128
SUBLANES = 8
BF16_ROWS = 16
NEG_BIG = -1e30
LOG2_E = 1.4426950408889634
VMEM_LIMIT = 56 * 1024 * 1024


def _params(sem, vmem=VMEM_LIMIT):
    return pltpu.CompilerParams(dimension_semantics=sem, vmem_limit_bytes=vmem)


def _rms(x, g):
    ms = jnp.mean(x * x, axis=-1, keepdims=True)
    return x * lax.rsqrt(ms + NORM_EPS) * g


def _inproj_kernel(x_ref, g_ref, w_ref, o_ref, h_ref):
    @pl.when(pl.program_id(1) == 0)
    def _():
        h_ref[...] = _rms(x_ref[...], g_ref[...]).astype(BF16)

    o_ref[...] = jnp.dot(h_ref[...], w_ref[...].astype(BF16), preferred_element_type=F32)


def in_projection(x, g, w, tm=1024, tn=512):
    s, d = x.shape
    n = w.shape[1]
    return pl.pallas_call(
        _inproj_kernel,
        grid=(s // tm, n // tn),
        in_specs=[pl.BlockSpec((tm, d), lambda i, j: (i, 0)),
                  pl.BlockSpec((1, d), lambda i, j: (0, 0)),
                  pl.BlockSpec((d, tn), lambda i, j: (0, j))],
        out_specs=pl.BlockSpec((tm, tn), lambda i, j: (i, j)),
        out_shape=jax.ShapeDtypeStruct((s, n), F32),
        scratch_shapes=[pltpu.VMEM((tm, d), BF16)],
        compiler_params=_params(("parallel", "arbitrary")),
        name="in_projection",
    )(x, g.reshape(1, d), w)


DIL_QBLK = 2048
DIL_SUB = 128
DIL_KWIN = 256
DIL_UNROLL = 16
DIL_EDGE_CASES = 3


def _dilated_kernel(q_ref, k_ref, v_ref, *rest, seq, n_cast):
    cast_src, rest = rest[:n_cast], rest[n_cast:]
    o_ref, cast_dst, rest = rest[0], rest[1:1 + n_cast], rest[1 + n_cast:]
    m_ref, l_ref, acc_ref, bias_ref = rest
    for src, dst in zip(cast_src, cast_dst):
        dst[...] = src[...].astype(dst.dtype)

    h = pl.program_id(0)
    base = pl.program_id(1) * DIL_QBLK
    slope = jnp.exp2(-(h + 1).astype(F32)) * LOG2_E
    scale = HEAD_DIM ** -0.5 * LOG2_E

    m_ref[...] = jnp.full(m_ref.shape, NEG_BIG, F32)
    l_ref[...] = jnp.zeros(l_ref.shape, F32)
    acc_ref[...] = jnp.zeros(acc_ref.shape, F32)

    row = lax.broadcasted_iota(jnp.int32, (DIL_SUB, DIL_KWIN), 0)
    col = lax.broadcasted_iota(jnp.int32, (DIL_SUB, DIL_KWIN), 1)
    col_minus_row = col - row

    for b_idx, (w, d) in enumerate(DIL_PATTERNS):
        half = (w // 2) // d
        assert half * 2 + DIL_SUB == DIL_KWIN
        seg_len = seq // d

        for v_idx in range(DIL_EDGE_CASES):
            rel = jnp.abs(col_minus_row - v_idx * half)
            bias_ref[b_idx * DIL_EDGE_CASES + v_idx] = jnp.where(
                rel <= half, -(slope * d) * rel.astype(F32), NEG_BIG)

        def rows(start, n, d=d):
            return pl.ds(start, n, stride=d) if d > 1 else pl.ds(start, n)

        def body(c, carry, d=d, half=half, seg_len=seg_len, rows=rows, b_idx=b_idx):
            r = c % d
            cc = c // d
            p0 = base // d + cc * DIL_SUB
            kp = jnp.clip(p0 - half, 0, seg_len - DIL_KWIN)
            lq = r + d * (cc * DIL_SUB)
            ks = r + d * kp

            q = (q_ref[rows(lq, DIL_SUB), :] * scale).astype(BF16)
            k = k_ref[rows(ks, DIL_KWIN), :].astype(BF16)
            v = v_ref[rows(ks, DIL_KWIN), :].astype(BF16)
            s = lax.dot_general(q, k, (((1,), (1,)), ((), ())), preferred_element_type=F32)
            s = s + bias_ref[b_idx * DIL_EDGE_CASES + (p0 - kp) // half]
            mb = jnp.max(s, axis=1, keepdims=True)
            p = jnp.exp2(s - mb)
            lb = jnp.sum(p, axis=1, keepdims=True)
            ob = jnp.dot(p.astype(BF16), v, preferred_element_type=F32)

            idx = rows(lq, DIL_SUB)
            m_old = m_ref[idx, :]
            m_new = jnp.maximum(m_old, mb)
            a_old = jnp.exp2(m_old - m_new)
            a_blk = jnp.exp2(mb - m_new)
            m_ref[idx, :] = m_new
            l_ref[idx, :] = a_old * l_ref[idx, :] + a_blk * lb
            acc_ref[idx, :] = a_old * acc_ref[idx, :] + a_blk * ob
            return carry

        def group(cg, carry, body=body):
            for u in range(DIL_UNROLL):
                body(cg * DIL_UNROLL + u, carry)
            return carry

        lax.fori_loop(0, DIL_QBLK // DIL_SUB // DIL_UNROLL, group, 0)

    o_ref[...] = (acc_ref[...] / l_ref[...]).astype(o_ref.dtype)


def dilated_attention(proj, seq, to_bf16=()):
    assert seq % DIL_QBLK == 0
    for _, d in DIL_PATTERNS:
        assert DIL_QBLK // d >= DIL_SUB and seq // d >= DIL_KWIN
    nh = A_HEADS
    n_qb = seq // DIL_QBLK
    n_steps = nh * n_qb
    cast_specs = []
    for w in to_bf16:
        assert w.shape[0] % n_steps == 0
        cast_specs.append(pl.BlockSpec((w.shape[0] // n_steps, w.shape[1]),
                                       lambda h, i: (h * n_qb + i, 0)))
    outs = pl.pallas_call(
        functools.partial(_dilated_kernel, seq=seq, n_cast=len(to_bf16)),
        grid=(nh, n_qb),
        in_specs=[pl.BlockSpec((DIL_QBLK, HEAD_DIM), lambda h, i: (i, h)),
                  pl.BlockSpec((seq, HEAD_DIM), lambda h, i: (0, nh + h)),
                  pl.BlockSpec((seq, HEAD_DIM), lambda h, i: (0, 2 * nh + h))] + cast_specs,
        out_specs=[pl.BlockSpec((DIL_QBLK, HEAD_DIM), lambda h, i: (i, h))] + cast_specs,
        out_shape=[jax.ShapeDtypeStruct((seq, nh * HEAD_DIM), BF16)]
        + [jax.ShapeDtypeStruct(w.shape, BF16) for w in to_bf16],
        scratch_shapes=[pltpu.VMEM((DIL_QBLK, LANES), F32),
                        pltpu.VMEM((DIL_QBLK, LANES), F32),
                        pltpu.VMEM((DIL_QBLK, HEAD_DIM), F32),
                        pltpu.VMEM((len(DIL_PATTERNS) * DIL_EDGE_CASES, DIL_SUB, DIL_KWIN), F32)],
        compiler_params=_params(("parallel", "arbitrary")),
        name="dilated_attention",
    )(proj, proj, proj, *to_bf16)
    return outs[0], outs[1:]


def _rope_tables(seq):
    rows = seq // GRID_W
    row = np.repeat(np.arange(rows, dtype=np.float64), GRID_W)
    col = np.tile(np.arange(GRID_W, dtype=np.float64), rows)
    half = HEAD_DIM // 2
    inv = ROPE_THETA ** (-np.arange(0, half, 2, dtype=np.float64) / half)
    ang = np.concatenate([row[:, None] * inv, col[:, None] * inv], axis=-1)
    cos = np.repeat(np.cos(ang), 2, axis=-1).astype(np.float32)
    sign = np.tile(np.asarray([-1.0, 1.0]), half)
    sin = (np.repeat(np.sin(ang), 2, axis=-1) * sign).astype(np.float32)
    return cos, sin


def _gqa_prep_kernel(p_ref, cos_ref, sin_ref, qg_ref, kg_ref, qt_ref, k_ref, vt_ref):
    cos = cos_ref[...]
    sin = sin_ref[...]
    even = (lax.broadcasted_iota(jnp.int32, cos.shape, 1) % 2) == 0

    def norm_rope(xh, g):
        y = _rms(xh, g)
        partner = jnp.where(even, pltpu.roll(y, LANES - 1, 1), pltpu.roll(y, 1, 1))
        return y * cos + partner * sin

    scale = HEAD_DIM ** -0.5 * LOG2_E
    for hq in range(B_Q_HEADS):
        xh = p_ref[:, hq * HEAD_DIM:(hq + 1) * HEAD_DIM]
        y = norm_rope(xh, qg_ref[...]) * scale
        qt_ref[hq * HEAD_DIM:(hq + 1) * HEAD_DIM, :] = y.T.astype(BF16)
    k0 = B_Q_HEADS * HEAD_DIM
    v0 = k0 + B_KV_HEADS * HEAD_DIM
    for hk in range(B_KV_HEADS):
        xh = p_ref[:, k0 + hk * HEAD_DIM:k0 + (hk + 1) * HEAD_DIM]
        k_ref[:, hk * HEAD_DIM:(hk + 1) * HEAD_DIM] = norm_rope(xh, kg_ref[...]).astype(BF16)
        vh = p_ref[:, v0 + hk * HEAD_DIM:v0 + (hk + 1) * HEAD_DIM]
        vt_ref[hk * HEAD_DIM:(hk + 1) * HEAD_DIM, :] = vh.T.astype(BF16)


def gqa_prep(proj, q_g, k_g, seq, col_block, ts=512):
    wb = (B_Q_HEADS + 2 * B_KV_HEADS) * HEAD_DIM
    cos, sin = _rope_tables(seq)
    return pl.pallas_call(
        _gqa_prep_kernel,
        grid=(seq // ts,),
        in_specs=[pl.BlockSpec((ts, wb), lambda i: (i, col_block)),
                  pl.BlockSpec((ts, HEAD_DIM), lambda i: (i, 0)),
                  pl.BlockSpec((ts, HEAD_DIM), lambda i: (i, 0)),
                  pl.BlockSpec((1, HEAD_DIM), lambda i: (0, 0)),
                  pl.BlockSpec((1, HEAD_DIM), lambda i: (0, 0))],
        out_specs=[pl.BlockSpec((B_Q_HEADS * HEAD_DIM, ts), lambda i: (0, i)),
                   pl.BlockSpec((ts, B_KV_HEADS * HEAD_DIM), lambda i: (i, 0)),
                   pl.BlockSpec((B_KV_HEADS * HEAD_DIM, ts), lambda i: (0, i))],
        out_shape=[jax.ShapeDtypeStruct((B_Q_HEADS * HEAD_DIM, seq), BF16),
                   jax.ShapeDtypeStruct((seq, B_KV_HEADS * HEAD_DIM), BF16),
                   jax.ShapeDtypeStruct((B_KV_HEADS * HEAD_DIM, seq), BF16)],
        compiler_params=_params(("parallel",)),
        name="gqa_prep",
    )(proj, cos, sin, q_g.reshape(1, HEAD_DIM), k_g.reshape(1, HEAD_DIM))


STAGE_ROWS = 32
GQA_UNROLL = 1


def _data_dependent_zero(tiles):
    bits = None
    for t in tiles:
        b = pltpu.bitcast(t, jnp.uint32)
        bits = b if bits is None else bits | b
    zero_words = (bits >> 16) >> 16
    return zero_words.astype(jnp.int32).astype(F32).astype(BF16)


def _gqa_kernel(qt_ref, k_ref, vt_ref, o_ref, m_ref, acc_ref, st_ref, ks_ref, *, seq, tk, nh):
    n_chunks = seq // tk
    tq = qt_ref.shape[1]
    m_ref[...] = jnp.full(m_ref.shape, NEG_BIG, F32)
    acc_ref[...] = jnp.zeros(acc_ref.shape, F32)

    def q_head(hh):
        return qt_ref[hh * HEAD_DIM:(hh + 1) * HEAD_DIM, :]

    def chunk_start(c):
        return pl.multiple_of(jnp.minimum(c, n_chunks - 1) * tk, tk)

    st_ref[...] = jnp.dot(k_ref[pl.ds(0, tk), :], q_head(0), preferred_element_type=F32)

    def chunk_items(c, st):
        here = chunk_start(c)
        vt = jnp.concatenate([vt_ref[:, pl.ds(here, tk)], jnp.ones((BF16_ROWS, tk), BF16)], axis=0)
        for hh in range(nh):
            m_old = m_ref[hh]
            m_new = jnp.maximum(m_old, jnp.max(st, axis=0, keepdims=True))
            alpha = jnp.exp2(m_old - m_new)
            pb = jnp.exp2(st - m_new).astype(BF16)

            nxt = here if hh + 1 < nh else chunk_start(c + 1)
            slot = ks_ref.at[hh % 2]
            for r in range(0, tk, STAGE_ROWS):
                strip = [pb[r:r + STAGE_ROWS, j:j + LANES] for j in range(0, tq, LANES)]
                zero = _data_dependent_zero(strip)
                piece = k_ref[pl.ds(pl.multiple_of(nxt + r, STAGE_ROWS), STAGE_ROWS), :]
                slot[r:r + STAGE_ROWS, :] = piece + pltpu.repeat(zero, STAGE_ROWS // zero.shape[0], axis=0)
            st = jnp.dot(slot[...], q_head((hh + 1) % nh), preferred_element_type=F32)

            acc_ref[hh] = alpha * acc_ref[hh] + jnp.dot(vt, pb, preferred_element_type=F32)
            m_ref[hh] = m_new
        return st

    def body(i, carry):
        st = st_ref[...]
        for u in range(GQA_UNROLL):
            st = chunk_items(i * GQA_UNROLL + u, st)
        st_ref[...] = st
        return carry

    lax.fori_loop(0, n_chunks // GQA_UNROLL, body, 0)
    for hh in range(nh):
        acc = acc_ref[hh]
        out = acc[:HEAD_DIM] / acc[HEAD_DIM:HEAD_DIM + 1]
        o_ref[:, hh * HEAD_DIM:(hh + 1) * HEAD_DIM] = out.T.astype(o_ref.dtype)


def gqa_attention(qt, k, vt, seq, tq=512, tk=512):
    nh = B_Q_HEADS // B_KV_HEADS
    return pl.pallas_call(
        functools.partial(_gqa_kernel, seq=seq, tk=tk, nh=nh),
        grid=(B_KV_HEADS, seq // tq),
        in_specs=[pl.BlockSpec((nh * HEAD_DIM, tq), lambda g, i: (g, i)),
                  pl.BlockSpec((seq, HEAD_DIM), lambda g, i: (0, g)),
                  pl.BlockSpec((HEAD_DIM, seq), lambda g, i: (g, 0))],
        out_specs=pl.BlockSpec((tq, nh * HEAD_DIM), lambda g, i: (i, g)),
        out_shape=jax.ShapeDtypeStruct((seq, B_Q_HEADS * HEAD_DIM), BF16),
        scratch_shapes=[pltpu.VMEM((nh, 1, tq), F32),
                        pltpu.VMEM((nh, HEAD_DIM + BF16_ROWS, tq), F32),
                        pltpu.VMEM((tk, tq), F32),
                        pltpu.VMEM((2, tk, HEAD_DIM), BF16)],
        compiler_params=_params(("parallel", "arbitrary")),
        name="gqa_attention",
    )(qt, k, vt)


def _outproj_kernel(ma_ref, mb_ref, wa_ref, wb_ref, x_ref, g_ref, x1_ref, ht_ref):
    y = x_ref[...]
    y = y + jnp.dot(ma_ref[...], wa_ref[...], preferred_element_type=F32)
    y = y + jnp.dot(mb_ref[...], wb_ref[...], preferred_element_type=F32)
    x1_ref[...] = y
    ht_ref[...] = _rms(y, g_ref[...]).T.astype(BF16)


def out_projection(mixed_a, mixed_b, w, x, g, tm=512):
    s, d = x.shape
    wa, wb = mixed_a.shape[1], mixed_b.shape[1]
    assert wa == wb and w.shape[0] == wa + wb
    return pl.pallas_call(
        _outproj_kernel,
        grid=(s // tm,),
        in_specs=[pl.BlockSpec((tm, wa), lambda i: (i, 0)),
                  pl.BlockSpec((tm, wb), lambda i: (i, 0)),
                  pl.BlockSpec((wa, d), lambda i: (0, 0)),
                  pl.BlockSpec((wb, d), lambda i: (1, 0)),
                  pl.BlockSpec((tm, d), lambda i: (i, 0)),
                  pl.BlockSpec((1, d), lambda i: (0, 0))],
        out_specs=[pl.BlockSpec((tm, d), lambda i: (i, 0)),
                   pl.BlockSpec((d, tm), lambda i: (0, i))],
        out_shape=[jax.ShapeDtypeStruct((s, d), F32),
                   jax.ShapeDtypeStruct((d, s), BF16)],
        compiler_params=_params(("parallel",)),
        name="out_projection",
    )(mixed_a, mixed_b, w, w, x, g.reshape(1, d))


N_RANK = PEER_TOPK + 1
RANK_ROWS = 24
CAND_COUNTS = tuple(N_RANK // (a + 1) for a in range(N_RANK))
CAND_ROWS = 8 * (1 << (-(-sum(CAND_COUNTS) // 8) - 1).bit_length())
ROUTER_UNROLL = 8


def _odd_even_merge_sort_pairs(n):
    pairs = []

    def merge(lo, hi, r):
        step = r * 2
        if step < hi - lo:
            merge(lo, hi, step)
            merge(lo + r, hi, step)
            pairs.extend((i, i + r) for i in range(lo + r, hi - r, step))
        else:
            pairs.append((lo, lo + r))

    def sort(lo, hi):
        if hi - lo >= 1:
            mid = lo + (hi - lo) // 2
            sort(lo, mid)
            sort(mid + 1, hi)
            merge(lo, hi, 1)

    sort(0, n - 1)
    return pairs


def _extract_sorted_tile(tile, n, emit):
    depth = tile.shape[0] // SUBLANES
    lists = [tile[SUBLANES * j:SUBLANES * (j + 1)] for j in range(depth)]
    for i, j in _odd_even_merge_sort_pairs(depth):
        lists[i], lists[j] = jnp.maximum(lists[i], lists[j]), jnp.minimum(lists[i], lists[j])
    sub_idx = lax.broadcasted_iota(jnp.int32, lists[0].shape, 0).astype(F32)
    for r in range(n):
        mj = jnp.max(lists[0], axis=0, keepdims=True)
        emit(r, mj)
        tied = lists[0] == mj
        first = jnp.min(jnp.where(tied, sub_idx, float(SUBLANES)), axis=0, keepdims=True)
        took = sub_idx == first
        for j in range(min(depth, n - (r + 1))):
            below = lists[j + 1] if j + 1 < depth else -jnp.inf
            lists[j] = jnp.where(took, below, lists[j])


def _router_kernel(ht_ref, wq_ref, keys_ref, thr_ref, w0_ref, s1_ref, e1_ref,
                   q_scr, s_scr, top_scr, cand_scr, best_scr, *, tb):
    q_scr[...] = lax.dot_general(wq_ref[...], ht_ref[...], (((0,), (0,)), ((), ())),
                                 preferred_element_type=F32).astype(BF16)
    lane_groups = [slice(lg * LANES, (lg + 1) * LANES) for lg in range(tb // LANES)]

    def half_body(hc):
        qhc = q_scr[pl.ds(pl.multiple_of(hc * PEER_N_KEYS, PEER_N_KEYS), PEER_N_KEYS), :]
        s = jnp.dot(keys_ref[hc], qhc, preferred_element_type=F32)
        s_scr[hc] = s
        for ls in lane_groups:
            def emit(j, mj, ls=ls):
                top_scr[hc, pl.ds(j, 1), ls] = mj

            _extract_sorted_tile(s[:, ls], N_RANK, emit)

    def half_group(i, carry):
        for u in range(ROUTER_UNROLL):
            half_body(i * ROUTER_UNROLL + u)
        return carry

    lax.fori_loop(0, 2 * PEER_HEADS // ROUTER_UNROLL, half_group, 0)

    def head_body(h, carry):
        t0 = top_scr[2 * h]
        t1 = top_scr[2 * h + 1]
        cand_scr[...] = jnp.full(cand_scr.shape, -jnp.inf, F32)
        off = 0
        for a, nb in enumerate(CAND_COUNTS):
            cand_scr[off:off + nb, :] = t0[a:a + 1, :] + t1[0:nb, :]
            off += nb

        for ls in lane_groups:
            def emit(j, mj, ls=ls):
                best_scr[pl.ds(j, 1), ls] = mj

            _extract_sorted_tile(cand_scr[:, ls], N_RANK, emit)
        best = [best_scr[j:j + 1, :] for j in range(N_RANK)]
        theta = 0.5 * (best[PEER_TOPK - 1] + best[PEER_TOPK])
        z = jnp.zeros_like(theta)
        for j in range(PEER_TOPK):
            z = z + jnp.exp(best[j] - best[0])

        s0 = s_scr[2 * h]
        s1 = s_scr[2 * h + 1]
        thr_ref[h] = theta - s0
        w0_ref[h] = jnp.exp(s0 - t0[0:1, :]) * (0.5 / z)
        e1 = jnp.exp(s1 - t1[0:1, :])
        for lg, ls in enumerate(lane_groups):
            s1_ref[h, lg] = s1[:, ls]
            e1_ref[h, lg] = e1[:, ls]
        return carry

    lax.fori_loop(0, PEER_HEADS, head_body, 0)


def peer_router(ht, wq, keys, tb=512):
    d, s = ht.shape
    nq = wq.shape[1]
    tab = jax.ShapeDtypeStruct((PEER_HEADS, PEER_N_KEYS, s), F32)
    tab_spec = pl.BlockSpec((PEER_HEADS, PEER_N_KEYS, tb), lambda i: (0, 0, i))
    tile = jax.ShapeDtypeStruct((PEER_HEADS, s // LANES, PEER_N_KEYS, LANES), F32)
    tile_spec = pl.BlockSpec((PEER_HEADS, tb // LANES, PEER_N_KEYS, LANES), lambda i: (0, i, 0, 0))
    return pl.pallas_call(
        functools.partial(_router_kernel, tb=tb),
        grid=(s // tb,),
        in_specs=[pl.BlockSpec((d, tb), lambda i: (0, i)),
                  pl.BlockSpec((d, nq), lambda i: (0, 0)),
                  pl.BlockSpec(keys.shape, lambda i: (0, 0, 0))],
        out_specs=[tab_spec, tab_spec, tile_spec, tile_spec],
        out_shape=[tab, tab, tile, tile],
        scratch_shapes=[pltpu.VMEM((nq, tb), BF16),
                        pltpu.VMEM((2 * PEER_HEADS, PEER_N_KEYS, tb), F32),
                        pltpu.VMEM((2 * PEER_HEADS, RANK_ROWS, tb), F32),
                        pltpu.VMEM((CAND_ROWS, tb), F32),
                        pltpu.VMEM((RANK_ROWS, tb), F32)],
        compiler_params=_params(("parallel",)),
        name="peer_router",
    )(ht, wq, keys)


GELU_C = 0.7978845608028654


def _gelu_tanh_x2(x):
    u = x * (GELU_C + (GELU_C * 0.044715) * (x * x))
    return x + x * jnp.tanh(u)


PEER_SUB = 256
GATE_ROWS = 32


def _peer_stage(ht_ref, u_ref, v_ref, thr_ref, w0_ref, s1_ref, e1_ref, o_ref,
                at_w, at_r, pt_w, pt_r, *, te, tb):
    d = o_ref.shape[1]
    n_sub = te // PEER_SUB
    col_w = d // n_sub

    def hidden_piece(j):
        rows = slice(j * PEER_SUB, (j + 1) * PEER_SUB)
        at = jnp.dot(u_ref[rows, :], ht_ref[...], preferred_element_type=F32)
        for lg in range(tb // LANES):
            at_w[lg, rows, :] = at[:, lg * LANES:(lg + 1) * LANES]

    def value_piece(c):
        cols = slice(c * col_w, (c + 1) * col_w)
        o_ref[:, cols] += lax.dot_general(pt_r[...], v_ref[:, cols], (((0,), (0,)), ((), ())),
                                          preferred_element_type=F32)

    n_i0 = te // PEER_N_KEYS
    n_q = PEER_N_KEYS // GATE_ROWS

    def gate_piece(lg, q):
        ls = slice(lg * LANES, (lg + 1) * LANES)
        qs = slice(q * GATE_ROWS, (q + 1) * GATE_ROWS)
        gates = [None] * n_i0
        for h in range(PEER_HEADS):
            s1 = s1_ref[h, lg, qs, :]
            e1 = e1_ref[h, lg, qs, :]
            for i0 in range(n_i0):
                thr = thr_ref[h, i0:i0 + 1, ls]
                w0 = w0_ref[h, i0:i0 + 1, ls]
                term = jnp.where(s1 >= thr, e1 * w0, 0.0)
                gates[i0] = term if gates[i0] is None else gates[i0] + term
        for i0 in range(n_i0):
            er = slice(i0 * PEER_N_KEYS + q * GATE_ROWS, i0 * PEER_N_KEYS + (q + 1) * GATE_ROWS)
            pt_w[er, ls] = (_gelu_tanh_x2(at_r[lg, er, :]) * gates[i0]).astype(BF16)

    mxu_pieces = [functools.partial(hidden_piece, j) for j in range(n_sub)]
    mxu_pieces += [functools.partial(value_piece, c) for c in range(n_sub)]
    vec_pieces = [(lg, q) for lg in range(tb // LANES) for q in range(n_q)]
    assert len(vec_pieces) % len(mxu_pieces) == 0
    per_mxu = len(vec_pieces) // len(mxu_pieces)
    for n, piece in enumerate(mxu_pieces):
        for lg, q in vec_pieces[n * per_mxu:(n + 1) * per_mxu]:
            gate_piece(lg, q)
        piece()


def _peer_kernel(ht_ref, u_ref, v_ref, thr_ref, w0_ref, s1_ref, e1_ref, x1_ref, g_ref,
                 o_ref, at0_ref, at1_ref, pt0_ref, pt1_ref, *, te, tb, n_e):
    g = pl.program_id(0)
    e_out = (g - 2) % n_e
    out_live = g >= 2
    at_refs = (at0_ref, at1_ref)
    pt_refs = (pt0_ref, pt1_ref)

    @pl.when(g == 0)
    def _():
        for r in at_refs + pt_refs:
            r[...] = jnp.zeros(r.shape, r.dtype)

    @pl.when((g == 0) | (out_live & (e_out == 0)))
    def _():
        o_ref[...] = x1_ref[...]

    for parity in range(2):
        @pl.when(g % 2 == parity)
        def _(parity=parity):
            _peer_stage(ht_ref, u_ref, v_ref, thr_ref, w0_ref, s1_ref, e1_ref, o_ref,
                        at_refs[parity], at_refs[1 - parity],
                        pt_refs[1 - parity], pt_refs[parity], te=te, tb=tb)

    @pl.when(out_live & (e_out == n_e - 1))
    def _():
        o_ref[...] = _rms(o_ref[...], g_ref[...])


def peer_experts(ht, u, v, tables, x1, g, tb=512, te=1024):
    d, s = ht.shape
    n_exp = u.shape[0]
    n_i0 = te // PEER_N_KEYS
    n_e = n_exp // te
    n_items = (s // tb) * n_e
    thr, w0, s1, e1 = tables

    def item(gi, lag):
        n = jnp.clip(gi - lag, 0, n_items - 1)
        return n // n_e, n % n_e

    row_spec = pl.BlockSpec((PEER_HEADS, n_i0, tb), lambda gi: (0, item(gi, 1)[1], item(gi, 1)[0]))
    tab_spec = pl.BlockSpec((PEER_HEADS, tb // LANES, PEER_N_KEYS, LANES),
                            lambda gi: (0, item(gi, 1)[0], 0, 0))
    return pl.pallas_call(
        functools.partial(_peer_kernel, te=te, tb=tb, n_e=n_e),
        grid=(n_items + 2,),
        in_specs=[pl.BlockSpec((d, tb), lambda gi: (0, item(gi, 0)[0])),
                  pl.BlockSpec((te, d), lambda gi: (item(gi, 0)[1], 0)),
                  pl.BlockSpec((te, d), lambda gi: (item(gi, 2)[1], 0)),
                  row_spec, row_spec, tab_spec, tab_spec,
                  pl.BlockSpec((tb, d), lambda gi: (item(gi, 2)[0], 0),
                               pipeline_mode=pl.Buffered(1)),
                  pl.BlockSpec((1, d), lambda gi: (0, 0))],
        out_specs=pl.BlockSpec((tb, d), lambda gi: (item(gi, 2)[0], 0)),
        out_shape=jax.ShapeDtypeStruct((s, d), F32),
        scratch_shapes=[pltpu.VMEM((tb // LANES, te, LANES), F32),
                        pltpu.VMEM((tb // LANES, te, LANES), F32),
                        pltpu.VMEM((te, tb), BF16),
                        pltpu.VMEM((te, tb), BF16)],
        compiler_params=_params(("arbitrary",)),
        name="peer_experts",
    )(ht, u, v, thr, w0, s1, e1, x1, g.reshape(1, d))


def kernel(x, norm1_g, w_in, q_norm_g, k_norm_g, w_out, norm2_g, peer_w_query,
           peer_sub_keys, peer_u, peer_v, final_norm_g):
    b, s, d = x.shape
    assert b == 1
    depth = w_in.shape[0]
    xs = x.reshape(s, d)
    for l in range(depth):
        proj = in_projection(xs, norm1_g[l], w_in[l])
        mixed_a, (u_bf16, v_bf16, w_o, w_q) = dilated_attention(
            proj, s, to_bf16=(peer_u[l], peer_v[l], w_out[l], peer_w_query[l]))
        qt, kb, vt = gqa_prep(proj, q_norm_g[l], k_norm_g[l], s, col_block=2)
        mixed_b = gqa_attention(qt, kb, vt, s)
        x1, ht = out_projection(mixed_a, mixed_b, w_o, xs, norm2_g[l])
        keys = peer_sub_keys[l].reshape(2 * PEER_HEADS, PEER_N_KEYS, -1).astype(BF16)
        tables = peer_router(ht, w_q, keys)
        assert l == depth - 1
        xs = peer_experts(ht, u_bf16, v_bf16, tables, x1, final_norm_g)
    return xs.reshape(b, s, d)
```

```python
import functools

import numpy as np
import jax
import jax.numpy as jnp
from jax import lax
from jax.experimental import pallas as pl
from jax.experimental.pallas import tpu as pltpu

F32 = jnp.float32
BF16 = jnp.bfloat16

HEAD_DIM = 128
A_HEADS = 8
DIL_PATTERNS = ((128, 1), (512, 4), (2048, 16))
B_Q_HEADS = 8
B_KV_HEADS = 2
GRID_W = 64
ROPE_THETA = 10000.0
NORM_EPS = 1e-6
PEER_HEADS = 8
PEER_N_KEYS = 128
PEER_TOPK = 16

LANES = 128
SUBLANES = 8
BF16_ROWS = 16
NEG_BIG = -1e30
LOG2_E = 1.4426950408889634
VMEM_LIMIT = 56 * 1024 * 1024


def _params(sem, vmem=VMEM_LIMIT):
    return pltpu.CompilerParams(dimension_semantics=sem, vmem_limit_bytes=vmem)


def _rms(x, g):
    ms = jnp.mean(x * x, axis=-1, keepdims=True)
    return x * lax.rsqrt(ms + NORM_EPS) * g


def _inproj_kernel(x_ref, g_ref, w_ref, o_ref, h_ref):
    @pl.when(pl.program_id(1) == 0)
    def _():
        h_ref[...] = _rms(x_ref[...], g_ref[...]).astype(BF16)

    o_ref[...] = jnp.dot(h_ref[...], w_ref[...].astype(BF16), preferred_element_type=F32)


def in_projection(x, g, w, tm=1024, tn=512):
    s, d = x.shape
    n = w.shape[1]
    return pl.pallas_call(
        _inproj_kernel,
        grid=(s // tm, n // tn),
        in_specs=[pl.BlockSpec((tm, d), lambda i, j: (i, 0)),
                  pl.BlockSpec((1, d), lambda i, j: (0, 0)),
                  pl.BlockSpec((d, tn), lambda i, j: (0, j))],
        out_specs=pl.BlockSpec((tm, tn), lambda i, j: (i, j)),
        out_shape=jax.ShapeDtypeStruct((s, n), F32),
        scratch_shapes=[pltpu.VMEM((tm, d), BF16)],
        compiler_params=_params(("parallel", "arbitrary")),
        name="in_projection",
    )(x, g.reshape(1, d), w)


DIL_QBLK = 2048
DIL_SUB = 128
DIL_KWIN = 256
DIL_UNROLL = 16
DIL_EDGE_CASES = 3


def _dilated_kernel(q_ref, k_ref, v_ref, *rest, seq, n_cast):
    cast_src, rest = rest[:n_cast], rest[n_cast:]
    o_ref, cast_dst, rest = rest[0], rest[1:1 + n_cast], rest[1 + n_cast:]
    m_ref, l_ref, acc_ref, bias_ref = rest
    for src, dst in zip(cast_src, cast_dst):
        dst[...] = src[...].astype(dst.dtype)

    h = pl.program_id(0)
    base = pl.program_id(1) * DIL_QBLK
    slope = jnp.exp2(-(h + 1).astype(F32)) * LOG2_E
    scale = HEAD_DIM ** -0.5 * LOG2_E

    m_ref[...] = jnp.full(m_ref.shape, NEG_BIG, F32)
    l_ref[...] = jnp.zeros(l_ref.shape, F32)
    acc_ref[...] = jnp.zeros(acc_ref.shape, F32)

    row = lax.broadcasted_iota(jnp.int32, (DIL_SUB, DIL_KWIN), 0)
    col = lax.broadcasted_iota(jnp.int32, (DIL_SUB, DIL_KWIN), 1)
    col_minus_row = col - row

    for b_idx, (w, d) in enumerate(DIL_PATTERNS):
        half = (w // 2) // d
        assert half * 2 + DIL_SUB == DIL_KWIN
        seg_len = seq // d

        for v_idx in range(DIL_EDGE_CASES):
            rel = jnp.abs(col_minus_row - v_idx * half)
            bias_ref[b_idx * DIL_EDGE_CASES + v_idx] = jnp.where(
                rel <= half, -(slope * d) * rel.astype(F32), NEG_BIG)

        def rows(start, n, d=d):
            return pl.ds(start, n, stride=d) if d > 1 else pl.ds(start, n)

        def body(c, carry, d=d, half=half, seg_len=seg_len, rows=rows, b_idx=b_idx):
            r = c % d
            cc = c // d
            p0 = base // d + cc * DIL_SUB
            kp = jnp.clip(p0 - half, 0, seg_len - DIL_KWIN)
            lq = r + d * (cc * DIL_SUB)
            ks = r + d * kp

            q = (q_ref[rows(lq, DIL_SUB), :] * scale).astype(BF16)
            k = k_ref[rows(ks, DIL_KWIN), :].astype(BF16)
            v = v_ref[rows(ks, DIL_KWIN), :].astype(BF16)
            s = lax.dot_general(q, k, (((1,), (1,)), ((), ())), preferred_element_type=F32)
            s = s + bias_ref[b_idx * DIL_EDGE_CASES + (p0 - kp) // half]
            mb = jnp.max(s, axis=1, keepdims=True)
            p = jnp.exp2(s - mb)
            lb = jnp.sum(p, axis=1, keepdims=True)
            ob = jnp.dot(p.astype(BF16), v, preferred_element_type=F32)

            idx = rows(lq, DIL_SUB)
            m_old = m_ref[idx, :]
            m_new = jnp.maximum(m_old, mb)
            a_old = jnp.exp2(m_old - m_new)
            a_blk = jnp.exp2(mb - m_new)
            m_ref[idx, :] = m_new
            l_ref[idx, :] = a_old * l_ref[idx, :] + a_blk * lb
            acc_ref[idx, :] = a_old * acc_ref[idx, :] + a_blk * ob
            return carry

        def group(cg, carry, body=body):
            for u in range(DIL_UNROLL):
                body(cg * DIL_UNROLL + u, carry)
            return carry

        lax.fori_loop(0, DIL_QBLK // DIL_SUB // DIL_UNROLL, group, 0)

    o_ref[...] = (acc_ref[...] / l_ref[...]).astype(o_ref.dtype)


def dilated_attention(proj, seq, to_bf16=()):
    assert seq % DIL_QBLK == 0
    for _, d in DIL_PATTERNS:
        assert DIL_QBLK // d >= DIL_SUB and seq // d >= DIL_KWIN
    nh = A_HEADS
    n_qb = seq // DIL_QBLK
    n_steps = nh * n_qb
    cast_specs = []
    for w in to_bf16:
        assert w.shape[0] % n_steps == 0
        cast_specs.append(pl.BlockSpec((w.shape[0] // n_steps, w.shape[1]),
                                       lambda h, i: (h * n_qb + i, 0)))
    outs = pl.pallas_call(
        functools.partial(_dilated_kernel, seq=seq, n_cast=len(to_bf16)),
        grid=(nh, n_qb),
        in_specs=[pl.BlockSpec((DIL_QBLK, HEAD_DIM), lambda h, i: (i, h)),
                  pl.BlockSpec((seq, HEAD_DIM), lambda h, i: (0, nh + h)),
                  pl.BlockSpec((seq, HEAD_DIM), lambda h, i: (0, 2 * nh + h))] + cast_specs,
        out_specs=[pl.BlockSpec((DIL_QBLK, HEAD_DIM), lambda h, i: (i, h))] + cast_specs,
        out_shape=[jax.ShapeDtypeStruct((seq, nh * HEAD_DIM), BF16)]
        + [jax.ShapeDtypeStruct(w.shape, BF16) for w in to_bf16],
        scratch_shapes=[pltpu.VMEM((DIL_QBLK, LANES), F32),
                        pltpu.VMEM((DIL_QBLK, LANES), F32),
                        pltpu.VMEM((DIL_QBLK, HEAD_DIM), F32),
                        pltpu.VMEM((len(DIL_PATTERNS) * DIL_EDGE_CASES, DIL_SUB, DIL_KWIN), F32)],
        compiler_params=_params(("parallel", "arbitrary")),
        name="dilated_attention",
    )(proj, proj, proj, *to_bf16)
    return outs[0], outs[1:]


def _rope_tables(seq):
    rows = seq // GRID_W
    row = np.repeat(np.arange(rows, dtype=np.float64), GRID_W)
    col = np.tile(np.arange(GRID_W, dtype=np.float64), rows)
    half = HEAD_DIM // 2
    inv = ROPE_THETA ** (-np.arange(0, half, 2, dtype=np.float64) / half)
    ang = np.concatenate([row[:, None] * inv, col[:, None] * inv], axis=-1)
    cos = np.repeat(np.cos(ang), 2, axis=-1).astype(np.float32)
    sign = np.tile(np.asarray([-1.0, 1.0]), half)
    sin = (np.repeat(np.sin(ang), 2, axis=-1) * sign).astype(np.float32)
    return cos, sin


def _gqa_prep_kernel(p_ref, cos_ref, sin_ref, qg_ref, kg_ref, qt_ref, k_ref, vt_ref):
    cos = cos_ref[...]
    sin = sin_ref[...]
    even = (lax.broadcasted_iota(jnp.int32, cos.shape, 1) % 2) == 0

    def norm_rope(xh, g):
        y = _rms(xh, g)
        partner = jnp.where(even, pltpu.roll(y, LANES - 1, 1), pltpu.roll(y, 1, 1))
        return y * cos + partner * sin

    scale = HEAD_DIM ** -0.5 * LOG2_E
    for hq in range(B_Q_HEADS):
        xh = p_ref[:, hq * HEAD_DIM:(hq + 1) * HEAD_DIM]
        y = norm_rope(xh, qg_ref[...]) * scale
        qt_ref[hq * HEAD_DIM:(hq + 1) * HEAD_DIM, :] = y.T.astype(BF16)
    k0 = B_Q_HEADS * HEAD_DIM
    v0 = k0 + B_KV_HEADS * HEAD_DIM
    for hk in range(B_KV_HEADS):
        xh = p_ref[:, k0 + hk * HEAD_DIM:k0 + (hk + 1) * HEAD_DIM]
        k_ref[:, hk * HEAD_DIM:(hk + 1) * HEAD_DIM] = norm_rope(xh, kg_ref[...]).astype(BF16)
        vh = p_ref[:, v0 + hk * HEAD_DIM:v0 + (hk + 1) * HEAD_DIM]
        vt_ref[hk * HEAD_DIM:(hk + 1) * HEAD_DIM, :] = vh.T.astype(BF16)


def gqa_prep(proj, q_g, k_g, seq, col_block, ts=512):
    wb = (B_Q_HEADS + 2 * B_KV_HEADS) * HEAD_DIM
    cos, sin = _rope_tables(seq)
    return pl.pallas_call(
        _gqa_prep_kernel,
        grid=(seq // ts,),
        in_specs=[pl.BlockSpec((ts, wb), lambda i: (i, col_block)),
                  pl.BlockSpec((ts, HEAD_DIM), lambda i: (i, 0)),
                  pl.BlockSpec((ts, HEAD_DIM), lambda i: (i, 0)),
                  pl.BlockSpec((1, HEAD_DIM), lambda i: (0, 0)),
                  pl.BlockSpec((1, HEAD_DIM), lambda i: (0, 0))],
        out_specs=[pl.BlockSpec((B_Q_HEADS * HEAD_DIM, ts), lambda i: (0, i)),
                   pl.BlockSpec((ts, B_KV_HEADS * HEAD_DIM), lambda i: (i, 0)),
                   pl.BlockSpec((B_KV_HEADS * HEAD_DIM, ts), lambda i: (0, i))],
        out_shape=[jax.ShapeDtypeStruct((B_Q_HEADS * HEAD_DIM, seq), BF16),
                   jax.ShapeDtypeStruct((seq, B_KV_HEADS * HEAD_DIM), BF16),
                   jax.ShapeDtypeStruct((B_KV_HEADS * HEAD_DIM, seq), BF16)],
        compiler_params=_params(("parallel",)),
        name="gqa_prep",
    )(proj, cos, sin, q_g.reshape(1, HEAD_DIM), k_g.reshape(1, HEAD_DIM))


STAGE_ROWS = 32


def _data_dependent_zero(tiles):
    bits = None
    for t in tiles:
        b = pltpu.bitcast(t, jnp.uint32)
        bits = b if bits is None else bits | b
    zero_words = (bits >> 16) >> 16
    return zero_words.astype(jnp.int32).astype(F32).astype(BF16)


def _gqa_kernel(qt_ref, k_ref, vt_ref, o_ref, m_ref, acc_ref, st_ref, ks_ref, *, seq, tk, nh, ng):
    n_chunks = seq // tk
    tq = qt_ref.shape[1]
    m_ref[...] = jnp.full(m_ref.shape, NEG_BIG, F32)
    acc_ref[...] = jnp.zeros(acc_ref.shape, F32)

    def q_head(g, hh):
        row = (g * nh + hh) * HEAD_DIM
        return qt_ref[row:row + HEAD_DIM, :]

    def keys(g, start, n):
        return k_ref[pl.ds(start, n), g * HEAD_DIM:(g + 1) * HEAD_DIM]

    def chunk_start(c):
        return pl.multiple_of(jnp.minimum(c, n_chunks - 1) * tk, tk)

    for g in range(ng):
        st_ref[g] = jnp.dot(keys(g, 0, tk), q_head(g, 0), preferred_element_type=F32)

    def item(g, hh, st, here, nxt, vt):
        idx = g * nh + hh
        m_old = m_ref[idx]
        m_new = jnp.maximum(m_old, jnp.max(st, axis=0, keepdims=True))
        alpha = jnp.exp2(m_old - m_new)
        pb = jnp.exp2(st - m_new).astype(BF16)

        src = here if hh + 1 < nh else nxt
        slot = ks_ref.at[g, hh % 2]
        for r in range(0, tk, STAGE_ROWS):
            strip = [pb[r:r + STAGE_ROWS, j:j + LANES] for j in range(0, tq, LANES)]
            zero = _data_dependent_zero(strip)
            piece = keys(g, pl.multiple_of(src + r, STAGE_ROWS), STAGE_ROWS)
            slot[r:r + STAGE_ROWS, :] = piece + pltpu.repeat(zero, STAGE_ROWS // zero.shape[0], axis=0)
        st_next = jnp.dot(slot[...], q_head(g, (hh + 1) % nh), preferred_element_type=F32)

        acc_ref[idx] = alpha * acc_ref[idx] + jnp.dot(vt, pb, preferred_element_type=F32)
        m_ref[idx] = m_new
        return st_next

    def body(c, carry):
        here = chunk_start(c)
        nxt = chunk_start(c + 1)
        ones = jnp.ones((BF16_ROWS, tk), BF16)
        vts = [jnp.concatenate([vt_ref[g * HEAD_DIM:(g + 1) * HEAD_DIM, pl.ds(here, tk)], ones], axis=0)
               for g in range(ng)]
        sts = [st_ref[g] for g in range(ng)]
        for hh in range(nh):
            for g in range(ng):
                sts[g] = item(g, hh, sts[g], here, nxt, vts[g])
        for g in range(ng):
            st_ref[g] = sts[g]
        return carry

    lax.fori_loop(0, n_chunks, body, 0)
    for idx in range(ng * nh):
        acc = acc_ref[idx]
        out = acc[:HEAD_DIM] / acc[HEAD_DIM:HEAD_DIM + 1]
        o_ref[:, idx * HEAD_DIM:(idx + 1) * HEAD_DIM] = out.T.astype(o_ref.dtype)


def gqa_attention(qt, k, vt, seq, tq=512, tk=512):
    ng = B_KV_HEADS
    nh = B_Q_HEADS // B_KV_HEADS
    return pl.pallas_call(
        functools.partial(_gqa_kernel, seq=seq, tk=tk, nh=nh, ng=ng),
        grid=(seq // tq,),
        in_specs=[pl.BlockSpec((ng * nh * HEAD_DIM, tq), lambda i: (0, i)),
                  pl.BlockSpec((seq, ng * HEAD_DIM), lambda i: (0, 0)),
                  pl.BlockSpec((ng * HEAD_DIM, seq), lambda i: (0, 0))],
        out_specs=pl.BlockSpec((tq, ng * nh * HEAD_DIM), lambda i: (i, 0)),
        out_shape=jax.ShapeDtypeStruct((seq, B_Q_HEADS * HEAD_DIM), BF16),
        scratch_shapes=[pltpu.VMEM((ng * nh, 1, tq), F32),
                        pltpu.VMEM((ng * nh, HEAD_DIM + BF16_ROWS, tq), F32),
                        pltpu.VMEM((ng, tk, tq), F32),
                        pltpu.VMEM((ng, 2, tk, HEAD_DIM), BF16)],
        compiler_params=_params(("parallel",)),
        name="gqa_attention",
    )(qt, k, vt)


def _outproj_kernel(ma_ref, mb_ref, wa_ref, wb_ref, x_ref, g_ref, x1_ref, ht_ref):
    y = x_ref[...]
    y = y + jnp.dot(ma_ref[...], wa_ref[...], preferred_element_type=F32)
    y = y + jnp.dot(mb_ref[...], wb_ref[...], preferred_element_type=F32)
    x1_ref[...] = y
    ht_ref[...] = _rms(y, g_ref[...]).T.astype(BF16)


def out_projection(mixed_a, mixed_b, w, x, g, tm=512):
    s, d = x.shape
    wa, wb = mixed_a.shape[1], mixed_b.shape[1]
    assert wa == wb and w.shape[0] == wa + wb
    return pl.pallas_call(
        _outproj_kernel,
        grid=(s // tm,),
        in_specs=[pl.BlockSpec((tm, wa), lambda i: (i, 0)),
                  pl.BlockSpec((tm, wb), lambda i: (i, 0)),
                  pl.BlockSpec((wa, d), lambda i: (0, 0)),
                  pl.BlockSpec((wb, d), lambda i: (1, 0)),
                  pl.BlockSpec((tm, d), lambda i: (i, 0)),
                  pl.BlockSpec((1, d), lambda i: (0, 0))],
        out_specs=[pl.BlockSpec((tm, d), lambda i: (i, 0)),
                   pl.BlockSpec((d, tm), lambda i: (0, i))],
        out_shape=[jax.ShapeDtypeStruct((s, d), F32),
                   jax.ShapeDtypeStruct((d, s), BF16)],
        compiler_params=_params(("parallel",)),
        name="out_projection",
    )(mixed_a, mixed_b, w, w, x, g.reshape(1, d))


N_RANK = PEER_TOPK + 1
RANK_ROWS = 24
CAND_COUNTS = tuple(N_RANK // (a + 1) for a in range(N_RANK))
CAND_ROWS = 8 * (1 << (-(-sum(CAND_COUNTS) // 8) - 1).bit_length())
ROUTER_UNROLL = 8


def _odd_even_merge_sort_pairs(n):
    pairs = []

    def merge(lo, hi, r):
        step = r * 2
        if step < hi - lo:
            merge(lo, hi, step)
            merge(lo + r, hi, step)
            pairs.extend((i, i + r) for i in range(lo + r, hi - r, step))
        else:
            pairs.append((lo, lo + r))

    def sort(lo, hi):
        if hi - lo >= 1:
            mid = lo + (hi - lo) // 2
            sort(lo, mid)
            sort(mid + 1, hi)
            merge(lo, hi, 1)

    sort(0, n - 1)
    return pairs


def _extract_sorted_tile(tile, n, emit):
    depth = tile.shape[0] // SUBLANES
    lists = [tile[SUBLANES * j:SUBLANES * (j + 1)] for j in range(depth)]
    for i, j in _odd_even_merge_sort_pairs(depth):
        lists[i], lists[j] = jnp.maximum(lists[i], lists[j]), jnp.minimum(lists[i], lists[j])
    sub_idx = lax.broadcasted_iota(jnp.int32, lists[0].shape, 0).astype(F32)
    for r in range(n):
        mj = jnp.max(lists[0], axis=0, keepdims=True)
        emit(r, mj)
        tied = lists[0] == mj
        first = jnp.min(jnp.where(tied, sub_idx, float(SUBLANES)), axis=0, keepdims=True)
        took = sub_idx == first
        for j in range(min(depth, n - (r + 1))):
            below = lists[j + 1] if j + 1 < depth else -jnp.inf
            lists[j] = jnp.where(took, below, lists[j])


def _router_kernel(ht_ref, wq_ref, keys_ref, thr_ref, w0_ref, s1_ref, e1_ref,
                   q_scr, s_scr, top_scr, cand_scr, best_scr, *, tb):
    q_scr[...] = lax.dot_general(wq_ref[...], ht_ref[...], (((0,), (0,)), ((), ())),
                                 preferred_element_type=F32).astype(BF16)
    lane_groups = [slice(lg * LANES, (lg + 1) * LANES) for lg in range(tb // LANES)]

    def half_body(hc):
        qhc = q_scr[pl.ds(pl.multiple_of(hc * PEER_N_KEYS, PEER_N_KEYS), PEER_N_KEYS), :]
        s = jnp.dot(keys_ref[hc], qhc, preferred_element_type=F32)
        s_scr[hc] = s
        for ls in lane_groups:
            def emit(j, mj, ls=ls):
                top_scr[hc, pl.ds(j, 1), ls] = mj

            _extract_sorted_tile(s[:, ls], N_RANK, emit)

    def half_group(i, carry):
        for u in range(ROUTER_UNROLL):
            half_body(i * ROUTER_UNROLL + u)
        return carry

    lax.fori_loop(0, 2 * PEER_HEADS // ROUTER_UNROLL, half_group, 0)

    def head_body(h, carry):
        t0 = top_scr[2 * h]
        t1 = top_scr[2 * h + 1]
        cand_scr[...] = jnp.full(cand_scr.shape, -jnp.inf, F32)
        off = 0
        for a, nb in enumerate(CAND_COUNTS):
            cand_scr[off:off + nb, :] = t0[a:a + 1, :] + t1[0:nb, :]
            off += nb

        for ls in lane_groups:
            def emit(j, mj, ls=ls):
                best_scr[pl.ds(j, 1), ls] = mj

            _extract_sorted_tile(cand_scr[:, ls], N_RANK, emit)
        best = [best_scr[j:j + 1, :] for j in range(N_RANK)]
        theta = 0.5 * (best[PEER_TOPK - 1] + best[PEER_TOPK])
        z = jnp.zeros_like(theta)
        for j in range(PEER_TOPK):
            z = z + jnp.exp(best[j] - best[0])

        s0 = s_scr[2 * h]
        s1 = s_scr[2 * h + 1]
        thr_ref[h] = theta - s0
        w0_ref[h] = jnp.exp(s0 - t0[0:1, :]) * (0.5 / z)
        e1 = jnp.exp(s1 - t1[0:1, :])
        for lg, ls in enumerate(lane_groups):
            s1_ref[h, lg] = s1[:, ls]
            e1_ref[h, lg] = e1[:, ls]
        return carry

    lax.fori_loop(0, PEER_HEADS, head_body, 0)


def peer_router(ht, wq, keys, tb=512):
    d, s = ht.shape
    nq = wq.shape[1]
    tab = jax.ShapeDtypeStruct((PEER_HEADS, PEER_N_KEYS, s), F32)
    tab_spec = pl.BlockSpec((PEER_HEADS, PEER_N_KEYS, tb), lambda i: (0, 0, i))
    tile = jax.ShapeDtypeStruct((PEER_HEADS, s // LANES, PEER_N_KEYS, LANES), F32)
    tile_spec = pl.BlockSpec((PEER_HEADS, tb // LANES, PEER_N_KEYS, LANES), lambda i: (0, i, 0, 0))
    return pl.pallas_call(
        functools.partial(_router_kernel, tb=tb),
        grid=(s // tb,),
        in_specs=[pl.BlockSpec((d, tb), lambda i: (0, i)),
                  pl.BlockSpec((d, nq), lambda i: (0, 0)),
                  pl.BlockSpec(keys.shape, lambda i: (0, 0, 0))],
        out_specs=[tab_spec, tab_spec, tile_spec, tile_spec],
        out_shape=[tab, tab, tile, tile],
        scratch_shapes=[pltpu.VMEM((nq, tb), BF16),
                        pltpu.VMEM((2 * PEER_HEADS, PEER_N_KEYS, tb), F32),
                        pltpu.VMEM((2 * PEER_HEADS, RANK_ROWS, tb), F32),
                        pltpu.VMEM((CAND_ROWS, tb), F32),
                        pltpu.VMEM((RANK_ROWS, tb), F32)],
        compiler_params=_params(("parallel",)),
        name="peer_router",
    )(ht, wq, keys)


GELU_C = 0.7978845608028654


def _gelu_tanh_x2(x):
    u = x * (GELU_C + (GELU_C * 0.044715) * (x * x))
    return x + x * jnp.tanh(u)


PEER_SUB = 256
GATE_ROWS = 32


def _peer_stage(ht_ref, u_ref, v_ref, thr_ref, w0_ref, s1_ref, e1_ref, o_ref,
                at_w, at_r, pt_w, pt_r, *, te, tb):
    d = o_ref.shape[1]
    n_sub = te // PEER_SUB
    col_w = d // n_sub

    def hidden_piece(j):
        rows = slice(j * PEER_SUB, (j + 1) * PEER_SUB)
        at = jnp.dot(u_ref[rows, :], ht_ref[...], preferred_element_type=F32)
        for lg in range(tb // LANES):
            at_w[lg, rows, :] = at[:, lg * LANES:(lg + 1) * LANES]

    def value_piece(c):
        cols = slice(c * col_w, (c + 1) * col_w)
        o_ref[:, cols] += lax.dot_general(pt_r[...], v_ref[:, cols], (((0,), (0,)), ((), ())),
                                          preferred_element_type=F32)

    n_i0 = te // PEER_N_KEYS
    n_q = PEER_N_KEYS // GATE_ROWS

    def gate_piece(lg, q):
        ls = slice(lg * LANES, (lg + 1) * LANES)
        qs = slice(q * GATE_ROWS, (q + 1) * GATE_ROWS)
        gates = [None] * n_i0
        for h in range(PEER_HEADS):
            s1 = s1_ref[h, lg, qs, :]
            e1 = e1_ref[h, lg, qs, :]
            for i0 in range(n_i0):
                thr = thr_ref[h, i0:i0 + 1, ls]
                w0 = w0_ref[h, i0:i0 + 1, ls]
                term = jnp.where(s1 >= thr, e1 * w0, 0.0)
                gates[i0] = term if gates[i0] is None else gates[i0] + term
        for i0 in range(n_i0):
            er = slice(i0 * PEER_N_KEYS + q * GATE_ROWS, i0 * PEER_N_KEYS + (q + 1) * GATE_ROWS)
            pt_w[er, ls] = (_gelu_tanh_x2(at_r[lg, er, :]) * gates[i0]).astype(BF16)

    mxu_pieces = [functools.partial(hidden_piece, j) for j in range(n_sub)]
    mxu_pieces += [functools.partial(value_piece, c) for c in range(n_sub)]
    vec_pieces = [(lg, q) for lg in range(tb // LANES) for q in range(n_q)]
    assert len(vec_pieces) % len(mxu_pieces) == 0
    per_mxu = len(vec_pieces) // len(mxu_pieces)
    for n, piece in enumerate(mxu_pieces):
        for lg, q in vec_pieces[n * per_mxu:(n + 1) * per_mxu]:
            gate_piece(lg, q)
        piece()


def _peer_kernel(ht_ref, u_ref, v_ref, thr_ref, w0_ref, s1_ref, e1_ref, x1_ref, g_ref,
                 o_ref, at0_ref, at1_ref, pt0_ref, pt1_ref, *, te, tb, n_e):
    g = pl.program_id(0)
    e_out = (g - 2) % n_e
    out_live = g >= 2
    at_refs = (at0_ref, at1_ref)
    pt_refs = (pt0_ref, pt1_ref)

    @pl.when(g == 0)
    def _():
        for r in at_refs + pt_refs:
            r[...] = jnp.zeros(r.shape, r.dtype)

    @pl.when((g == 0) | (out_live & (e_out == 0)))
    def _():
        o_ref[...] = x1_ref[...]

    for parity in range(2):
        @pl.when(g % 2 == parity)
        def _(parity=parity):
            _peer_stage(ht_ref, u_ref, v_ref, thr_ref, w0_ref, s1_ref, e1_ref, o_ref,
                        at_refs[parity], at_refs[1 - parity],
                        pt_refs[1 - parity], pt_refs[parity], te=te, tb=tb)

    @pl.when(out_live & (e_out == n_e - 1))
    def _():
        o_ref[...] = _rms(o_ref[...], g_ref[...])


def peer_experts(ht, u, v, tables, x1, g, tb=512, te=1024):
    d, s = ht.shape
    n_exp = u.shape[0]
    n_i0 = te // PEER_N_KEYS
    n_e = n_exp // te
    n_items = (s // tb) * n_e
    thr, w0, s1, e1 = tables

    def item(gi, lag):
        n = jnp.clip(gi - lag, 0, n_items - 1)
        return n // n_e, n % n_e

    row_spec = pl.BlockSpec((PEER_HEADS, n_i0, tb), lambda gi: (0, item(gi, 1)[1], item(gi, 1)[0]))
    tab_spec = pl.BlockSpec((PEER_HEADS, tb // LANES, PEER_N_KEYS, LANES),
                            lambda gi: (0, item(gi, 1)[0], 0, 0))
    return pl.pallas_call(
        functools.partial(_peer_kernel, te=te, tb=tb, n_e=n_e),
        grid=(n_items + 2,),
        in_specs=[pl.BlockSpec((d, tb), lambda gi: (0, item(gi, 0)[0])),
                  pl.BlockSpec((te, d), lambda gi: (item(gi, 0)[1], 0)),
                  pl.BlockSpec((te, d), lambda gi: (item(gi, 2)[1], 0)),
                  row_spec, row_spec, tab_spec, tab_spec,
                  pl.BlockSpec((tb, d), lambda gi: (item(gi, 2)[0], 0),
                               pipeline_mode=pl.Buffered(1)),
                  pl.BlockSpec((1, d), lambda gi: (0, 0))],
        out_specs=pl.BlockSpec((tb, d), lambda gi: (item(gi, 2)[0], 0)),
        out_shape=jax.ShapeDtypeStruct((s, d), F32),
        scratch_shapes=[pltpu.VMEM((tb // LANES, te, LANES), F32),
                        pltpu.VMEM((tb // LANES, te, LANES), F32),
                        pltpu.VMEM((te, tb), BF16),
                        pltpu.VMEM((te, tb), BF16)],
        compiler_params=_params(("arbitrary",)),
        name="peer_experts",
    )(ht, u, v, thr, w0, s1, e1, x1, g.reshape(1, d))


def kernel(x, norm1_g, w_in, q_norm_g, k_norm_g, w_out, norm2_g, peer_w_query,
           peer_sub_keys, peer_u, peer_v, final_norm_g):
    b, s, d = x.shape
    assert b == 1
    depth = w_in.shape[0]
    xs = x.reshape(s, d)
    for l in range(depth):
        proj = in_projection(xs, norm1_g[l], w_in[l])
        mixed_a, (u_bf16, v_bf16, w_o, w_q) = dilated_attention(
            proj, s, to_bf16=(peer_u[l], peer_v[l], w_out[l], peer_w_query[l]))
        qt, kb, vt = gqa_prep(proj, q_norm_g[l], k_norm_g[l], s, col_block=2)
        mixed_b = gqa_attention(qt, kb, vt, s)
        x1, ht = out_projection(mixed_a, mixed_b, w_o, xs, norm2_g[l])
        keys = peer_sub_keys[l].reshape(2 * PEER_HEADS, PEER_N_KEYS, -1).astype(BF16)
        tables = peer_router(ht, w_q, keys)
        assert l == depth - 1
        xs = peer_experts(ht, u_bf16, v_bf16, tables, x1, final_norm_g)
    return xs.reshape(b, s, d)
```

```python
import functools

import numpy as np
import jax
import jax.numpy as jnp
from jax import lax
from jax.experimental import pallas as pl
from jax.experimental.pallas import tpu as pltpu

F32 = jnp.float32
BF16 = jnp.bfloat16

HEAD_DIM = 128
A_HEADS = 8
DIL_PATTERNS = ((128, 1), (512, 4), (2048, 16))
B_Q_HEADS = 8
B_KV_HEADS = 2
GRID_W = 64
ROPE_THETA = 10000.0
NORM_EPS = 1e-6
PEER_HEADS = 8
PEER_N_KEYS = 128
PEER_TOPK = 16

LANES = 128
SUBLANES = 8
BF16_ROWS = 16
NEG_BIG = -1e30
LOG2_E = 1.4426950408889634
VMEM_LIMIT = 56 * 1024 * 1024


def _params(sem, vmem=VMEM_LIMIT):
    return pltpu.CompilerParams(dimension_semantics=sem, vmem_limit_bytes=vmem)


def _rms(x, g):
    ms = jnp.mean(x * x, axis=-1, keepdims=True)
    return x * lax.rsqrt(ms + NORM_EPS) * g


def _inproj_kernel(x_ref, g_ref, w_ref, o_ref, h_ref):
    @pl.when(pl.program_id(1) == 0)
    def _():
        h_ref[...] = _rms(x_ref[...], g_ref[...]).astype(BF16)

    o_ref[...] = jnp.dot(h_ref[...], w_ref[...].astype(BF16), preferred_element_type=F32)


def in_projection(x, g, w, tm=1024, tn=512):
    s, d = x.shape
    n = w.shape[1]
    return pl.pallas_call(
        _inproj_kernel,
        grid=(s // tm, n // tn),
        in_specs=[pl.BlockSpec((tm, d), lambda i, j: (i, 0)),
                  pl.BlockSpec((1, d), lambda i, j: (0, 0)),
                  pl.BlockSpec((d, tn), lambda i, j: (0, j))],
        out_specs=pl.BlockSpec((tm, tn), lambda i, j: (i, j)),
        out_shape=jax.ShapeDtypeStruct((s, n), F32),
        scratch_shapes=[pltpu.VMEM((tm, d), BF16)],
        compiler_params=_params(("parallel", "arbitrary")),
        name="in_projection",
    )(x, g.reshape(1, d), w)


DIL_QBLK = 2048
DIL_SUB = 128
DIL_KWIN = 256
DIL_UNROLL = 16
DIL_EDGE_CASES = 3


def _dilated_kernel(q_ref, k_ref, v_ref, *rest, seq, n_cast):
    cast_src, rest = rest[:n_cast], rest[n_cast:]
    o_ref, cast_dst, rest = rest[0], rest[1:1 + n_cast], rest[1 + n_cast:]
    m_ref, l_ref, acc_ref, bias_ref = rest
    for src, dst in zip(cast_src, cast_dst):
        dst[...] = src[...].astype(dst.dtype)

    h = pl.program_id(0)
    base = pl.program_id(1) * DIL_QBLK
    slope = jnp.exp2(-(h + 1).astype(F32)) * LOG2_E
    scale = HEAD_DIM ** -0.5 * LOG2_E

    m_ref[...] = jnp.full(m_ref.shape, NEG_BIG, F32)
    l_ref[...] = jnp.zeros(l_ref.shape, F32)
    acc_ref[...] = jnp.zeros(acc_ref.shape, F32)

    row = lax.broadcasted_iota(jnp.int32, (DIL_SUB, DIL_KWIN), 0)
    col = lax.broadcasted_iota(jnp.int32, (DIL_SUB, DIL_KWIN), 1)
    col_minus_row = col - row

    for b_idx, (w, d) in enumerate(DIL_PATTERNS):
        half = (w // 2) // d
        assert half * 2 + DIL_SUB == DIL_KWIN
        seg_len = seq // d

        for v_idx in range(DIL_EDGE_CASES):
            rel = jnp.abs(col_minus_row - v_idx * half)
            bias_ref[b_idx * DIL_EDGE_CASES + v_idx] = jnp.where(
                rel <= half, -(slope * d) * rel.astype(F32), NEG_BIG)

        def rows(start, n, d=d):
            return pl.ds(start, n, stride=d) if d > 1 else pl.ds(start, n)

        def body(c, carry, d=d, half=half, seg_len=seg_len, rows=rows, b_idx=b_idx):
            r = c % d
            cc = c // d
            p0 = base // d + cc * DIL_SUB
            kp = jnp.clip(p0 - half, 0, seg_len - DIL_KWIN)
            lq = r + d * (cc * DIL_SUB)
            ks = r + d * kp

            q = (q_ref[rows(lq, DIL_SUB), :] * scale).astype(BF16)
            k = k_ref[rows(ks, DIL_KWIN), :].astype(BF16)
            v = v_ref[rows(ks, DIL_KWIN), :].astype(BF16)
            s = lax.dot_general(q, k, (((1,), (1,)), ((), ())), preferred_element_type=F32)
            s = s + bias_ref[b_idx * DIL_EDGE_CASES + (p0 - kp) // half]
            mb = jnp.max(s, axis=1, keepdims=True)
            p = jnp.exp2(s - mb)
            lb = jnp.sum(p, axis=1, keepdims=True)
            ob = jnp.dot(p.astype(BF16), v, preferred_element_type=F32)

            idx = rows(lq, DIL_SUB)
            m_old = m_ref[idx, :]
            m_new = jnp.maximum(m_old, mb)
            a_old = jnp.exp2(m_old - m_new)
            a_blk = jnp.exp2(mb - m_new)
            m_ref[idx, :] = m_new
            l_ref[idx, :] = a_old * l_ref[idx, :] + a_blk * lb
            acc_ref[idx, :] = a_old * acc_ref[idx, :] + a_blk * ob
            return carry

        def group(cg, carry, body=body):
            for u in range(DIL_UNROLL):
                body(cg * DIL_UNROLL + u, carry)
            return carry

        lax.fori_loop(0, DIL_QBLK // DIL_SUB // DIL_UNROLL, group, 0)

    o_ref[...] = (acc_ref[...] / l_ref[...]).astype(o_ref.dtype)


def dilated_attention(proj, seq, to_bf16=()):
    assert seq % DIL_QBLK == 0
    for _, d in DIL_PATTERNS:
        assert DIL_QBLK // d >= DIL_SUB and seq // d >= DIL_KWIN
    nh = A_HEADS
    n_qb = seq // DIL_QBLK
    n_steps = nh * n_qb
    cast_specs = []
    for w in to_bf16:
        assert w.shape[0] % n_steps == 0
        cast_specs.append(pl.BlockSpec((w.shape[0] // n_steps, w.shape[1]),
                                       lambda h, i: (h * n_qb + i, 0)))
    outs = pl.pallas_call(
        functools.partial(_dilated_kernel, seq=seq, n_cast=len(to_bf16)),
        grid=(nh, n_qb),
        in_specs=[pl.BlockSpec((DIL_QBLK, HEAD_DIM), lambda h, i: (i, h)),
                  pl.BlockSpec((seq, HEAD_DIM), lambda h, i: (0, nh + h)),
                  pl.BlockSpec((seq, HEAD_DIM), lambda h, i: (0, 2 * nh + h))] + cast_specs,
        out_specs=[pl.BlockSpec((DIL_QBLK, HEAD_DIM), lambda h, i: (i, h))] + cast_specs,
        out_shape=[jax.ShapeDtypeStruct((seq, nh * HEAD_DIM), BF16)]
        + [jax.ShapeDtypeStruct(w.shape, BF16) for w in to_bf16],
        scratch_shapes=[pltpu.VMEM((DIL_QBLK, LANES), F32),
                        pltpu.VMEM((DIL_QBLK, LANES), F32),
                        pltpu.VMEM((DIL_QBLK, HEAD_DIM), F32),
                        pltpu.VMEM((len(DIL_PATTERNS) * DIL_EDGE_CASES, DIL_SUB, DIL_KWIN), F32)],
        compiler_params=_params(("parallel", "arbitrary")),
        name="dilated_attention",
    )(proj, proj, proj, *to_bf16)
    return outs[0], outs[1:]


def _rope_tables(seq):
    rows = seq // GRID_W
    row = np.repeat(np.arange(rows, dtype=np.float64), GRID_W)
    col = np.tile(np.arange(GRID_W, dtype=np.float64), rows)
    half = HEAD_DIM // 2
    inv = ROPE_THETA ** (-np.arange(0, half, 2, dtype=np.float64) / half)
    ang = np.concatenate([row[:, None] * inv, col[:, None] * inv], axis=-1)
    cos = np.repeat(np.cos(ang), 2, axis=-1).astype(np.float32)
    sign = np.tile(np.asarray([-1.0, 1.0]), half)
    sin = (np.repeat(np.sin(ang), 2, axis=-1) * sign).astype(np.float32)
    return cos, sin


def _gqa_prep_kernel(p_ref, cos_ref, sin_ref, qg_ref, kg_ref, qt_ref, k_ref, vt_ref):
    cos = cos_ref[...]
    sin = sin_ref[...]
    even = (lax.broadcasted_iota(jnp.int32, cos.shape, 1) % 2) == 0

    def norm_rope(xh, g):
        y = _rms(xh, g)
        partner = jnp.where(even, pltpu.roll(y, LANES - 1, 1), pltpu.roll(y, 1, 1))
        return y * cos + partner * sin

    scale = HEAD_DIM ** -0.5 * LOG2_E
    for hq in range(B_Q_HEADS):
        xh = p_ref[:, hq * HEAD_DIM:(hq + 1) * HEAD_DIM]
        y = norm_rope(xh, qg_ref[...]) * scale
        qt_ref[hq * HEAD_DIM:(hq + 1) * HEAD_DIM, :] = y.T.astype(BF16)
    k0 = B_Q_HEADS * HEAD_DIM
    v0 = k0 + B_KV_HEADS * HEAD_DIM
    for hk in range(B_KV_HEADS):
        xh = p_ref[:, k0 + hk * HEAD_DIM:k0 + (hk + 1) * HEAD_DIM]
        k_ref[:, hk * HEAD_DIM:(hk + 1) * HEAD_DIM] = norm_rope(xh, kg_ref[...]).astype(BF16)
        vh = p_ref[:, v0 + hk * HEAD_DIM:v0 + (hk + 1) * HEAD_DIM]
        vt_ref[hk * HEAD_DIM:(hk + 1) * HEAD_DIM, :] = vh.T.astype(BF16)


def gqa_prep(proj, q_g, k_g, seq, col_block, ts=512):
    wb = (B_Q_HEADS + 2 * B_KV_HEADS) * HEAD_DIM
    cos, sin = _rope_tables(seq)
    return pl.pallas_call(
        _gqa_prep_kernel,
        grid=(seq // ts,),
        in_specs=[pl.BlockSpec((ts, wb), lambda i: (i, col_block)),
                  pl.BlockSpec((ts, HEAD_DIM), lambda i: (i, 0)),
                  pl.BlockSpec((ts, HEAD_DIM), lambda i: (i, 0)),
                  pl.BlockSpec((1, HEAD_DIM), lambda i: (0, 0)),
                  pl.BlockSpec((1, HEAD_DIM), lambda i: (0, 0))],
        out_specs=[pl.BlockSpec((B_Q_HEADS * HEAD_DIM, ts), lambda i: (0, i)),
                   pl.BlockSpec((ts, B_KV_HEADS * HEAD_DIM), lambda i: (i, 0)),
                   pl.BlockSpec((B_KV_HEADS * HEAD_DIM, ts), lambda i: (0, i))],
        out_shape=[jax.ShapeDtypeStruct((B_Q_HEADS * HEAD_DIM, seq), BF16),
                   jax.ShapeDtypeStruct((seq, B_KV_HEADS * HEAD_DIM), BF16),
                   jax.ShapeDtypeStruct((B_KV_HEADS * HEAD_DIM, seq), BF16)],
        compiler_params=_params(("parallel",)),
        name="gqa_prep",
    )(proj, cos, sin, q_g.reshape(1, HEAD_DIM), k_g.reshape(1, HEAD_DIM))


STAGE_ROWS = 32


def _data_dependent_zero(tiles):
    bits = None
    for t in tiles:
        b = pltpu.bitcast(t, jnp.uint32)
        bits = b if bits is None else bits | b
    zero_words = (bits >> 16) >> 16
    return zero_words.astype(jnp.int32).astype(F32).astype(BF16)


def _gqa_kernel(qt_ref, k_ref, vt_ref, o_ref, m_ref, acc_ref, st_ref, ks_ref, *, seq, tk, nh, ng):
    n_chunks = seq // tk
    tq = qt_ref.shape[1]
    m_ref[...] = jnp.full(m_ref.shape, NEG_BIG, F32)
    acc_ref[...] = jnp.zeros(acc_ref.shape, F32)

    def q_head(g, hh):
        row = (g * nh + hh) * HEAD_DIM
        return qt_ref[row:row + HEAD_DIM, :]

    def keys(g, start, n):
        return k_ref[pl.ds(start, n), g * HEAD_DIM:(g + 1) * HEAD_DIM]

    def chunk_start(c):
        return pl.multiple_of(jnp.minimum(c, n_chunks - 1) * tk, tk)

    for g in range(ng):
        st_ref[g] = jnp.dot(keys(g, 0, tk), q_head(g, 0), preferred_element_type=F32)

    def item(g, hh, st, here, nxt, vt):
        idx = g * nh + hh
        m_old = m_ref[idx]
        m_new = jnp.maximum(m_old, jnp.max(st, axis=0, keepdims=True))
        alpha = jnp.exp2(m_old - m_new)
        pb = jnp.exp2(st - m_new).astype(BF16)

        src = here if hh + 1 < nh else nxt
        slot = ks_ref.at[g, hh % 2]
        for r in range(0, tk, STAGE_ROWS):
            strip = [pb[r:r + STAGE_ROWS, j:j + LANES] for j in range(0, tq, LANES)]
            zero = _data_dependent_zero(strip)
            piece = keys(g, pl.multiple_of(src + r, STAGE_ROWS), STAGE_ROWS)
            slot[r:r + STAGE_ROWS, :] = piece + pltpu.repeat(zero, STAGE_ROWS // zero.shape[0], axis=0)
        st_next = jnp.dot(slot[...], q_head(g, (hh + 1) % nh), preferred_element_type=F32)

        acc_ref[idx] = alpha * acc_ref[idx] + jnp.dot(vt, pb, preferred_element_type=F32)
        m_ref[idx] = m_new
        return st_next

    def body(c, carry):
        here = chunk_start(c)
        nxt = chunk_start(c + 1)
        ones = jnp.ones((BF16_ROWS, tk), BF16)
        vts = [jnp.concatenate([vt_ref[g * HEAD_DIM:(g + 1) * HEAD_DIM, pl.ds(here, tk)], ones], axis=0)
               for g in range(ng)]
        sts = [st_ref[g] for g in range(ng)]
        for hh in range(nh):
            for g in range(ng):
                sts[g] = item(g, hh, sts[g], here, nxt, vts[g])
        for g in range(ng):
            st_ref[g] = sts[g]
        return carry

    lax.fori_loop(0, n_chunks, body, 0)
    for idx in range(ng * nh):
        acc = acc_ref[idx]
        out = acc[:HEAD_DIM] / acc[HEAD_DIM:HEAD_DIM + 1]
        o_ref[:, idx * HEAD_DIM:(idx + 1) * HEAD_DIM] = out.T.astype(o_ref.dtype)


def gqa_attention(qt, k, vt, seq, tq=512, tk=512):
    ng = B_KV_HEADS
    nh = B_Q_HEADS // B_KV_HEADS
    return pl.pallas_call(
        functools.partial(_gqa_kernel, seq=seq, tk=tk, nh=nh, ng=ng),
        grid=(seq // tq,),
        in_specs=[pl.BlockSpec((ng * nh * HEAD_DIM, tq), lambda i: (0, i)),
                  pl.BlockSpec((seq, ng * HEAD_DIM), lambda i: (0, 0)),
                  pl.BlockSpec((ng * HEAD_DIM, seq), lambda i: (0, 0))],
        out_specs=pl.BlockSpec((tq, ng * nh * HEAD_DIM), lambda i: (i, 0)),
        out_shape=jax.ShapeDtypeStruct((seq, B_Q_HEADS * HEAD_DIM), BF16),
        scratch_shapes=[pltpu.VMEM((ng * nh, 1, tq), F32),
                        pltpu.VMEM((ng * nh, HEAD_DIM + BF16_ROWS, tq), F32),
                        pltpu.VMEM((ng, tk, tq), F32),
                        pltpu.VMEM((ng, 2, tk, HEAD_DIM), BF16)],
        compiler_params=_params(("parallel",)),
        name="gqa_attention",
    )(qt, k, vt)


def _outproj_kernel(ma_ref, mb_ref, wa_ref, wb_ref, x_ref, g_ref, x1_ref, ht_ref):
    y = x_ref[...]
    y = y + jnp.dot(ma_ref[...], wa_ref[...], preferred_element_type=F32)
    y = y + jnp.dot(mb_ref[...], wb_ref[...], preferred_element_type=F32)
    x1_ref[...] = y
    ht_ref[...] = _rms(y, g_ref[...]).T.astype(BF16)


def out_projection(mixed_a, mixed_b, w, x, g, tm=512):
    s, d = x.shape
    wa, wb = mixed_a.shape[1], mixed_b.shape[1]
    assert wa == wb and w.shape[0] == wa + wb
    return pl.pallas_call(
        _outproj_kernel,
        grid=(s // tm,),
        in_specs=[pl.BlockSpec((tm, wa), lambda i: (i, 0)),
                  pl.BlockSpec((tm, wb), lambda i: (i, 0)),
                  pl.BlockSpec((wa, d), lambda i: (0, 0)),
                  pl.BlockSpec((wb, d), lambda i: (1, 0)),
                  pl.BlockSpec((tm, d), lambda i: (i, 0)),
                  pl.BlockSpec((1, d), lambda i: (0, 0))],
        out_specs=[pl.BlockSpec((tm, d), lambda i: (i, 0)),
                   pl.BlockSpec((d, tm), lambda i: (0, i))],
        out_shape=[jax.ShapeDtypeStruct((s, d), F32),
                   jax.ShapeDtypeStruct((d, s), BF16)],
        compiler_params=_params(("parallel",)),
        name="out_projection",
    )(mixed_a, mixed_b, w, w, x, g.reshape(1, d))


N_RANK = PEER_TOPK + 1
RANK_ROWS = 24
CAND_COUNTS = tuple(N_RANK // (a + 1) for a in range(N_RANK))
CAND_ROWS = 8 * (1 << (-(-sum(CAND_COUNTS) // 8) - 1).bit_length())
ROUTER_UNROLL = 8


def _odd_even_merge_sort_pairs(n):
    pairs = []

    def merge(lo, hi, r):
        step = r * 2
        if step < hi - lo:
            merge(lo, hi, step)
            merge(lo + r, hi, step)
            pairs.extend((i, i + r) for i in range(lo + r, hi - r, step))
        else:
            pairs.append((lo, lo + r))

    def sort(lo, hi):
        if hi - lo >= 1:
            mid = lo + (hi - lo) // 2
            sort(lo, mid)
            sort(mid + 1, hi)
            merge(lo, hi, 1)

    sort(0, n - 1)
    return pairs


def _extract_sorted_tile(tile, n, emit):
    depth = tile.shape[0] // SUBLANES
    lists = [tile[SUBLANES * j:SUBLANES * (j + 1)] for j in range(depth)]
    for i, j in _odd_even_merge_sort_pairs(depth):
        lists[i], lists[j] = jnp.maximum(lists[i], lists[j]), jnp.minimum(lists[i], lists[j])
    sub_idx = lax.broadcasted_iota(jnp.int32, lists[0].shape, 0).astype(F32)
    for r in range(n):
        mj = jnp.max(lists[0], axis=0, keepdims=True)
        emit(r, mj)
        tied = lists[0] == mj
        first = jnp.min(jnp.where(tied, sub_idx, float(SUBLANES)), axis=0, keepdims=True)
        took = sub_idx == first
        for j in range(min(depth, n - (r + 1))):
            below = lists[j + 1] if j + 1 < depth else -jnp.inf
            lists[j] = jnp.where(took, below, lists[j])


def _router_kernel(ht_ref, wq_ref, keys_ref, thr_ref, w0_ref, s1_ref, e1_ref,
                   q_scr, s_scr, top_scr, cand_scr, best_scr, *, tb):
    q_scr[...] = lax.dot_general(wq_ref[...], ht_ref[...], (((0,), (0,)), ((), ())),
                                 preferred_element_type=F32).astype(BF16)
    lane_groups = [slice(lg * LANES, (lg + 1) * LANES) for lg in range(tb // LANES)]

    def half_body(hc):
        qhc = q_scr[pl.ds(pl.multiple_of(hc * PEER_N_KEYS, PEER_N_KEYS), PEER_N_KEYS), :]
        s = jnp.dot(keys_ref[hc], qhc, preferred_element_type=F32)
        s_scr[hc] = s
        for ls in lane_groups:
            def emit(j, mj, ls=ls):
                top_scr[hc, pl.ds(j, 1), ls] = mj

            _extract_sorted_tile(s[:, ls], N_RANK, emit)

    def half_group(i, carry):
        for u in range(ROUTER_UNROLL):
            half_body(i * ROUTER_UNROLL + u)
        return carry

    lax.fori_loop(0, 2 * PEER_HEADS // ROUTER_UNROLL, half_group, 0)

    def head_body(h, carry):
        t0 = top_scr[2 * h]
        t1 = top_scr[2 * h + 1]
        cand_scr[...] = jnp.full(cand_scr.shape, -jnp.inf, F32)
        off = 0
        for a, nb in enumerate(CAND_COUNTS):
            cand_scr[off:off + nb, :] = t0[a:a + 1, :] + t1[0:nb, :]
            off += nb

        for ls in lane_groups:
            def emit(j, mj, ls=ls):
                best_scr[pl.ds(j, 1), ls] = mj

            _extract_sorted_tile(cand_scr[:, ls], N_RANK, emit)
        best = [best_scr[j:j + 1, :] for j in range(N_RANK)]
        theta = 0.5 * (best[PEER_TOPK - 1] + best[PEER_TOPK])
        z = jnp.zeros_like(theta)
        for j in range(PEER_TOPK):
            z = z + jnp.exp(best[j] - best[0])

        s0 = s_scr[2 * h]
        s1 = s_scr[2 * h + 1]
        thr_ref[h] = theta - s0
        w0_ref[h] = jnp.exp(s0 - t0[0:1, :]) * (0.5 / z)
        e1 = jnp.exp(s1 - t1[0:1, :])
        for lg, ls in enumerate(lane_groups):
            s1_ref[h, lg] = s1[:, ls]
            e1_ref[h, lg] = e1[:, ls]
        return carry

    lax.fori_loop(0, PEER_HEADS, head_body, 0)


def peer_router(ht, wq, keys, tb=512):
    d, s = ht.shape
    nq = wq.shape[1]
    tab = jax.ShapeDtypeStruct((PEER_HEADS, PEER_N_KEYS, s), F32)
    tab_spec = pl.BlockSpec((PEER_HEADS, PEER_N_KEYS, tb), lambda i: (0, 0, i))
    tile = jax.ShapeDtypeStruct((PEER_HEADS, s // LANES, PEER_N_KEYS, LANES), F32)
    tile_spec = pl.BlockSpec((PEER_HEADS, tb // LANES, PEER_N_KEYS, LANES), lambda i: (0, i, 0, 0))
    return pl.pallas_call(
        functools.partial(_router_kernel, tb=tb),
        grid=(s // tb,),
        in_specs=[pl.BlockSpec((d, tb), lambda i: (0, i)),
                  pl.BlockSpec((d, nq), lambda i: (0, 0)),
                  pl.BlockSpec(keys.shape, lambda i: (0, 0, 0))],
        out_specs=[tab_spec, tab_spec, tile_spec, tile_spec],
        out_shape=[tab, tab, tile, tile],
        scratch_shapes=[pltpu.VMEM((nq, tb), BF16),
                        pltpu.VMEM((2 * PEER_HEADS, PEER_N_KEYS, tb), F32),
                        pltpu.VMEM((2 * PEER_HEADS, RANK_ROWS, tb), F32),
                        pltpu.VMEM((CAND_ROWS, tb), F32),
                        pltpu.VMEM((RANK_ROWS, tb), F32)],
        compiler_params=_params(("parallel",)),
        name="peer_router",
    )(ht, wq, keys)


GELU_C = 0.7978845608028654


def _gelu_tanh_x2(x):
    u = x * (GELU_C + (GELU_C * 0.044715) * (x * x))
    return x + x * jnp.tanh(u)


PEER_SUB = 1024
GATE_ROWS = 32


def _peer_stage(ht_ref, u_ref, v_ref, thr_ref, w0_ref, s1_ref, e1_ref, o_ref,
                at_w, at_r, pt_w, pt_r, *, te, tb):
    d = o_ref.shape[1]
    n_sub = te // PEER_SUB
    col_w = d // n_sub

    def hidden_piece(j):
        rows = slice(j * PEER_SUB, (j + 1) * PEER_SUB)
        at = jnp.dot(u_ref[rows, :], ht_ref[...], preferred_element_type=F32)
        for lg in range(tb // LANES):
            at_w[lg, rows, :] = at[:, lg * LANES:(lg + 1) * LANES]

    def value_piece(c):
        cols = slice(c * col_w, (c + 1) * col_w)
        o_ref[:, cols] += lax.dot_general(pt_r[...], v_ref[:, cols], (((0,), (0,)), ((), ())),
                                          preferred_element_type=F32)

    n_i0 = te // PEER_N_KEYS
    n_q = PEER_N_KEYS // GATE_ROWS

    def gate_piece(lg, q):
        ls = slice(lg * LANES, (lg + 1) * LANES)
        qs = slice(q * GATE_ROWS, (q + 1) * GATE_ROWS)
        gates = [None] * n_i0
        for h in range(PEER_HEADS):
            s1 = s1_ref[h, lg, qs, :]
            e1 = e1_ref[h, lg, qs, :]
            for i0 in range(n_i0):
                thr = thr_ref[h, i0:i0 + 1, ls]
                w0 = w0_ref[h, i0:i0 + 1, ls]
                term = jnp.where(s1 >= thr, e1 * w0, 0.0)
                gates[i0] = term if gates[i0] is None else gates[i0] + term
        for i0 in range(n_i0):
            er = slice(i0 * PEER_N_KEYS + q * GATE_ROWS, i0 * PEER_N_KEYS + (q + 1) * GATE_ROWS)
            pt_w[er, ls] = (_gelu_tanh_x2(at_r[lg, er, :]) * gates[i0]).astype(BF16)

    mxu_pieces = [functools.partial(hidden_piece, j) for j in range(n_sub)]
    mxu_pieces += [functools.partial(value_piece, c) for c in range(n_sub)]
    vec_pieces = [(lg, q) for lg in range(tb // LANES) for q in range(n_q)]
    assert len(vec_pieces) % len(mxu_pieces) == 0
    per_mxu = len(vec_pieces) // len(mxu_pieces)
    for n, piece in enumerate(mxu_pieces):
        for lg, q in vec_pieces[n * per_mxu:(n + 1) * per_mxu]:
            gate_piece(lg, q)
        piece()


def _peer_kernel(ht_ref, u_ref, v_ref, thr_ref, w0_ref, s1_ref, e1_ref, x1_ref, g_ref,
                 o_ref, at0_ref, at1_ref, pt0_ref, pt1_ref, *, te, tb, n_e):
    g = pl.program_id(0)
    e_out = (g - 2) % n_e
    out_live = g >= 2
    at_refs = (at0_ref, at1_ref)
    pt_refs = (pt0_ref, pt1_ref)

    @pl.when(g == 0)
    def _():
        for r in at_refs + pt_refs:
            r[...] = jnp.zeros(r.shape, r.dtype)

    @pl.when((g == 0) | (out_live & (e_out == 0)))
    def _():
        o_ref[...] = x1_ref[...]

    for parity in range(2):
        @pl.when(g % 2 == parity)
        def _(parity=parity):
            _peer_stage(ht_ref, u_ref, v_ref, thr_ref, w0_ref, s1_ref, e1_ref, o_ref,
                        at_refs[parity], at_refs[1 - parity],
                        pt_refs[1 - parity], pt_refs[parity], te=te, tb=tb)

    @pl.when(out_live & (e_out == n_e - 1))
    def _():
        o_ref[...] = _rms(o_ref[...], g_ref[...])


def peer_experts(ht, u, v, tables, x1, g, tb=512, te=1024):
    d, s = ht.shape
    n_exp = u.shape[0]
    n_i0 = te // PEER_N_KEYS
    n_e = n_exp // te
    n_items = (s // tb) * n_e
    thr, w0, s1, e1 = tables

    def item(gi, lag):
        n = jnp.clip(gi - lag, 0, n_items - 1)
        return n // n_e, n % n_e

    row_spec = pl.BlockSpec((PEER_HEADS, n_i0, tb), lambda gi: (0, item(gi, 1)[1], item(gi, 1)[0]))
    tab_spec = pl.BlockSpec((PEER_HEADS, tb // LANES, PEER_N_KEYS, LANES),
                            lambda gi: (0, item(gi, 1)[0], 0, 0))
    return pl.pallas_call(
        functools.partial(_peer_kernel, te=te, tb=tb, n_e=n_e),
        grid=(n_items + 2,),
        in_specs=[pl.BlockSpec((d, tb), lambda gi: (0, item(gi, 0)[0])),
                  pl.BlockSpec((te, d), lambda gi: (item(gi, 0)[1], 0)),
                  pl.BlockSpec((te, d), lambda gi: (item(gi, 2)[1], 0)),
                  row_spec, row_spec, tab_spec, tab_spec,
                  pl.BlockSpec((tb, d), lambda gi: (item(gi, 2)[0], 0),
                               pipeline_mode=pl.Buffered(1)),
                  pl.BlockSpec((1, d), lambda gi: (0, 0))],
        out_specs=pl.BlockSpec((tb, d), lambda gi: (item(gi, 2)[0], 0)),
        out_shape=jax.ShapeDtypeStruct((s, d), F32),
        scratch_shapes=[pltpu.VMEM((tb // LANES, te, LANES), F32),
                        pltpu.VMEM((tb // LANES, te, LANES), F32),
                        pltpu.VMEM((te, tb), BF16),
                        pltpu.VMEM((te, tb), BF16)],
        compiler_params=_params(("arbitrary",)),
        name="peer_experts",
    )(ht, u, v, thr, w0, s1, e1, x1, g.reshape(1, d))


def kernel(x, norm1_g, w_in, q_norm_g, k_norm_g, w_out, norm2_g, peer_w_query,
           peer_sub_keys, peer_u, peer_v, final_norm_g):
    b, s, d = x.shape
    assert b == 1
    depth = w_in.shape[0]
    xs = x.reshape(s, d)
    for l in range(depth):
        proj = in_projection(xs, norm1_g[l], w_in[l])
        mixed_a, (u_bf16, v_bf16, w_o, w_q) = dilated_attention(
            proj, s, to_bf16=(peer_u[l], peer_v[l], w_out[l], peer_w_query[l]))
        qt, kb, vt = gqa_prep(proj, q_norm_g[l], k_norm_g[l], s, col_block=2)
        mixed_b = gqa_attention(qt, kb, vt, s)
        x1, ht = out_projection(mixed_a, mixed_b, w_o, xs, norm2_g[l])
        keys = peer_sub_keys[l].reshape(2 * PEER_HEADS, PEER_N_KEYS, -1).astype(BF16)
        tables = peer_router(ht, w_q, keys)
        assert l == depth - 1
        xs = peer_experts(ht, u_bf16, v_bf16, tables, x1, final_norm_g)
    return xs.reshape(b, s, d)
```

```python
import functools

import numpy as np
import jax
import jax.numpy as jnp
from jax import lax
from jax.experimental import pallas as pl
from jax.experimental.pallas import tpu as pltpu

F32 = jnp.float32
BF16 = jnp.bfloat16

HEAD_DIM = 128
A_HEADS = 8
DIL_PATTERNS = ((128, 1), (512, 4), (2048, 16))
B_Q_HEADS = 8
B_KV_HEADS = 2
GRID_W = 64
ROPE_THETA = 10000.0
NORM_EPS = 1e-6
PEER_HEADS = 8
PEER_N_KEYS = 128
PEER_TOPK = 16

LANES = 128
SUBLANES = 8
BF16_ROWS = 16
NEG_BIG = -1e30
LOG2_E = 1.4426950408889634
VMEM_LIMIT = 56 * 1024 * 1024


def _params(sem, vmem=VMEM_LIMIT):
    return pltpu.CompilerParams(dimension_semantics=sem, vmem_limit_bytes=vmem)


def _rms(x, g):
    ms = jnp.mean(x * x, axis=-1, keepdims=True)
    return x * lax.rsqrt(ms + NORM_EPS) * g


def _inproj_kernel(x_ref, g_ref, w_ref, o_ref, h_ref):
    @pl.when(pl.program_id(1) == 0)
    def _():
        h_ref[...] = _rms(x_ref[...], g_ref[...]).astype(BF16)

    o_ref[...] = jnp.dot(h_ref[...], w_ref[...].astype(BF16), preferred_element_type=F32)


def in_projection(x, g, w, tm=1024, tn=512):
    s, d = x.shape
    n = w.shape[1]
    return pl.pallas_call(
        _inproj_kernel,
        grid=(s // tm, n // tn),
        in_specs=[pl.BlockSpec((tm, d), lambda i, j: (i, 0)),
                  pl.BlockSpec((1, d), lambda i, j: (0, 0)),
                  pl.BlockSpec((d, tn), lambda i, j: (0, j))],
        out_specs=pl.BlockSpec((tm, tn), lambda i, j: (i, j)),
        out_shape=jax.ShapeDtypeStruct((s, n), F32),
        scratch_shapes=[pltpu.VMEM((tm, d), BF16)],
        compiler_params=_params(("parallel", "arbitrary")),
        name="in_projection",
    )(x, g.reshape(1, d), w)


DIL_QBLK = 2048
DIL_SUB = 128
DIL_KWIN = 256
DIL_UNROLL = 16
DIL_EDGE_CASES = 3


def _dilated_kernel(q_ref, k_ref, v_ref, *rest, seq, n_cast):
    cast_src, rest = rest[:n_cast], rest[n_cast:]
    o_ref, cast_dst, rest = rest[0], rest[1:1 + n_cast], rest[1 + n_cast:]
    m_ref, l_ref, acc_ref, bias_ref = rest
    for src, dst in zip(cast_src, cast_dst):
        dst[...] = src[...].astype(dst.dtype)

    h = pl.program_id(0)
    base = pl.program_id(1) * DIL_QBLK
    slope = jnp.exp2(-(h + 1).astype(F32)) * LOG2_E
    scale = HEAD_DIM ** -0.5 * LOG2_E

    m_ref[...] = jnp.full(m_ref.shape, NEG_BIG, F32)
    l_ref[...] = jnp.zeros(l_ref.shape, F32)
    acc_ref[...] = jnp.zeros(acc_ref.shape, F32)

    row = lax.broadcasted_iota(jnp.int32, (DIL_SUB, DIL_KWIN), 0)
    col = lax.broadcasted_iota(jnp.int32, (DIL_SUB, DIL_KWIN), 1)
    col_minus_row = col - row

    for b_idx, (w, d) in enumerate(DIL_PATTERNS):
        half = (w // 2) // d
        assert half * 2 + DIL_SUB == DIL_KWIN
        seg_len = seq // d

        for v_idx in range(DIL_EDGE_CASES):
            rel = jnp.abs(col_minus_row - v_idx * half)
            bias_ref[b_idx * DIL_EDGE_CASES + v_idx] = jnp.where(
                rel <= half, -(slope * d) * rel.astype(F32), NEG_BIG)

        def rows(start, n, d=d):
            return pl.ds(start, n, stride=d) if d > 1 else pl.ds(start, n)

        def body(c, carry, d=d, half=half, seg_len=seg_len, rows=rows, b_idx=b_idx):
            r = c % d
            cc = c // d
            p0 = base // d + cc * DIL_SUB
            kp = jnp.clip(p0 - half, 0, seg_len - DIL_KWIN)
            lq = r + d * (cc * DIL_SUB)
            ks = r + d * kp

            q = (q_ref[rows(lq, DIL_SUB), :] * scale).astype(BF16)
            k = k_ref[rows(ks, DIL_KWIN), :].astype(BF16)
            v = v_ref[rows(ks, DIL_KWIN), :].astype(BF16)
            s = lax.dot_general(q, k, (((1,), (1,)), ((), ())), preferred_element_type=F32)
            s = s + bias_ref[b_idx * DIL_EDGE_CASES + (p0 - kp) // half]
            mb = jnp.max(s, axis=1, keepdims=True)
            p = jnp.exp2(s - mb)
            lb = jnp.sum(p, axis=1, keepdims=True)
            ob = jnp.dot(p.astype(BF16), v, preferred_element_type=F32)

            idx = rows(lq, DIL_SUB)
            m_old = m_ref[idx, :]
            m_new = jnp.maximum(m_old, mb)
            a_old = jnp.exp2(m_old - m_new)
            a_blk = jnp.exp2(mb - m_new)
            m_ref[idx, :] = m_new
            l_ref[idx, :] = a_old * l_ref[idx, :] + a_blk * lb
            acc_ref[idx, :] = a_old * acc_ref[idx, :] + a_blk * ob
            return carry

        def group(cg, carry, body=body):
            for u in range(DIL_UNROLL):
                body(cg * DIL_UNROLL + u, carry)
            return carry

        lax.fori_loop(0, DIL_QBLK // DIL_SUB // DIL_UNROLL, group, 0)

    o_ref[...] = (acc_ref[...] / l_ref[...]).astype(o_ref.dtype)


def dilated_attention(proj, seq, to_bf16=()):
    assert seq % DIL_QBLK == 0
    for _, d in DIL_PATTERNS:
        assert DIL_QBLK // d >= DIL_SUB and seq // d >= DIL_KWIN
    nh = A_HEADS
    n_qb = seq // DIL_QBLK
    n_steps = nh * n_qb
    cast_specs = []
    for w in to_bf16:
        assert w.shape[0] % n_steps == 0
        cast_specs.append(pl.BlockSpec((w.shape[0] // n_steps, w.shape[1]),
                                       lambda h, i: (h * n_qb + i, 0)))
    outs = pl.pallas_call(
        functools.partial(_dilated_kernel, seq=seq, n_cast=len(to_bf16)),
        grid=(nh, n_qb),
        in_specs=[pl.BlockSpec((DIL_QBLK, HEAD_DIM), lambda h, i: (i, h)),
                  pl.BlockSpec((seq, HEAD_DIM), lambda h, i: (0, nh + h)),
                  pl.BlockSpec((seq, HEAD_DIM), lambda h, i: (0, 2 * nh + h))] + cast_specs,
        out_specs=[pl.BlockSpec((DIL_QBLK, HEAD_DIM), lambda h, i: (i, h))] + cast_specs,
        out_shape=[jax.ShapeDtypeStruct((seq, nh * HEAD_DIM), BF16)]
        + [jax.ShapeDtypeStruct(w.shape, BF16) for w in to_bf16],
        scratch_shapes=[pltpu.VMEM((DIL_QBLK, LANES), F32),
                        pltpu.VMEM((DIL_QBLK, LANES), F32),
                        pltpu.VMEM((DIL_QBLK, HEAD_DIM), F32),
                        pltpu.VMEM((len(DIL_PATTERNS) * DIL_EDGE_CASES, DIL_SUB, DIL_KWIN), F32)],
        compiler_params=_params(("parallel", "arbitrary")),
        name="dilated_attention",
    )(proj, proj, proj, *to_bf16)
    return outs[0], outs[1:]


def _rope_tables(seq):
    rows = seq // GRID_W
    row = np.repeat(np.arange(rows, dtype=np.float64), GRID_W)
    col = np.tile(np.arange(GRID_W, dtype=np.float64), rows)
    half = HEAD_DIM // 2
    inv = ROPE_THETA ** (-np.arange(0, half, 2, dtype=np.float64) / half)
    ang = np.concatenate([row[:, None] * inv, col[:, None] * inv], axis=-1)
    cos = np.repeat(np.cos(ang), 2, axis=-1).astype(np.float32)
    sign = np.tile(np.asarray([-1.0, 1.0]), half)
    sin = (np.repeat(np.sin(ang), 2, axis=-1) * sign).astype(np.float32)
    return cos, sin


def _gqa_prep_kernel(p_ref, cos_ref, sin_ref, qg_ref, kg_ref, qt_ref, k_ref, vt_ref):
    cos = cos_ref[...]
    sin = sin_ref[...]
    even = (lax.broadcasted_iota(jnp.int32, cos.shape, 1) % 2) == 0

    def norm_rope(xh, g):
        y = _rms(xh, g)
        partner = jnp.where(even, pltpu.roll(y, LANES - 1, 1), pltpu.roll(y, 1, 1))
        return y * cos + partner * sin

    scale = HEAD_DIM ** -0.5 * LOG2_E
    for hq in range(B_Q_HEADS):
        xh = p_ref[:, hq * HEAD_DIM:(hq + 1) * HEAD_DIM]
        y = norm_rope(xh, qg_ref[...]) * scale
        qt_ref[hq * HEAD_DIM:(hq + 1) * HEAD_DIM, :] = y.T.astype(BF16)
    k0 = B_Q_HEADS * HEAD_DIM
    v0 = k0 + B_KV_HEADS * HEAD_DIM
    for hk in range(B_KV_HEADS):
        xh = p_ref[:, k0 + hk * HEAD_DIM:k0 + (hk + 1) * HEAD_DIM]
        k_ref[:, hk * HEAD_DIM:(hk + 1) * HEAD_DIM] = norm_rope(xh, kg_ref[...]).astype(BF16)
        vh = p_ref[:, v0 + hk * HEAD_DIM:v0 + (hk + 1) * HEAD_DIM]
        vt_ref[hk * HEAD_DIM:(hk + 1) * HEAD_DIM, :] = vh.T.astype(BF16)


def gqa_prep(proj, q_g, k_g, seq, col_block, ts=512):
    wb = (B_Q_HEADS + 2 * B_KV_HEADS) * HEAD_DIM
    cos, sin = _rope_tables(seq)
    return pl.pallas_call(
        _gqa_prep_kernel,
        grid=(seq // ts,),
        in_specs=[pl.BlockSpec((ts, wb), lambda i: (i, col_block)),
                  pl.BlockSpec((ts, HEAD_DIM), lambda i: (i, 0)),
                  pl.BlockSpec((ts, HEAD_DIM), lambda i: (i, 0)),
                  pl.BlockSpec((1, HEAD_DIM), lambda i: (0, 0)),
                  pl.BlockSpec((1, HEAD_DIM), lambda i: (0, 0))],
        out_specs=[pl.BlockSpec((B_Q_HEADS * HEAD_DIM, ts), lambda i: (0, i)),
                   pl.BlockSpec((ts, B_KV_HEADS * HEAD_DIM), lambda i: (i, 0)),
                   pl.BlockSpec((B_KV_HEADS * HEAD_DIM, ts), lambda i: (0, i))],
        out_shape=[jax.ShapeDtypeStruct((B_Q_HEADS * HEAD_DIM, seq), BF16),
                   jax.ShapeDtypeStruct((seq, B_KV_HEADS * HEAD_DIM), BF16),
                   jax.ShapeDtypeStruct((B_KV_HEADS * HEAD_DIM, seq), BF16)],
        compiler_params=_params(("parallel",)),
        name="gqa_prep",
    )(proj, cos, sin, q_g.reshape(1, HEAD_DIM), k_g.reshape(1, HEAD_DIM))


STAGE_ROWS = 32


def _data_dependent_zero(tiles):
    bits = None
    for t in tiles:
        b = pltpu.bitcast(t, jnp.uint32)
        bits = b if bits is None else bits | b
    zero_words = (bits >> 16) >> 16
    return zero_words.astype(jnp.int32).astype(F32).astype(BF16)


def _gqa_kernel(qt_ref, k_ref, vt_ref, o_ref, m_ref, acc_ref, st_ref, ks_ref, *, seq, tk, nh, ng):
    n_chunks = seq // tk
    tq = qt_ref.shape[1]
    m_ref[...] = jnp.full(m_ref.shape, NEG_BIG, F32)
    acc_ref[...] = jnp.zeros(acc_ref.shape, F32)

    def q_head(g, hh):
        row = (g * nh + hh) * HEAD_DIM
        return qt_ref[row:row + HEAD_DIM, :]

    def keys(g, start, n):
        return k_ref[pl.ds(start, n), g * HEAD_DIM:(g + 1) * HEAD_DIM]

    def chunk_start(c):
        return pl.multiple_of(jnp.minimum(c, n_chunks - 1) * tk, tk)

    for g in range(ng):
        st_ref[g] = jnp.dot(keys(g, 0, tk), q_head(g, 0), preferred_element_type=F32)

    def item(g, hh, st, here, nxt, vt):
        idx = g * nh + hh
        m_old = m_ref[idx]
        m_new = jnp.maximum(m_old, jnp.max(st, axis=0, keepdims=True))
        alpha = jnp.exp2(m_old - m_new)
        pb = jnp.exp2(st - m_new).astype(BF16)

        src = here if hh + 1 < nh else nxt
        slot = ks_ref.at[g, hh % 2]
        for r in range(0, tk, STAGE_ROWS):
            strip = [pb[r:r + STAGE_ROWS, j:j + LANES] for j in range(0, tq, LANES)]
            zero = _data_dependent_zero(strip)
            piece = keys(g, pl.multiple_of(src + r, STAGE_ROWS), STAGE_ROWS)
            slot[r:r + STAGE_ROWS, :] = piece + pltpu.repeat(zero, STAGE_ROWS // zero.shape[0], axis=0)
        st_next = jnp.dot(slot[...], q_head(g, (hh + 1) % nh), preferred_element_type=F32)

        acc_ref[idx] = alpha * acc_ref[idx] + jnp.dot(vt, pb, preferred_element_type=F32)
        m_ref[idx] = m_new
        return st_next

    def body(c, carry):
        here = chunk_start(c)
        nxt = chunk_start(c + 1)
        ones = jnp.ones((BF16_ROWS, tk), BF16)
        vts = [jnp.concatenate([vt_ref[g * HEAD_DIM:(g + 1) * HEAD_DIM, pl.ds(here, tk)], ones], axis=0)
               for g in range(ng)]
        sts = [st_ref[g] for g in range(ng)]
        for hh in range(nh):
            for g in range(ng):
                sts[g] = item(g, hh, sts[g], here, nxt, vts[g])
        for g in range(ng):
            st_ref[g] = sts[g]
        return carry

    lax.fori_loop(0, n_chunks, body, 0)
    for idx in range(ng * nh):
        acc = acc_ref[idx]
        out = acc[:HEAD_DIM] / acc[HEAD_DIM:HEAD_DIM + 1]
        o_ref[:, idx * HEAD_DIM:(idx + 1) * HEAD_DIM] = out.T.astype(o_ref.dtype)


def gqa_attention(qt, k, vt, seq, tq=512, tk=512):
    ng = B_KV_HEADS
    nh = B_Q_HEADS // B_KV_HEADS
    return pl.pallas_call(
        functools.partial(_gqa_kernel, seq=seq, tk=tk, nh=nh, ng=ng),
        grid=(seq // tq,),
        in_specs=[pl.BlockSpec((ng * nh * HEAD_DIM, tq), lambda i: (0, i)),
                  pl.BlockSpec((seq, ng * HEAD_DIM), lambda i: (0, 0)),
                  pl.BlockSpec((ng * HEAD_DIM, seq), lambda i: (0, 0))],
        out_specs=pl.BlockSpec((tq, ng * nh * HEAD_DIM), lambda i: (i, 0)),
        out_shape=jax.ShapeDtypeStruct((seq, B_Q_HEADS * HEAD_DIM), BF16),
        scratch_shapes=[pltpu.VMEM((ng * nh, 1, tq), F32),
                        pltpu.VMEM((ng * nh, HEAD_DIM + BF16_ROWS, tq), F32),
                        pltpu.VMEM((ng, tk, tq), F32),
                        pltpu.VMEM((ng, 2, tk, HEAD_DIM), BF16)],
        compiler_params=_params(("parallel",)),
        name="gqa_attention",
    )(qt, k, vt)


def _outproj_kernel(ma_ref, mb_ref, wa_ref, wb_ref, x_ref, g_ref, x1_ref, ht_ref):
    y = x_ref[...]
    y = y + jnp.dot(ma_ref[...], wa_ref[...], preferred_element_type=F32)
    y = y + jnp.dot(mb_ref[...], wb_ref[...], preferred_element_type=F32)
    x1_ref[...] = y
    ht_ref[...] = _rms(y, g_ref[...]).T.astype(BF16)


def out_projection(mixed_a, mixed_b, w, x, g, tm=512):
    s, d = x.shape
    wa, wb = mixed_a.shape[1], mixed_b.shape[1]
    assert wa == wb and w.shape[0] == wa + wb
    return pl.pallas_call(
        _outproj_kernel,
        grid=(s // tm,),
        in_specs=[pl.BlockSpec((tm, wa), lambda i: (i, 0)),
                  pl.BlockSpec((tm, wb), lambda i: (i, 0)),
                  pl.BlockSpec((wa, d), lambda i: (0, 0)),
                  pl.BlockSpec((wb, d), lambda i: (1, 0)),
                  pl.BlockSpec((tm, d), lambda i: (i, 0)),
                  pl.BlockSpec((1, d), lambda i: (0, 0))],
        out_specs=[pl.BlockSpec((tm, d), lambda i: (i, 0)),
                   pl.BlockSpec((d, tm), lambda i: (0, i))],
        out_shape=[jax.ShapeDtypeStruct((s, d), F32),
                   jax.ShapeDtypeStruct((d, s), BF16)],
        compiler_params=_params(("parallel",)),
        name="out_projection",
    )(mixed_a, mixed_b, w, w, x, g.reshape(1, d))


N_RANK = PEER_TOPK + 1
RANK_ROWS = 24
CAND_COUNTS = tuple(N_RANK // (a + 1) for a in range(N_RANK))
CAND_ROWS = 8 * (1 << (-(-sum(CAND_COUNTS) // 8) - 1).bit_length())
ROUTER_UNROLL = 8


def _odd_even_merge_sort_pairs(n):
    pairs = []

    def merge(lo, hi, r):
        step = r * 2
        if step < hi - lo:
            merge(lo, hi, step)
            merge(lo + r, hi, step)
            pairs.extend((i, i + r) for i in range(lo + r, hi - r, step))
        else:
            pairs.append((lo, lo + r))

    def sort(lo, hi):
        if hi - lo >= 1:
            mid = lo + (hi - lo) // 2
            sort(lo, mid)
            sort(mid + 1, hi)
            merge(lo, hi, 1)

    sort(0, n - 1)
    return pairs


def _extract_sorted_tile(tile, n, emit):
    depth = tile.shape[0] // SUBLANES
    lists = [tile[SUBLANES * j:SUBLANES * (j + 1)] for j in range(depth)]
    for i, j in _odd_even_merge_sort_pairs(depth):
        lists[i], lists[j] = jnp.maximum(lists[i], lists[j]), jnp.minimum(lists[i], lists[j])
    sub_idx = lax.broadcasted_iota(jnp.int32, lists[0].shape, 0).astype(F32)
    for r in range(n):
        mj = jnp.max(lists[0], axis=0, keepdims=True)
        emit(r, mj)
        tied = lists[0] == mj
        first = jnp.min(jnp.where(tied, sub_idx, float(SUBLANES)), axis=0, keepdims=True)
        took = sub_idx == first
        for j in range(min(depth, n - (r + 1))):
            below = lists[j + 1] if j + 1 < depth else -jnp.inf
            lists[j] = jnp.where(took, below, lists[j])


def _router_kernel(ht_ref, wq_ref, keys_ref, thr_ref, w0_ref, s1_ref, e1_ref,
                   q_scr, s_scr, top_scr, cand_scr, best_scr, *, tb):
    q_scr[...] = lax.dot_general(wq_ref[...], ht_ref[...], (((0,), (0,)), ((), ())),
                                 preferred_element_type=F32).astype(BF16)
    lane_groups = [slice(lg * LANES, (lg + 1) * LANES) for lg in range(tb // LANES)]

    def half_body(hc):
        qhc = q_scr[pl.ds(pl.multiple_of(hc * PEER_N_KEYS, PEER_N_KEYS), PEER_N_KEYS), :]
        s = jnp.dot(keys_ref[hc], qhc, preferred_element_type=F32)
        s_scr[hc] = s
        for ls in lane_groups:
            def emit(j, mj, ls=ls):
                top_scr[hc, pl.ds(j, 1), ls] = mj

            _extract_sorted_tile(s[:, ls], N_RANK, emit)

    def half_group(i, carry):
        for u in range(ROUTER_UNROLL):
            half_body(i * ROUTER_UNROLL + u)
        return carry

    lax.fori_loop(0, 2 * PEER_HEADS // ROUTER_UNROLL, half_group, 0)

    def head_body(h, carry):
        t0 = top_scr[2 * h]
        t1 = top_scr[2 * h + 1]
        cand_scr[...] = jnp.full(cand_scr.shape, -jnp.inf, F32)
        off = 0
        for a, nb in enumerate(CAND_COUNTS):
            cand_scr[off:off + nb, :] = t0[a:a + 1, :] + t1[0:nb, :]
            off += nb

        for ls in lane_groups:
            def emit(j, mj, ls=ls):
                best_scr[pl.ds(j, 1), ls] = mj

            _extract_sorted_tile(cand_scr[:, ls], N_RANK, emit)
        best = [best_scr[j:j + 1, :] for j in range(N_RANK)]
        theta = 0.5 * (best[PEER_TOPK - 1] + best[PEER_TOPK])
        z = jnp.zeros_like(theta)
        for j in range(PEER_TOPK):
            z = z + jnp.exp(best[j] - best[0])

        s0 = s_scr[2 * h]
        s1 = s_scr[2 * h + 1]
        thr_ref[h] = theta - s0
        w0_ref[h] = jnp.exp(s0 - t0[0:1, :]) * (0.5 / z)
        e1 = jnp.exp(s1 - t1[0:1, :])
        for lg, ls in enumerate(lane_groups):
            s1_ref[h, lg] = s1[:, ls]
            e1_ref[h, lg] = e1[:, ls]
        return carry

    lax.fori_loop(0, PEER_HEADS, head_body, 0)


def peer_router(ht, wq, keys, tb=512):
    d, s = ht.shape
    nq = wq.shape[1]
    tab = jax.ShapeDtypeStruct((PEER_HEADS, PEER_N_KEYS, s), F32)
    tab_spec = pl.BlockSpec((PEER_HEADS, PEER_N_KEYS, tb), lambda i: (0, 0, i))
    tile = jax.ShapeDtypeStruct((PEER_HEADS, s // LANES, PEER_N_KEYS, LANES), F32)
    tile_spec = pl.BlockSpec((PEER_HEADS, tb // LANES, PEER_N_KEYS, LANES), lambda i: (0, i, 0, 0))
    return pl.pallas_call(
        functools.partial(_router_kernel, tb=tb),
        grid=(s // tb,),
        in_specs=[pl.BlockSpec((d, tb), lambda i: (0, i)),
                  pl.BlockSpec((d, nq), lambda i: (0, 0)),
                  pl.BlockSpec(keys.shape, lambda i: (0, 0, 0))],
        out_specs=[tab_spec, tab_spec, tile_spec, tile_spec],
        out_shape=[tab, tab, tile, tile],
        scratch_shapes=[pltpu.VMEM((nq, tb), BF16),
                        pltpu.VMEM((2 * PEER_HEADS, PEER_N_KEYS, tb), F32),
                        pltpu.VMEM((2 * PEER_HEADS, RANK_ROWS, tb), F32),
                        pltpu.VMEM((CAND_ROWS, tb), F32),
                        pltpu.VMEM((RANK_ROWS, tb), F32)],
        compiler_params=_params(("parallel",)),
        name="peer_router",
    )(ht, wq, keys)


GELU_C = 0.7978845608028654


def _gelu_tanh_x2(x):
    u = x * (GELU_C + (GELU_C * 0.044715) * (x * x))
    return x + x * jnp.tanh(u)


PEER_SUB = 1024
GATE_ROWS = 32


def _peer_stage(ht_ref, u_ref, v_ref, thr_ref, w0_ref, s1_ref, e1_ref, acc_t,
                at_w, at_r, pt_w, pt_r, *, te, tb):
    d = acc_t.shape[0]
    n_sub = te // PEER_SUB
    col_w = d // n_sub

    def hidden_piece(j):
        rows = slice(j * PEER_SUB, (j + 1) * PEER_SUB)
        at = jnp.dot(u_ref[rows, :], ht_ref[...], preferred_element_type=F32)
        for lg in range(tb // LANES):
            at_w[lg, rows, :] = at[:, lg * LANES:(lg + 1) * LANES]

    def value_piece(c):
        cols = slice(c * col_w, (c + 1) * col_w)
        acc_t[cols, :] += lax.dot_general(v_ref[:, cols], pt_r[...], (((0,), (0,)), ((), ())),
                                          preferred_element_type=F32)

    n_i0 = te // PEER_N_KEYS
    n_q = PEER_N_KEYS // GATE_ROWS

    def gate_piece(lg, q):
        ls = slice(lg * LANES, (lg + 1) * LANES)
        qs = slice(q * GATE_ROWS, (q + 1) * GATE_ROWS)
        gates = [None] * n_i0
        for h in range(PEER_HEADS):
            s1 = s1_ref[h, lg, qs, :]
            e1 = e1_ref[h, lg, qs, :]
            for i0 in range(n_i0):
                thr = thr_ref[h, i0:i0 + 1, ls]
                w0 = w0_ref[h, i0:i0 + 1, ls]
                term = jnp.where(s1 >= thr, e1 * w0, 0.0)
                gates[i0] = term if gates[i0] is None else gates[i0] + term
        for i0 in range(n_i0):
            er = slice(i0 * PEER_N_KEYS + q * GATE_ROWS, i0 * PEER_N_KEYS + (q + 1) * GATE_ROWS)
            pt_w[er, ls] = (_gelu_tanh_x2(at_r[lg, er, :]) * gates[i0]).astype(BF16)

    mxu_pieces = [functools.partial(hidden_piece, j) for j in range(n_sub)]
    mxu_pieces += [functools.partial(value_piece, c) for c in range(n_sub)]
    vec_pieces = [(lg, q) for lg in range(tb // LANES) for q in range(n_q)]
    assert len(vec_pieces) % len(mxu_pieces) == 0
    per_mxu = len(vec_pieces) // len(mxu_pieces)
    for n, piece in enumerate(mxu_pieces):
        for lg, q in vec_pieces[n * per_mxu:(n + 1) * per_mxu]:
            gate_piece(lg, q)
        piece()


def _peer_kernel(ht_ref, u_ref, v_ref, thr_ref, w0_ref, s1_ref, e1_ref, x1_ref, g_ref,
                 o_ref, at0_ref, at1_ref, pt0_ref, pt1_ref, acc_t, *, te, tb, n_e):
    g = pl.program_id(0)
    e_out = (g - 2) % n_e
    out_live = g >= 2
    at_refs = (at0_ref, at1_ref)
    pt_refs = (pt0_ref, pt1_ref)

    @pl.when(g == 0)
    def _():
        for r in at_refs + pt_refs:
            r[...] = jnp.zeros(r.shape, r.dtype)

    @pl.when((g == 0) | (out_live & (e_out == 0)))
    def _():
        acc_t[...] = jnp.zeros(acc_t.shape, F32)

    for parity in range(2):
        @pl.when(g % 2 == parity)
        def _(parity=parity):
            _peer_stage(ht_ref, u_ref, v_ref, thr_ref, w0_ref, s1_ref, e1_ref, acc_t,
                        at_refs[parity], at_refs[1 - parity],
                        pt_refs[1 - parity], pt_refs[parity], te=te, tb=tb)

    @pl.when(out_live & (e_out == n_e - 1))
    def _():
        o_ref[...] = _rms(x1_ref[...] + acc_t[...].T, g_ref[...])


def peer_experts(ht, u, v, tables, x1, g, tb=512, te=1024):
    d, s = ht.shape
    n_exp = u.shape[0]
    n_i0 = te // PEER_N_KEYS
    n_e = n_exp // te
    n_items = (s // tb) * n_e
    thr, w0, s1, e1 = tables

    def item(gi, lag):
        n = jnp.clip(gi - lag, 0, n_items - 1)
        return n // n_e, n % n_e

    row_spec = pl.BlockSpec((PEER_HEADS, n_i0, tb), lambda gi: (0, item(gi, 1)[1], item(gi, 1)[0]))
    tab_spec = pl.BlockSpec((PEER_HEADS, tb // LANES, PEER_N_KEYS, LANES),
                            lambda gi: (0, item(gi, 1)[0], 0, 0))
    return pl.pallas_call(
        functools.partial(_peer_kernel, te=te, tb=tb, n_e=n_e),
        grid=(n_items + 2,),
        in_specs=[pl.BlockSpec((d, tb), lambda gi: (0, item(gi, 0)[0])),
                  pl.BlockSpec((te, d), lambda gi: (item(gi, 0)[1], 0)),
                  pl.BlockSpec((te, d), lambda gi: (item(gi, 2)[1], 0)),
                  row_spec, row_spec, tab_spec, tab_spec,
                  pl.BlockSpec((tb, d), lambda gi: (item(gi, 2)[0], 0),
                               pipeline_mode=pl.Buffered(1)),
                  pl.BlockSpec((1, d), lambda gi: (0, 0))],
        out_specs=pl.BlockSpec((tb, d), lambda gi: (item(gi, 2)[0], 0)),
        out_shape=jax.ShapeDtypeStruct((s, d), F32),
        scratch_shapes=[pltpu.VMEM((tb // LANES, te, LANES), F32),
                        pltpu.VMEM((tb // LANES, te, LANES), F32),
                        pltpu.VMEM((te, tb), BF16),
                        pltpu.VMEM((te, tb), BF16),
                        pltpu.VMEM((d, tb), F32)],
        compiler_params=_params(("arbitrary",)),
        name="peer_experts",
    )(ht, u, v, thr, w0, s1, e1, x1, g.reshape(1, d))


def kernel(x, norm1_g, w_in, q_norm_g, k_norm_g, w_out, norm2_g, peer_w_query,
           peer_sub_keys, peer_u, peer_v, final_norm_g):
    b, s, d = x.shape
    assert b == 1
    depth = w_in.shape[0]
    xs = x.reshape(s, d)
    for l in range(depth):
        proj = in_projection(xs, norm1_g[l], w_in[l])
        mixed_a, (u_bf16, v_bf16, w_o, w_q) = dilated_attention(
            proj, s, to_bf16=(peer_u[l], peer_v[l], w_out[l], peer_w_query[l]))
        qt, kb, vt = gqa_prep(proj, q_norm_g[l], k_norm_g[l], s, col_block=2)
        mixed_b = gqa_attention(qt, kb, vt, s)
        x1, ht = out_projection(mixed_a, mixed_b, w_o, xs, norm2_g[l])
        keys = peer_sub_keys[l].reshape(2 * PEER_HEADS, PEER_N_KEYS, -1).astype(BF16)
        tables = peer_router(ht, w_q, keys)
        assert l == depth - 1
        xs = peer_experts(ht, u_bf16, v_bf16, tables, x1, final_norm_g)
    return xs.reshape(b, s, d)
```

```python
import functools

import numpy as np
import jax
import jax.numpy as jnp
from jax import lax
from jax.experimental import pallas as pl
from jax.experimental.pallas import tpu as pltpu

F32 = jnp.float32
BF16 = jnp.bfloat16

HEAD_DIM = 128
A_HEADS = 8
DIL_PATTERNS = ((128, 1), (512, 4), (2048, 16))
B_Q_HEADS = 8
B_KV_HEADS = 2
GRID_W = 64
ROPE_THETA = 10000.0
NORM_EPS = 1e-6
PEER_HEADS = 8
PEER_N_KEYS = 128
PEER_TOPK = 16

LANES = 128
SUBLANES = 8
BF16_ROWS = 16
NEG_BIG = -1e30
LOG2_E = 1.4426950408889634
VMEM_LIMIT = 56 * 1024 * 1024


def _params(sem, vmem=VMEM_LIMIT):
    return pltpu.CompilerParams(dimension_semantics=sem, vmem_limit_bytes=vmem)


def _rms(x, g):
    ms = jnp.mean(x * x, axis=-1, keepdims=True)
    return x * lax.rsqrt(ms + NORM_EPS) * g


def _inproj_kernel(x_ref, g_ref, w_ref, o_ref, h_ref):
    @pl.when(pl.program_id(1) == 0)
    def _():
        h_ref[...] = _rms(x_ref[...], g_ref[...]).astype(BF16)

    o_ref[...] = jnp.dot(h_ref[...], w_ref[...].astype(BF16), preferred_element_type=F32)


def in_projection(x, g, w, tm=1024, tn=512):
    s, d = x.shape
    n = w.shape[1]
    return pl.pallas_call(
        _inproj_kernel,
        grid=(s // tm, n // tn),
        in_specs=[pl.BlockSpec((tm, d), lambda i, j: (i, 0)),
                  pl.BlockSpec((1, d), lambda i, j: (0, 0)),
                  pl.BlockSpec((d, tn), lambda i, j: (0, j))],
        out_specs=pl.BlockSpec((tm, tn), lambda i, j: (i, j)),
        out_shape=jax.ShapeDtypeStruct((s, n), F32),
        scratch_shapes=[pltpu.VMEM((tm, d), BF16)],
        compiler_params=_params(("parallel", "arbitrary")),
        name="in_projection",
    )(x, g.reshape(1, d), w)


DIL_QBLK = 2048
DIL_SUB = 128
DIL_KWIN = 256
DIL_UNROLL = 16
DIL_EDGE_CASES = 3


def _dilated_kernel(q_ref, k_ref, v_ref, *rest, seq, n_cast):
    cast_src, rest = rest[:n_cast], rest[n_cast:]
    o_ref, cast_dst, rest = rest[0], rest[1:1 + n_cast], rest[1 + n_cast:]
    m_ref, l_ref, acc_ref, bias_ref = rest
    for src, dst in zip(cast_src, cast_dst):
        dst[...] = src[...].astype(dst.dtype)

    h = pl.program_id(0)
    base = pl.program_id(1) * DIL_QBLK
    slope = jnp.exp2(-(h + 1).astype(F32)) * LOG2_E
    scale = HEAD_DIM ** -0.5 * LOG2_E

    m_ref[...] = jnp.full(m_ref.shape, NEG_BIG, F32)
    l_ref[...] = jnp.zeros(l_ref.shape, F32)
    acc_ref[...] = jnp.zeros(acc_ref.shape, F32)

    row = lax.broadcasted_iota(jnp.int32, (DIL_SUB, DIL_KWIN), 0)
    col = lax.broadcasted_iota(jnp.int32, (DIL_SUB, DIL_KWIN), 1)
    col_minus_row = col - row

    for b_idx, (w, d) in enumerate(DIL_PATTERNS):
        half = (w // 2) // d
        assert half * 2 + DIL_SUB == DIL_KWIN
        seg_len = seq // d

        for v_idx in range(DIL_EDGE_CASES):
            rel = jnp.abs(col_minus_row - v_idx * half)
            bias_ref[b_idx * DIL_EDGE_CASES + v_idx] = jnp.where(
                rel <= half, -(slope * d) * rel.astype(F32), NEG_BIG)

        def rows(start, n, d=d):
            return pl.ds(start, n, stride=d) if d > 1 else pl.ds(start, n)

        def body(c, carry, d=d, half=half, seg_len=seg_len, rows=rows, b_idx=b_idx):
            r = c % d
            cc = c // d
            p0 = base // d + cc * DIL_SUB
            kp = jnp.clip(p0 - half, 0, seg_len - DIL_KWIN)
            lq = r + d * (cc * DIL_SUB)
            ks = r + d * kp

            q = (q_ref[rows(lq, DIL_SUB), :] * scale).astype(BF16)
            k = k_ref[rows(ks, DIL_KWIN), :].astype(BF16)
            v = v_ref[rows(ks, DIL_KWIN), :].astype(BF16)
            s = lax.dot_general(q, k, (((1,), (1,)), ((), ())), preferred_element_type=F32)
            s = s + bias_ref[b_idx * DIL_EDGE_CASES + (p0 - kp) // half]
            mb = jnp.max(s, axis=1, keepdims=True)
            p = jnp.exp2(s - mb)
            lb = jnp.sum(p, axis=1, keepdims=True)
            ob = jnp.dot(p.astype(BF16), v, preferred_element_type=F32)

            idx = rows(lq, DIL_SUB)
            m_old = m_ref[idx, :]
            m_new = jnp.maximum(m_old, mb)
            a_old = jnp.exp2(m_old - m_new)
            a_blk = jnp.exp2(mb - m_new)
            m_ref[idx, :] = m_new
            l_ref[idx, :] = a_old * l_ref[idx, :] + a_blk * lb
            acc_ref[idx, :] = a_old * acc_ref[idx, :] + a_blk * ob
            return carry

        def group(cg, carry, body=body):
            for u in range(DIL_UNROLL):
                body(cg * DIL_UNROLL + u, carry)
            return carry

        lax.fori_loop(0, DIL_QBLK // DIL_SUB // DIL_UNROLL, group, 0)

    o_ref[...] = (acc_ref[...] / l_ref[...]).astype(o_ref.dtype)


def dilated_attention(proj, seq, to_bf16=()):
    assert seq % DIL_QBLK == 0
    for _, d in DIL_PATTERNS:
        assert DIL_QBLK // d >= DIL_SUB and seq // d >= DIL_KWIN
    nh = A_HEADS
    n_qb = seq // DIL_QBLK
    n_steps = nh * n_qb
    cast_specs = []
    for w in to_bf16:
        assert w.shape[0] % n_steps == 0
        cast_specs.append(pl.BlockSpec((w.shape[0] // n_steps, w.shape[1]),
                                       lambda h, i: (h * n_qb + i, 0)))
    outs = pl.pallas_call(
        functools.partial(_dilated_kernel, seq=seq, n_cast=len(to_bf16)),
        grid=(nh, n_qb),
        in_specs=[pl.BlockSpec((DIL_QBLK, HEAD_DIM), lambda h, i: (i, h)),
                  pl.BlockSpec((seq, HEAD_DIM), lambda h, i: (0, nh + h)),
                  pl.BlockSpec((seq, HEAD_DIM), lambda h, i: (0, 2 * nh + h))] + cast_specs,
        out_specs=[pl.BlockSpec((DIL_QBLK, HEAD_DIM), lambda h, i: (i, h))] + cast_specs,
        out_shape=[jax.ShapeDtypeStruct((seq, nh * HEAD_DIM), BF16)]
        + [jax.ShapeDtypeStruct(w.shape, BF16) for w in to_bf16],
        scratch_shapes=[pltpu.VMEM((DIL_QBLK, LANES), F32),
                        pltpu.VMEM((DIL_QBLK, LANES), F32),
                        pltpu.VMEM((DIL_QBLK, HEAD_DIM), F32),
                        pltpu.VMEM((len(DIL_PATTERNS) * DIL_EDGE_CASES, DIL_SUB, DIL_KWIN), F32)],
        compiler_params=_params(("parallel", "arbitrary")),
        name="dilated_attention",
    )(proj, proj, proj, *to_bf16)
    return outs[0], outs[1:]


def _rope_tables(seq):
    rows = seq // GRID_W
    row = np.repeat(np.arange(rows, dtype=np.float64), GRID_W)
    col = np.tile(np.arange(GRID_W, dtype=np.float64), rows)
    half = HEAD_DIM // 2
    inv = ROPE_THETA ** (-np.arange(0, half, 2, dtype=np.float64) / half)
    ang = np.concatenate([row[:, None] * inv, col[:, None] * inv], axis=-1)
    cos = np.repeat(np.cos(ang), 2, axis=-1).astype(np.float32)
    sign = np.tile(np.asarray([-1.0, 1.0]), half)
    sin = (np.repeat(np.sin(ang), 2, axis=-1) * sign).astype(np.float32)
    return cos, sin


def _gqa_prep_kernel(p_ref, cos_ref, sin_ref, qg_ref, kg_ref, qt_ref, k_ref, vt_ref):
    cos = cos_ref[...]
    sin = sin_ref[...]
    even = (lax.broadcasted_iota(jnp.int32, cos.shape, 1) % 2) == 0

    def norm_rope(xh, g):
        y = _rms(xh, g)
        partner = jnp.where(even, pltpu.roll(y, LANES - 1, 1), pltpu.roll(y, 1, 1))
        return y * cos + partner * sin

    scale = HEAD_DIM ** -0.5 * LOG2_E
    for hq in range(B_Q_HEADS):
        xh = p_ref[:, hq * HEAD_DIM:(hq + 1) * HEAD_DIM]
        y = norm_rope(xh, qg_ref[...]) * scale
        qt_ref[hq * HEAD_DIM:(hq + 1) * HEAD_DIM, :] = y.T.astype(BF16)
    k0 = B_Q_HEADS * HEAD_DIM
    v0 = k0 + B_KV_HEADS * HEAD_DIM
    for hk in range(B_KV_HEADS):
        xh = p_ref[:, k0 + hk * HEAD_DIM:k0 + (hk + 1) * HEAD_DIM]
        k_ref[:, hk * HEAD_DIM:(hk + 1) * HEAD_DIM] = norm_rope(xh, kg_ref[...]).astype(BF16)
        vh = p_ref[:, v0 + hk * HEAD_DIM:v0 + (hk + 1) * HEAD_DIM]
        vt_ref[hk * HEAD_DIM:(hk + 1) * HEAD_DIM, :] = vh.T.astype(BF16)


def gqa_prep(proj, q_g, k_g, seq, col_block, ts=512):
    wb = (B_Q_HEADS + 2 * B_KV_HEADS) * HEAD_DIM
    cos, sin = _rope_tables(seq)
    return pl.pallas_call(
        _gqa_prep_kernel,
        grid=(seq // ts,),
        in_specs=[pl.BlockSpec((ts, wb), lambda i: (i, col_block)),
                  pl.BlockSpec((ts, HEAD_DIM), lambda i: (i, 0)),
                  pl.BlockSpec((ts, HEAD_DIM), lambda i: (i, 0)),
                  pl.BlockSpec((1, HEAD_DIM), lambda i: (0, 0)),
                  pl.BlockSpec((1, HEAD_DIM), lambda i: (0, 0))],
        out_specs=[pl.BlockSpec((B_Q_HEADS * HEAD_DIM, ts), lambda i: (0, i)),
                   pl.BlockSpec((ts, B_KV_HEADS * HEAD_DIM), lambda i: (i, 0)),
                   pl.BlockSpec((B_KV_HEADS * HEAD_DIM, ts), lambda i: (0, i))],
        out_shape=[jax.ShapeDtypeStruct((B_Q_HEADS * HEAD_DIM, seq), BF16),
                   jax.ShapeDtypeStruct((seq, B_KV_HEADS * HEAD_DIM), BF16),
                   jax.ShapeDtypeStruct((B_KV_HEADS * HEAD_DIM, seq), BF16)],
        compiler_params=_params(("parallel",)),
        name="gqa_prep",
    )(proj, cos, sin, q_g.reshape(1, HEAD_DIM), k_g.reshape(1, HEAD_DIM))


STAGE_ROWS = 32


def _data_dependent_zero(tiles):
    bits = None
    for t in tiles:
        b = pltpu.bitcast(t, jnp.uint32)
        bits = b if bits is None else bits | b
    zero_words = (bits >> 16) >> 16
    return zero_words.astype(jnp.int32).astype(F32).astype(BF16)


def _gqa_kernel(qt_ref, k_ref, vt_ref, o_ref, m_ref, acc_ref, st_ref, ks_ref, *, seq, tk, nh, ng, nb):
    n_chunks = seq // tk
    tq = qt_ref.shape[1] // nb
    m_ref[...] = jnp.full(m_ref.shape, NEG_BIG, F32)
    acc_ref[...] = jnp.zeros(acc_ref.shape, F32)

    def q_head(ch, hh):
        g, b = ch % ng, ch // ng
        row = (g * nh + hh) * HEAD_DIM
        return qt_ref[row:row + HEAD_DIM, b * tq:(b + 1) * tq]

    def keys(g, start, n):
        return k_ref[pl.ds(start, n), g * HEAD_DIM:(g + 1) * HEAD_DIM]

    def chunk_start(c):
        return pl.multiple_of(jnp.minimum(c, n_chunks - 1) * tk, tk)

    for ch in range(ng * nb):
        st_ref[ch] = jnp.dot(keys(ch % ng, 0, tk), q_head(ch, 0), preferred_element_type=F32)

    def item(ch, hh, st, here, nxt, vt):
        g = ch % ng
        idx = ch * nh + hh
        m_old = m_ref[idx]
        m_new = jnp.maximum(m_old, jnp.max(st, axis=0, keepdims=True))
        alpha = jnp.exp2(m_old - m_new)
        pb = jnp.exp2(st - m_new).astype(BF16)

        src = here if hh + 1 < nh else nxt
        slot = ks_ref.at[ch, hh % 2]
        for r in range(0, tk, STAGE_ROWS):
            strip = [pb[r:r + STAGE_ROWS, j:j + LANES] for j in range(0, tq, LANES)]
            zero = _data_dependent_zero(strip)
            piece = keys(g, pl.multiple_of(src + r, STAGE_ROWS), STAGE_ROWS)
            slot[r:r + STAGE_ROWS, :] = piece + pltpu.repeat(zero, STAGE_ROWS // zero.shape[0], axis=0)
        st_next = jnp.dot(slot[...], q_head(ch, (hh + 1) % nh), preferred_element_type=F32)

        acc_ref[idx] = alpha * acc_ref[idx] + jnp.dot(vt, pb, preferred_element_type=F32)
        m_ref[idx] = m_new
        return st_next

    def body(c, carry):
        here = chunk_start(c)
        nxt = chunk_start(c + 1)
        ones = jnp.ones((BF16_ROWS, tk), BF16)
        vts = [jnp.concatenate([vt_ref[g * HEAD_DIM:(g + 1) * HEAD_DIM, pl.ds(here, tk)], ones], axis=0)
               for g in range(ng)]
        sts = [st_ref[ch] for ch in range(ng * nb)]
        for hh in range(nh):
            for ch in range(ng * nb):
                sts[ch] = item(ch, hh, sts[ch], here, nxt, vts[ch % ng])
        for ch in range(ng * nb):
            st_ref[ch] = sts[ch]
        return carry

    lax.fori_loop(0, n_chunks, body, 0)
    for ch in range(ng * nb):
        g, b = ch % ng, ch // ng
        for hh in range(nh):
            acc = acc_ref[ch * nh + hh]
            out = acc[:HEAD_DIM] / acc[HEAD_DIM:HEAD_DIM + 1]
            col = (g * nh + hh) * HEAD_DIM
            o_ref[b * tq:(b + 1) * tq, col:col + HEAD_DIM] = out.T.astype(o_ref.dtype)


def gqa_attention(qt, k, vt, seq, tq=512, tk=512, nb=2):
    ng = B_KV_HEADS
    nh = B_Q_HEADS // B_KV_HEADS
    return pl.pallas_call(
        functools.partial(_gqa_kernel, seq=seq, tk=tk, nh=nh, ng=ng, nb=nb),
        grid=(seq // (tq * nb),),
        in_specs=[pl.BlockSpec((ng * nh * HEAD_DIM, tq * nb), lambda i: (0, i)),
                  pl.BlockSpec((seq, ng * HEAD_DIM), lambda i: (0, 0)),
                  pl.BlockSpec((ng * HEAD_DIM, seq), lambda i: (0, 0))],
        out_specs=pl.BlockSpec((tq * nb, ng * nh * HEAD_DIM), lambda i: (i, 0)),
        out_shape=jax.ShapeDtypeStruct((seq, B_Q_HEADS * HEAD_DIM), BF16),
        scratch_shapes=[pltpu.VMEM((ng * nb * nh, 1, tq), F32),
                        pltpu.VMEM((ng * nb * nh, HEAD_DIM + BF16_ROWS, tq), F32),
                        pltpu.VMEM((ng * nb, tk, tq), F32),
                        pltpu.VMEM((ng * nb, 2, tk, HEAD_DIM), BF16)],
        compiler_params=_params(("parallel",)),
        name="gqa_attention",
    )(qt, k, vt)


def _outproj_kernel(ma_ref, mb_ref, wa_ref, wb_ref, x_ref, g_ref, x1_ref, ht_ref):
    y = x_ref[...]
    y = y + jnp.dot(ma_ref[...], wa_ref[...], preferred_element_type=F32)
    y = y + jnp.dot(mb_ref[...], wb_ref[...], preferred_element_type=F32)
    x1_ref[...] = y
    ht_ref[...] = _rms(y, g_ref[...]).T.astype(BF16)


def out_projection(mixed_a, mixed_b, w, x, g, tm=512):
    s, d = x.shape
    wa, wb = mixed_a.shape[1], mixed_b.shape[1]
    assert wa == wb and w.shape[0] == wa + wb
    return pl.pallas_call(
        _outproj_kernel,
        grid=(s // tm,),
        in_specs=[pl.BlockSpec((tm, wa), lambda i: (i, 0)),
                  pl.BlockSpec((tm, wb), lambda i: (i, 0)),
                  pl.BlockSpec((wa, d), lambda i: (0, 0)),
                  pl.BlockSpec((wb, d), lambda i: (1, 0)),
                  pl.BlockSpec((tm, d), lambda i: (i, 0)),
                  pl.BlockSpec((1, d), lambda i: (0, 0))],
        out_specs=[pl.BlockSpec((tm, d), lambda i: (i, 0)),
                   pl.BlockSpec((d, tm), lambda i: (0, i))],
        out_shape=[jax.ShapeDtypeStruct((s, d), F32),
                   jax.ShapeDtypeStruct((d, s), BF16)],
        compiler_params=_params(("parallel",)),
        name="out_projection",
    )(mixed_a, mixed_b, w, w, x, g.reshape(1, d))


N_RANK = PEER_TOPK + 1
RANK_ROWS = 24
CAND_COUNTS = tuple(N_RANK // (a + 1) for a in range(N_RANK))
CAND_ROWS = 8 * (1 << (-(-sum(CAND_COUNTS) // 8) - 1).bit_length())
ROUTER_UNROLL = 8


def _odd_even_merge_sort_pairs(n):
    pairs = []

    def merge(lo, hi, r):
        step = r * 2
        if step < hi - lo:
            merge(lo, hi, step)
            merge(lo + r, hi, step)
            pairs.extend((i, i + r) for i in range(lo + r, hi - r, step))
        else:
            pairs.append((lo, lo + r))

    def sort(lo, hi):
        if hi - lo >= 1:
            mid = lo + (hi - lo) // 2
            sort(lo, mid)
            sort(mid + 1, hi)
            merge(lo, hi, 1)

    sort(0, n - 1)
    return pairs


def _extract_sorted_tile(tile, n, emit):
    depth = tile.shape[0] // SUBLANES
    lists = [tile[SUBLANES * j:SUBLANES * (j + 1)] for j in range(depth)]
    for i, j in _odd_even_merge_sort_pairs(depth):
        lists[i], lists[j] = jnp.maximum(lists[i], lists[j]), jnp.minimum(lists[i], lists[j])
    sub_idx = lax.broadcasted_iota(jnp.int32, lists[0].shape, 0).astype(F32)
    for r in range(n):
        mj = jnp.max(lists[0], axis=0, keepdims=True)
        emit(r, mj)
        tied = lists[0] == mj
        first = jnp.min(jnp.where(tied, sub_idx, float(SUBLANES)), axis=0, keepdims=True)
        took = sub_idx == first
        for j in range(min(depth, n - (r + 1))):
            below = lists[j + 1] if j + 1 < depth else -jnp.inf
            lists[j] = jnp.where(took, below, lists[j])


def _router_kernel(ht_ref, wq_ref, keys_ref, thr_ref, w0_ref, s1_ref, e1_ref,
                   q_scr, s_scr, top_scr, cand_scr, best_scr, *, tb):
    q_scr[...] = lax.dot_general(wq_ref[...], ht_ref[...], (((0,), (0,)), ((), ())),
                                 preferred_element_type=F32).astype(BF16)
    lane_groups = [slice(lg * LANES, (lg + 1) * LANES) for lg in range(tb // LANES)]

    def half_body(hc):
        qhc = q_scr[pl.ds(pl.multiple_of(hc * PEER_N_KEYS, PEER_N_KEYS), PEER_N_KEYS), :]
        s = jnp.dot(keys_ref[hc], qhc, preferred_element_type=F32)
        s_scr[hc] = s
        for ls in lane_groups:
            def emit(j, mj, ls=ls):
                top_scr[hc, pl.ds(j, 1), ls] = mj

            _extract_sorted_tile(s[:, ls], N_RANK, emit)

    def half_group(i, carry):
        for u in range(ROUTER_UNROLL):
            half_body(i * ROUTER_UNROLL + u)
        return carry

    lax.fori_loop(0, 2 * PEER_HEADS // ROUTER_UNROLL, half_group, 0)

    def head_body(h, carry):
        t0 = top_scr[2 * h]
        t1 = top_scr[2 * h + 1]
        cand_scr[...] = jnp.full(cand_scr.shape, -jnp.inf, F32)
        off = 0
        for a, nb in enumerate(CAND_COUNTS):
            cand_scr[off:off + nb, :] = t0[a:a + 1, :] + t1[0:nb, :]
            off += nb

        for ls in lane_groups:
            def emit(j, mj, ls=ls):
                best_scr[pl.ds(j, 1), ls] = mj

            _extract_sorted_tile(cand_scr[:, ls], N_RANK, emit)
        best = [best_scr[j:j + 1, :] for j in range(N_RANK)]
        theta = 0.5 * (best[PEER_TOPK - 1] + best[PEER_TOPK])
        z = jnp.zeros_like(theta)
        for j in range(PEER_TOPK):
            z = z + jnp.exp(best[j] - best[0])

        s0 = s_scr[2 * h]
        s1 = s_scr[2 * h + 1]
        thr_ref[h] = theta - s0
        w0_ref[h] = jnp.exp(s0 - t0[0:1, :]) * (0.5 / z)
        e1 = jnp.exp(s1 - t1[0:1, :])
        for lg, ls in enumerate(lane_groups):
            s1_ref[h, lg] = s1[:, ls]
            e1_ref[h, lg] = e1[:, ls]
        return carry

    lax.fori_loop(0, PEER_HEADS, head_body, 0)


def peer_router(ht, wq, keys, tb=512):
    d, s = ht.shape
    nq = wq.shape[1]
    tab = jax.ShapeDtypeStruct((PEER_HEADS, PEER_N_KEYS, s), F32)
    tab_spec = pl.BlockSpec((PEER_HEADS, PEER_N_KEYS, tb), lambda i: (0, 0, i))
    tile = jax.ShapeDtypeStruct((PEER_HEADS, s // LANES, PEER_N_KEYS, LANES), F32)
    tile_spec = pl.BlockSpec((PEER_HEADS, tb // LANES, PEER_N_KEYS, LANES), lambda i: (0, i, 0, 0))
    return pl.pallas_call(
        functools.partial(_router_kernel, tb=tb),
        grid=(s // tb,),
        in_specs=[pl.BlockSpec((d, tb), lambda i: (0, i)),
                  pl.BlockSpec((d, nq), lambda i: (0, 0)),
                  pl.BlockSpec(keys.shape, lambda i: (0, 0, 0))],
        out_specs=[tab_spec, tab_spec, tile_spec, tile_spec],
        out_shape=[tab, tab, tile, tile],
        scratch_shapes=[pltpu.VMEM((nq, tb), BF16),
                        pltpu.VMEM((2 * PEER_HEADS, PEER_N_KEYS, tb), F32),
                        pltpu.VMEM((2 * PEER_HEADS, RANK_ROWS, tb), F32),
                        pltpu.VMEM((CAND_ROWS, tb), F32),
                        pltpu.VMEM((RANK_ROWS, tb), F32)],
        compiler_params=_params(("parallel",)),
        name="peer_router",
    )(ht, wq, keys)


GELU_C = 0.7978845608028654


def _gelu_tanh_x2(x):
    u = x * (GELU_C + (GELU_C * 0.044715) * (x * x))
    return x + x * jnp.tanh(u)


PEER_SUB = 1024
GATE_ROWS = 32


def _peer_stage(ht_ref, u_ref, v_ref, thr_ref, w0_ref, s1_ref, e1_ref, o_ref,
                at_w, at_r, pt_w, pt_r, *, te, tb):
    d = o_ref.shape[1]
    n_sub = te // PEER_SUB
    col_w = d // n_sub

    def hidden_piece(j):
        rows = slice(j * PEER_SUB, (j + 1) * PEER_SUB)
        at = jnp.dot(u_ref[rows, :], ht_ref[...], preferred_element_type=F32)
        for lg in range(tb // LANES):
            at_w[lg, rows, :] = at[:, lg * LANES:(lg + 1) * LANES]

    def value_piece(c):
        cols = slice(c * col_w, (c + 1) * col_w)
        o_ref[:, cols] += lax.dot_general(pt_r[...], v_ref[:, cols], (((0,), (0,)), ((), ())),
                                          preferred_element_type=F32)

    n_i0 = te // PEER_N_KEYS
    n_q = PEER_N_KEYS // GATE_ROWS

    def gate_piece(lg, q):
        ls = slice(lg * LANES, (lg + 1) * LANES)
        qs = slice(q * GATE_ROWS, (q + 1) * GATE_ROWS)
        gates = [None] * n_i0
        for h in range(PEER_HEADS):
            s1 = s1_ref[h, lg, qs, :]
            e1 = e1_ref[h, lg, qs, :]
            for i0 in range(n_i0):
                thr = thr_ref[h, i0:i0 + 1, ls]
                w0 = w0_ref[h, i0:i0 + 1, ls]
                term = jnp.where(s1 >= thr, e1 * w0, 0.0)
                gates[i0] = term if gates[i0] is None else gates[i0] + term
        for i0 in range(n_i0):
            er = slice(i0 * PEER_N_KEYS + q * GATE_ROWS, i0 * PEER_N_KEYS + (q + 1) * GATE_ROWS)
            pt_w[er, ls] = (_gelu_tanh_x2(at_r[lg, er, :]) * gates[i0]).astype(BF16)

    mxu_pieces = [functools.partial(hidden_piece, j) for j in range(n_sub)]
    mxu_pieces += [functools.partial(value_piece, c) for c in range(n_sub)]
    vec_pieces = [(lg, q) for lg in range(tb // LANES) for q in range(n_q)]
    assert len(vec_pieces) % len(mxu_pieces) == 0
    per_mxu = len(vec_pieces) // len(mxu_pieces)
    for n, piece in enumerate(mxu_pieces):
        for lg, q in vec_pieces[n * per_mxu:(n + 1) * per_mxu]:
            gate_piece(lg, q)
        piece()


def _peer_kernel(ht_ref, u_ref, v_ref, thr_ref, w0_ref, s1_ref, e1_ref, x1_ref, g_ref,
                 o_ref, at0_ref, at1_ref, pt0_ref, pt1_ref, *, te, tb, n_e):
    g = pl.program_id(0)
    e_out = (g - 2) % n_e
    out_live = g >= 2
    at_refs = (at0_ref, at1_ref)
    pt_refs = (pt0_ref, pt1_ref)

    @pl.when(g == 0)
    def _():
        for r in at_refs + pt_refs:
            r[...] = jnp.zeros(r.shape, r.dtype)

    @pl.when((g == 0) | (out_live & (e_out == 0)))
    def _():
        o_ref[...] = x1_ref[...]

    for parity in range(2):
        @pl.when(g % 2 == parity)
        def _(parity=parity):
            _peer_stage(ht_ref, u_ref, v_ref, thr_ref, w0_ref, s1_ref, e1_ref, o_ref,
                        at_refs[parity], at_refs[1 - parity],
                        pt_refs[1 - parity], pt_refs[parity], te=te, tb=tb)

    @pl.when(out_live & (e_out == n_e - 1))
    def _():
        o_ref[...] = _rms(o_ref[...], g_ref[...])


def peer_experts(ht, u, v, tables, x1, g, tb=512, te=1024):
    d, s = ht.shape
    n_exp = u.shape[0]
    n_i0 = te // PEER_N_KEYS
    n_e = n_exp // te
    n_items = (s // tb) * n_e
    thr, w0, s1, e1 = tables

    def item(gi, lag):
        n = jnp.clip(gi - lag, 0, n_items - 1)
        return n // n_e, n % n_e

    row_spec = pl.BlockSpec((PEER_HEADS, n_i0, tb), lambda gi: (0, item(gi, 1)[1], item(gi, 1)[0]))
    tab_spec = pl.BlockSpec((PEER_HEADS, tb // LANES, PEER_N_KEYS, LANES),
                            lambda gi: (0, item(gi, 1)[0], 0, 0))
    return pl.pallas_call(
        functools.partial(_peer_kernel, te=te, tb=tb, n_e=n_e),
        grid=(n_items + 2,),
        in_specs=[pl.BlockSpec((d, tb), lambda gi: (0, item(gi, 0)[0])),
                  pl.BlockSpec((te, d), lambda gi: (item(gi, 0)[1], 0)),
                  pl.BlockSpec((te, d), lambda gi: (item(gi, 2)[1], 0)),
                  row_spec, row_spec, tab_spec, tab_spec,
                  pl.BlockSpec((tb, d), lambda gi: (item(gi, 2)[0], 0),
                               pipeline_mode=pl.Buffered(1)),
                  pl.BlockSpec((1, d), lambda gi: (0, 0))],
        out_specs=pl.BlockSpec((tb, d), lambda gi: (item(gi, 2)[0], 0)),
        out_shape=jax.ShapeDtypeStruct((s, d), F32),
        scratch_shapes=[pltpu.VMEM((tb // LANES, te, LANES), F32),
                        pltpu.VMEM((tb // LANES, te, LANES), F32),
                        pltpu.VMEM((te, tb), BF16),
                        pltpu.VMEM((te, tb), BF16)],
        compiler_params=_params(("arbitrary",)),
        name="peer_experts",
    )(ht, u, v, thr, w0, s1, e1, x1, g.reshape(1, d))


def kernel(x, norm1_g, w_in, q_norm_g, k_norm_g, w_out, norm2_g, peer_w_query,
           peer_sub_keys, peer_u, peer_v, final_norm_g):
    b, s, d = x.shape
    assert b == 1
    depth = w_in.shape[0]
    xs = x.reshape(s, d)
    for l in range(depth):
        proj = in_projection(xs, norm1_g[l], w_in[l])
        mixed_a, (u_bf16, v_bf16, w_o, w_q) = dilated_attention(
            proj, s, to_bf16=(peer_u[l], peer_v[l], w_out[l], peer_w_query[l]))
        qt, kb, vt = gqa_prep(proj, q_norm_g[l], k_norm_g[l], s, col_block=2)
        mixed_b = gqa_attention(qt, kb, vt, s)
        x1, ht = out_projection(mixed_a, mixed_b, w_o, xs, norm2_g[l])
        keys = peer_sub_keys[l].reshape(2 * PEER_HEADS, PEER_N_KEYS, -1).astype(BF16)
        tables = peer_router(ht, w_q, keys)
        assert l == depth - 1
        xs = peer_experts(ht, u_bf16, v_bf16, tables, x1, final_norm_g)
    return xs.reshape(b, s, d)
```
